```python
import math
import jax
import jax.numpy as jnp
from jax import lax
import numpy as np

D_MODEL = 1024
BATCH = 2
SEQ = 8192
DEPTH = 2

EPS = 1e-6
CONV_WIDTH = 4
N_BRANCHES = 4
BRANCH_WIDTH = D_MODEL // 2

SSM_HEADS = 8
SSM_HEAD_DIM = BRANCH_WIDTH // SSM_HEADS
SSM_GROUPS = 2
SSM_STATE = 64
SSM_CHUNK = 64
SSM_XBC = BRANCH_WIDTH + 2 * SSM_GROUPS * SSM_STATE

GDN_HEADS = 4
GDN_HEAD_DIM = BRANCH_WIDTH // GDN_HEADS
GDN_CHUNK = 64

GLA_HEADS = 4
GLA_VAL_DIM = BRANCH_WIDTH // GLA_HEADS
GLA_KEY_DIM = GLA_VAL_DIM // 2
GLA_GATE_RANK = 16
GLA_GATE_TAU = 16.0
GLA_CHUNK = 16

LRU_BLOCKS = 8
LRU_BLOCK_DIM = BRANCH_WIDTH // LRU_BLOCKS
LRU_C = 8.0

PEER_HEADS = 8
PEER_N_KEYS = 128
PEER_N_EXPERTS = PEER_N_KEYS * PEER_N_KEYS
PEER_QUERY_DIM = 256
PEER_HALF_DIM = PEER_QUERY_DIM // 2
PEER_TOPK = 16
PEER_TOKEN_BLOCK = 128

IN_SIZES = (
    BRANCH_WIDTH,
    SSM_XBC,
    SSM_HEADS,
    3 * BRANCH_WIDTH,
    BRANCH_WIDTH,
    GDN_HEADS,
    GDN_HEADS,
    GLA_HEADS * GLA_KEY_DIM,
    GLA_HEADS * GLA_KEY_DIM,
    BRANCH_WIDTH,
    BRANCH_WIDTH,
    GLA_GATE_RANK,
    BRANCH_WIDTH,
    BRANCH_WIDTH,
    N_BRANCHES * D_MODEL,
)
D_IN = sum(IN_SIZES)

kernel_name = "hybrid_ssd_gdn_gla_rglru_peer_adaln"


def _split(t, sizes):
    out, o = [], 0
    for s in sizes:
        out.append(t[..., o:o + s])
        o += s
    return out


def _rmsnorm(t, w):
    tf = t.astype(jnp.float32)
    tf = tf * lax.rsqrt(jnp.mean(tf * tf, axis=-1, keepdims=True) + EPS)
    return (tf * w.astype(jnp.float32)).astype(t.dtype)


def _l2norm(t):
    return t * lax.rsqrt(jnp.sum(t * t, axis=-1, keepdims=True) + EPS)


def _causal_dwconv(t, w, b=None):
    k_w, ch = w.shape
    y = lax.conv_general_dilated(t, w[:, None, :].astype(t.dtype), window_strides=(1,),
                                 padding=((k_w - 1, 0),), dimension_numbers=("NWC", "WIO", "NWC"),
                                 feature_group_count=ch)
    return y if b is None else y + b


def _ssd_mixer(z, xbc, dt_raw, conv_w, conv_b, dt_bias, a_log, d_skip, norm_w):
    f32 = jnp.float32
    bsz, T, _ = xbc.shape
    nc, L, H = T // SSM_CHUNK, SSM_CHUNK, SSM_HEADS
    xbc = jax.nn.silu(_causal_dwconv(xbc, conv_w, conv_b)).astype(f32)
    xs, bm, cm = _split(xbc, (BRANCH_WIDTH, SSM_GROUPS * SSM_STATE, SSM_GROUPS * SSM_STATE))
    xs = xs.reshape(bsz, T, H, SSM_HEAD_DIM)
    rep = H // SSM_GROUPS
    bm = jnp.repeat(bm.reshape(bsz, T, SSM_GROUPS, SSM_STATE), rep, axis=2)
    cm = jnp.repeat(cm.reshape(bsz, T, SSM_GROUPS, SSM_STATE), rep, axis=2)
    dt = jax.nn.softplus(dt_raw.astype(f32) + dt_bias.astype(f32))
    da = dt * (-jnp.exp(a_log.astype(f32)))
    shp = (bsz, nc, L)
    xdt = (xs * dt[..., None]).reshape(shp + (H, SSM_HEAD_DIM))
    bc = bm.reshape(shp + (H, SSM_STATE))
    cc = cm.reshape(shp + (H, SSM_STATE))
    cs = jnp.cumsum(da.reshape(shp + (H,)), axis=2)
    causal = jnp.tril(jnp.ones((L, L), bool))[None, None, :, :, None]
    seg = jnp.exp(jnp.where(causal, cs[:, :, :, None, :] - cs[:, :, None, :, :], -jnp.inf))
    y_diag = jnp.einsum("bcijh,bcjhp->bcihp", jnp.einsum("bcihn,bcjhn->bcijh", cc, bc) * seg, xdt)
    states = jnp.einsum("bcjhn,bcjh,bcjhp->bchpn", bc, jnp.exp(cs[:, :, -1:, :] - cs), xdt)
    chunk_decay = jnp.exp(cs[:, :, -1, :])

    def step(s, inp):
        st, dec = inp
        return s * dec[:, :, None, None] + st, s

    s0 = jnp.zeros((bsz, H, SSM_HEAD_DIM, SSM_STATE), f32)
    _, s_in = lax.scan(step, s0, (jnp.moveaxis(states, 1, 0), jnp.moveaxis(chunk_decay, 1, 0)))
    s_in = jnp.moveaxis(s_in, 0, 1)
    y_off = jnp.einsum("bcihn,bchpn,bcih->bcihp", cc, s_in, jnp.exp(cs))
    y = (y_diag + y_off).reshape(bsz, T, H, SSM_HEAD_DIM) + xs * d_skip.astype(f32)[:, None]
    y = y.reshape(bsz, T, BRANCH_WIDTH) * jax.nn.silu(z.astype(f32))
    return _rmsnorm(y, norm_w)


def _gdn_mixer(qkv, z, beta_raw, decay_raw, conv_w, a_log, dt_bias, norm_w):
    f32 = jnp.float32
    bsz, T, _ = qkv.shape
    L, H, dh = GDN_CHUNK, GDN_HEADS, GDN_HEAD_DIM
    nc = T // L
    qkv = jax.nn.silu(_causal_dwconv(qkv, conv_w)).astype(f32)
    q, k, v = [t.reshape(bsz, T, H, dh) for t in _split(qkv, (BRANCH_WIDTH,) * 3)]
    q = _l2norm(q) * (dh ** -0.5)
    k = _l2norm(k)
    beta = jax.nn.sigmoid(beta_raw.astype(f32))
    g = -jnp.exp(a_log.astype(f32)) * jax.nn.softplus(decay_raw.astype(f32) + dt_bias.astype(f32))

    def chunks(t):
        return jnp.moveaxis(t.reshape(bsz, nc, L, H, -1), 3, 1)

    q, k, v = chunks(q), chunks(k), chunks(v)
    beta = chunks(beta[..., None])[..., 0]
    gc = jnp.cumsum(chunks(g[..., None])[..., 0], axis=-1)
    incl = jnp.tril(jnp.ones((L, L), bool))
    strict = jnp.tril(jnp.ones((L, L), bool), k=-1)
    decay = jnp.exp(jnp.where(incl, gc[..., :, None] - gc[..., None, :], -jnp.inf))
    kb = k * beta[..., None]
    m = jnp.where(strict, jnp.einsum("bhcid,bhcjd->bhcij", kb, k) * decay, 0.0)
    rhs = jnp.concatenate([v * beta[..., None], kb * jnp.exp(gc)[..., None]], axis=-1)
    sol = lax.linalg.triangular_solve(m + jnp.eye(L, dtype=f32), rhs, left_side=True,
                                      lower=True, unit_diagonal=True)
    u, w = sol[..., :dh], sol[..., dh:]
    attn = jnp.einsum("bhcid,bhcjd->bhcij", q, k) * decay
    qg = q * jnp.exp(gc)[..., None]
    kd = k * jnp.exp(gc[..., -1:] - gc)[..., None]
    dec = jnp.exp(gc[..., -1])

    def step(s, inp):
        qg_c, kd_c, u_c, w_c, a_c, d_c = inp
        v_new = u_c - w_c @ s
        o = qg_c @ s + a_c @ v_new
        s = s * d_c[..., None, None] + jnp.swapaxes(kd_c, -1, -2) @ v_new
        return s, o

    s0 = jnp.zeros((bsz, H, dh, dh), f32)
    _, o = lax.scan(step, s0, tuple(jnp.moveaxis(t, 2, 0) for t in (qg, kd, u, w, attn, dec)))
    o = o.transpose(1, 0, 3, 2, 4).reshape(bsz, T, H, dh)
    o = _rmsnorm(o, norm_w) * jax.nn.silu(z.astype(f32).reshape(bsz, T, H, dh))
    return o.reshape(bsz, T, BRANCH_WIDTH)


def _gla_mixer(q, k, v, r, gate_low, w_gate, b_gate, norm_w):
    f32 = jnp.float32
    bsz, T, _ = q.shape
    L, H, dk, dv = GLA_CHUNK, GLA_HEADS, GLA_KEY_DIM, GLA_VAL_DIM
    nc = T // L
    log_a = jax.nn.log_sigmoid((gate_low @ w_gate + b_gate).astype(f32)) / GLA_GATE_TAU

    def chunks(t, d):
        return jnp.moveaxis(t.astype(f32).reshape(bsz, nc, L, H, d), 3, 1)

    q = chunks(q, dk) * (dk ** -0.5)
    k = chunks(k, dk)
    v = chunks(v, dv)
    G = jnp.cumsum(chunks(log_a, dk), axis=3)
    incl = jnp.tril(jnp.ones((L, L), bool))[:, :, None]
    decay = jnp.exp(jnp.where(incl, G[..., :, None, :] - G[..., None, :, :], -jnp.inf))
    attn = jnp.einsum("bhcid,bhcjd,bhcijd->bhcij", q, k, decay)
    o_intra = jnp.einsum("bhcij,bhcjv->bhciv", attn, v)
    qg = q * jnp.exp(G)
    kd = k * jnp.exp(G[..., -1:, :] - G)
    dec = jnp.exp(G[..., -1, :])

    def step(s, inp):
        qg_c, kd_c, v_c, d_c = inp
        o = qg_c @ s
        s = s * d_c[..., :, None] + jnp.swapaxes(kd_c, -1, -2) @ v_c
        return s, o

    s0 = jnp.zeros((bsz, H, dk, dv), f32)
    _, o_inter = lax.scan(step, s0, tuple(jnp.moveaxis(t, 2, 0) for t in (qg, kd, v, dec)))
    o = o_intra + jnp.moveaxis(o_inter, 0, 2)
    o = jnp.moveaxis(o, 1, 3).reshape(bsz, T, H, dv)
    o = _rmsnorm(o, norm_w) * jax.nn.silu(r.astype(f32).reshape(bsz, T, H, dv))
    return o.reshape(bsz, T, BRANCH_WIDTH)


def _rglru_mixer(xb, gate_b, conv_w, conv_b, w_a, b_a, w_x, b_x, lam):
    f32 = jnp.float32
    bsz, T, _ = xb.shape
    xc = _causal_dwconv(xb, conv_w, conv_b).astype(f32)
    xblk = xc.reshape(bsz, T, LRU_BLOCKS, LRU_BLOCK_DIM)
    gate_r = jax.nn.sigmoid(jnp.einsum("btnd,nde->btne", xblk, w_a.astype(f32)).reshape(bsz, T, BRANCH_WIDTH) + b_a.astype(f32))
    gate_i = jax.nn.sigmoid(jnp.einsum("btnd,nde->btne", xblk, w_x.astype(f32)).reshape(bsz, T, BRANCH_WIDTH) + b_x.astype(f32))
    log_a = -LRU_C * gate_r * jax.nn.softplus(-lam.astype(f32))
    a = jnp.exp(log_a)
    u = jnp.sqrt(-jnp.expm1(2.0 * log_a)) * (gate_i * xc)

    def combine(left, right):
        a_l, h_l = left
        a_r, h_r = right
        return a_l * a_r, a_r * h_l + h_r

    _, h = lax.associative_scan(combine, (a, u), axis=1)
    return h * jax.nn.gelu(gate_b.astype(f32))


def _peer_ffn(h, w_q, sub_keys, u_tab, v_tab):
    f32 = jnp.float32
    bsz, T, D = h.shape
    n_tok = bsz * T
    tok = h.reshape(n_tok, D)
    q = (tok @ w_q).reshape(n_tok, PEER_HEADS, 2, PEER_HALF_DIM)
    s = jnp.einsum("thpd,hpkd->thpk", q, sub_keys).astype(f32)
    s1, i1 = lax.top_k(s[:, :, 0], PEER_TOPK)
    s2, i2 = lax.top_k(s[:, :, 1], PEER_TOPK)
    n_cand = PEER_TOPK * PEER_TOPK
    cand_s = (s1[..., :, None] + s2[..., None, :]).reshape(n_tok, PEER_HEADS, n_cand)
    cand_i = (i1[..., :, None] * PEER_N_KEYS + i2[..., None, :]).reshape(n_tok, PEER_HEADS, n_cand)
    top_s, pos = lax.top_k(cand_s, PEER_TOPK)
    idx = jnp.take_along_axis(cand_i, pos, axis=-1)
    gate = jax.nn.softmax(top_s, axis=-1)
    nblk = n_tok // PEER_TOKEN_BLOCK
    k_all = PEER_HEADS * PEER_TOPK

    def expert_block(args):
        x_b, i_b, g_b = args
        act = jax.nn.gelu(jnp.einsum("td,tkd->tk", x_b, u_tab[i_b]).astype(f32)) * g_b
        return jnp.einsum("tk,tkd->td", act.astype(x_b.dtype), v_tab[i_b])

    out = lax.map(expert_block, (tok.reshape(nblk, PEER_TOKEN_BLOCK, D),
                                 idx.reshape(nblk, PEER_TOKEN_BLOCK, k_all),
                                 gate.reshape(nblk, PEER_TOKEN_BLOCK, k_all)))
    return out.reshape(bsz, T, D)


def setup_inputs(seed: int = 0) -> dict:
    key = jax.random.key(seed)
    keys = iter(jax.random.split(key, 48))
    f32 = jnp.float32
    L, D = DEPTH, D_MODEL

    def nrm(shape, scale):
        return jax.random.normal(next(keys), shape, f32) * scale

    def gain(shape):
        return 1.0 + nrm(shape, 0.02)

    def unif(shape, lo, hi):
        return jax.random.uniform(next(keys), shape, f32, lo, hi)

    def dt_bias(shape):
        dt = jnp.exp(unif(shape, math.log(1e-3), math.log(1e-1)))
        return dt + jnp.log(-jnp.expm1(-dt))

    a_c = unif((L, BRANCH_WIDTH), 0.9, 0.999)
    s = a_c ** (1.0 / LRU_C)
    return {
        "x": nrm((BATCH, SEQ, D), 1.0),
        "c": nrm((BATCH, D), 1.0),
        "w_ada": nrm((L, D, 6 * D), D ** -0.5),
        "b_ada": nrm((L, 6 * D), 0.02),
        "norm_mix_w": gain((L, D)),
        "norm_ffn_w": gain((L, D)),
        "w_in": nrm((L, D, D_IN), D ** -0.5),
        "ssm_conv_w": nrm((L, CONV_WIDTH, SSM_XBC), CONV_WIDTH ** -0.5),
        "ssm_conv_b": nrm((L, SSM_XBC), 0.02),
        "ssm_dt_bias": dt_bias((L, SSM_HEADS)),
        "ssm_a_log": jnp.log(unif((L, SSM_HEADS), 1.0, 16.0)),
        "ssm_d": gain((L, SSM_HEADS)),
        "ssm_norm_w": gain((L, BRANCH_WIDTH)),
        "gdn_conv_w": nrm((L, CONV_WIDTH, 3 * BRANCH_WIDTH), CONV_WIDTH ** -0.5),
        "gdn_a_log": jnp.log(unif((L, GDN_HEADS), 1.0, 16.0)),
        "gdn_dt_bias": dt_bias((L, GDN_HEADS)),
        "gdn_norm_w": gain((L, GDN_HEAD_DIM)),
        "gla_w_gate": nrm((L, GLA_GATE_RANK, GLA_HEADS * GLA_KEY_DIM), GLA_GATE_RANK ** -0.5),
        "gla_b_gate": nrm((L, GLA_HEADS * GLA_KEY_DIM), 0.02),
        "gla_norm_w": gain((L, GLA_VAL_DIM)),
        "lru_conv_w": nrm((L, CONV_WIDTH, BRANCH_WIDTH), CONV_WIDTH ** -0.5),
        "lru_conv_b": nrm((L, BRANCH_WIDTH), 0.02),
        "lru_w_a": nrm((L, LRU_BLOCKS, LRU_BLOCK_DIM, LRU_BLOCK_DIM), LRU_BLOCK_DIM ** -0.5),
        "lru_b_a": nrm((L, BRANCH_WIDTH), 0.02),
        "lru_w_x": nrm((L, LRU_BLOCKS, LRU_BLOCK_DIM, LRU_BLOCK_DIM), LRU_BLOCK_DIM ** -0.5),
        "lru_b_x": nrm((L, BRANCH_WIDTH), 0.02),
        "lru_lambda": jnp.log(s) - jnp.log1p(-s),
        "w_branch": nrm((L, N_BRANCHES, BRANCH_WIDTH, D), BRANCH_WIDTH ** -0.5),
        "w_out": nrm((L, D, D), D ** -0.5),
        "peer_w_q": nrm((L, D, PEER_HEADS * PEER_QUERY_DIM), D ** -0.5),
        "peer_sub_keys": nrm((L, PEER_HEADS, 2, PEER_N_KEYS, PEER_HALF_DIM), PEER_HALF_DIM ** -0.5),
        "peer_u": nrm((L, PEER_N_EXPERTS, D), D ** -0.5),
        "peer_v": nrm((L, PEER_N_EXPERTS, D), PEER_HEADS ** -0.5),
        "final_norm_w": gain((D,)),
    }


def reference(x, c, w_ada, b_ada, norm_mix_w, norm_ffn_w, w_in, ssm_conv_w, ssm_conv_b,
              ssm_dt_bias, ssm_a_log, ssm_d, ssm_norm_w, gdn_conv_w, gdn_a_log, gdn_dt_bias,
              gdn_norm_w, gla_w_gate, gla_b_gate, gla_norm_w, lru_conv_w, lru_conv_b, lru_w_a,
              lru_b_a, lru_w_x, lru_b_x, lru_lambda, w_branch, w_out, peer_w_q, peer_sub_keys,
              peer_u, peer_v, final_norm_w):
    bsz, T, _ = x.shape
    c_act = jax.nn.silu(c)
    for l in range(DEPTH):
        mod = c_act @ w_ada[l] + b_ada[l]
        sh1, sc1, g1, sh2, sc2, g2 = jnp.split(mod[:, None, :], 6, axis=-1)
        h = _rmsnorm(x, norm_mix_w[l]) * (1.0 + sc1) + sh1
        proj = h @ w_in[l]
        (ssm_z, ssm_xbc, ssm_dt, gdn_qkv, gdn_z, gdn_beta, gdn_decay, gla_q, gla_k, gla_v,
         gla_r, gla_low, lru_x, lru_gate, merge_logits) = _split(proj, IN_SIZES)
        y_ssm = _ssd_mixer(ssm_z, ssm_xbc, ssm_dt, ssm_conv_w[l], ssm_conv_b[l], ssm_dt_bias[l],
                           ssm_a_log[l], ssm_d[l], ssm_norm_w[l])
        y_gdn = _gdn_mixer(gdn_qkv, gdn_z, gdn_beta, gdn_decay, gdn_conv_w[l], gdn_a_log[l],
                           gdn_dt_bias[l], gdn_norm_w[l])
        y_gla = _gla_mixer(gla_q, gla_k, gla_v, gla_r, gla_low, gla_w_gate[l], gla_b_gate[l],
                           gla_norm_w[l])
        y_lru = _rglru_mixer(lru_x, lru_gate, lru_conv_w[l], lru_conv_b[l], lru_w_a[l], lru_b_a[l],
                             lru_w_x[l], lru_b_x[l], lru_lambda[l])
        branches = (y_ssm, y_gdn, y_gla, y_lru)
        gates = jax.nn.sigmoid(merge_logits.astype(jnp.float32)).reshape(bsz, T, N_BRANCHES, D_MODEL)
        merged = sum(gates[:, :, i] * (branches[i].astype(x.dtype) @ w_branch[l, i])
                     for i in range(N_BRANCHES))
        x = x + g1 * (merged.astype(x.dtype) @ w_out[l])
        h2 = _rmsnorm(x, norm_ffn_w[l]) * (1.0 + sc2) + sh2
        x = x + g2 * _peer_ffn(h2, peer_w_q[l], peer_sub_keys[l], peer_u[l], peer_v[l]).astype(x.dtype)
    return _rmsnorm(x, final_norm_w)
```

```python
import functools
import math

import jax
import jax.numpy as jnp
from jax import lax
from jax.experimental import pallas as pl
from jax.experimental.pallas import tpu as pltpu

F32 = jnp.float32
BF16 = jnp.bfloat16
HIGHEST = lax.Precision.HIGHEST
NEG_INF = float("-inf")

D_MODEL = 1024
N_LAYERS = 2
EPS = 1e-6
CONV_W = 4
BRANCH = 512
N_BRANCH = 4
SSM_HEADS = 8
SSM_HEAD_DIM = 64
SSM_GROUPS = 2
SSM_STATE = 64
GDN_HEADS = 4
GDN_DIM = 128
GLA_HEADS = 4
GLA_DK = 64
GLA_DV = 128
GLA_RANK = 16
GLA_TAU = 16.0
GLA_CHUNK = 16
LRU_BLOCKS = 8
LRU_BLOCK_DIM = 64
LRU_C = 8.0
PEER_HEADS = 8
PEER_KEYS = 128
PEER_EXPERTS = PEER_KEYS * PEER_KEYS
PEER_HALF = 128
PEER_TOPK = 16

LANES = 128
SUBLANES = 8
VMEM_LIMIT = 48 * 1024 * 1024

_SRC = {}
_off = 0
for _name, _w in (("ssm_z", 512), ("ssm_x", 512), ("ssm_b", 128), ("ssm_c", 128), ("ssm_dt", 8),
                  ("gdn_q", 512), ("gdn_k", 512), ("gdn_v", 512), ("gdn_z", 512), ("gdn_beta", 4),
                  ("gdn_decay", 4), ("gla_q", 256), ("gla_k", 256), ("gla_v", 512), ("gla_r", 512),
                  ("gla_low", 16), ("lru_x", 512), ("lru_gate", 512), ("merge", 4096)):
    _SRC[_name] = (_off, _w)
    _off += _w
D_IN = _off
_DST_ORDER = ("merge", "ssm_z", "ssm_x", "gdn_q", "gdn_k", "gdn_v", "gdn_z", "gla_q", "gla_k",
              "gla_v", "gla_r", "lru_x", "lru_gate", "ssm_b", "ssm_c", "ssm_dt", "gdn_beta",
              "gdn_decay", "gla_low")
SMALL_W = 256
N_COLS = 4096 + 11 * 512 + 256 + SMALL_W
U512 = {"ssm_z": 8, "ssm_x": 9, "gdn_q": 10, "gdn_k": 11, "gdn_v": 12, "gdn_z": 13, "gla_qk": 14,
        "gla_v": 15, "gla_r": 16, "lru_x": 17, "lru_gate": 18}
U256_BC = 38
U256_SMALL = 39
SM_DT = 0
SM_BETA = 8
SM_DECAY = 12
SM_LOW = 16


def _permute_w_in(w):
    parts = [w[:, _SRC[name][0]:_SRC[name][0] + _SRC[name][1]] for name in _DST_ORDER]
    used = sum(_SRC[name][1] for name in _DST_ORDER)
    parts.append(jnp.zeros((w.shape[0], N_COLS - used), w.dtype))
    return jnp.concatenate(parts, axis=1).astype(BF16)


def _mm(a, b):
    return jnp.dot(a.astype(BF16), b.astype(BF16), preferred_element_type=F32)


def _mm_nt(a, b):
    return lax.dot_general(a.astype(BF16), b.astype(BF16), (((1,), (1,)), ((), ())),
                           preferred_element_type=F32)


def _mm_tn(a, b):
    return lax.dot_general(a.astype(BF16), b.astype(BF16), (((0,), (0,)), ((), ())),
                           preferred_element_type=F32)


def _mm_hi(a, b):
    return jnp.dot(a, b, precision=HIGHEST, preferred_element_type=F32)


def _sigmoid(x):
    return 1.0 / (1.0 + jnp.exp(-x))


def _silu(x):
    return x * _sigmoid(x)


def _softplus(x):
    return jnp.maximum(x, 0.0) + jnp.log1p(jnp.exp(-jnp.abs(x)))


def _gelu(x):
    c = math.sqrt(2.0 / math.pi)
    return 0.5 * x * (1.0 + jnp.tanh(c * (x + 0.044715 * (x * x * x))))


def _rms_rows(x, w):
    ms = jnp.mean(x * x, axis=-1, keepdims=True)
    return x * lax.rsqrt(ms + EPS) * w


def _causal_conv(ext_ref, w_ref, n_rows):
    acc = None
    for k in range(CONV_W):
        term = w_ref[k:k + 1, :] * ext_ref[SUBLANES - CONV_W + 1 + k:SUBLANES - CONV_W + 1 + k + n_rows, :]
        acc = term if acc is None else acc + term
    return acc


def _params(sem):
    return pltpu.CompilerParams(dimension_semantics=sem, vmem_limit_bytes=VMEM_LIMIT)


ADA_TN = 1536


def _ada_kernel(c_ref, w_ref, b_ref, o_ref):
    c = c_ref[...]
    o_ref[0] = _mm(_silu(c), w_ref[0]) + b_ref[0]


def _ada_call(c_pad, w_ada, b_ada):
    n_l = w_ada.shape[0]
    n_out = w_ada.shape[2]
    return pl.pallas_call(
        _ada_kernel,
        grid=(n_l, n_out // ADA_TN),
        in_specs=[pl.BlockSpec((SUBLANES, D_MODEL), lambda l, j: (0, 0)),
                  pl.BlockSpec((1, D_MODEL, ADA_TN), lambda l, j: (l, 0, j)),
                  pl.BlockSpec((1, 1, ADA_TN), lambda l, j: (l, 0, j))],
        out_specs=pl.BlockSpec((1, SUBLANES, ADA_TN), lambda l, j: (l, 0, j)),
        out_shape=jax.ShapeDtypeStruct((n_l, SUBLANES, n_out), F32),
        compiler_params=_params(("arbitrary", "arbitrary")),
    )(c_pad, w_ada, b_ada.reshape(n_l, 1, n_out))


INP_TM = 512
INP_TN = 2048


def _inproj_kernel(x_ref, nw_ref, sc_ref, sh_ref, w_ref, o_ref, h_scr):
    @pl.when(pl.program_id(1) == 0)
    def _():
        h = _rms_rows(x_ref[...], nw_ref[...])
        h = h * (1.0 + sc_ref[0]) + sh_ref[0]
        h_scr[...] = h.astype(BF16)

    o_ref[...] = jnp.dot(h_scr[...], w_ref[...], preferred_element_type=F32)


def _inproj_call(x2, nw, sc, sh, w_perm, seq):
    n_tok = x2.shape[0]
    per_b = seq // INP_TM
    return pl.pallas_call(
        _inproj_kernel,
        grid=(n_tok // INP_TM, N_COLS // INP_TN),
        in_specs=[pl.BlockSpec((INP_TM, D_MODEL), lambda i, j: (i, 0)),
                  pl.BlockSpec((1, D_MODEL), lambda i, j: (0, 0)),
                  pl.BlockSpec((1, 1, D_MODEL), lambda i, j: (i // per_b, 0, 0)),
                  pl.BlockSpec((1, 1, D_MODEL), lambda i, j: (i // per_b, 0, 0)),
                  pl.BlockSpec((D_MODEL, INP_TN), lambda i, j: (0, j))],
        out_specs=pl.BlockSpec((INP_TM, INP_TN), lambda i, j: (i, j)),
        out_shape=jax.ShapeDtypeStruct((n_tok, N_COLS), F32),
        scratch_shapes=[pltpu.VMEM((INP_TM, D_MODEL), BF16)],
        compiler_params=_params(("arbitrary", "arbitrary")),
    )(x2, nw, sc, sh, w_perm)


SSD_L = 128


def _ssd_kernel(z_ref, xs_ref, bc_ref, sm_ref, cwx_ref, cbx_ref, cwbc_ref, cbbc_ref, dtb_ref,
                alog_ref, dfull_ref, nw_ref, o_ref, extx, extbc, state):
    L = SSD_L
    t = pl.program_id(1)

    @pl.when(t == 0)
    def _():
        extx[0:SUBLANES, :] = jnp.zeros((SUBLANES, BRANCH), F32)
        extbc[0:SUBLANES, :] = jnp.zeros((SUBLANES, 256), F32)
        state[...] = jnp.zeros_like(state)

    extx[SUBLANES:SUBLANES + L, :] = xs_ref[...]
    extbc[SUBLANES:SUBLANES + L, :] = bc_ref[...]
    xs = _silu(_causal_conv(extx, cwx_ref, L) + cbx_ref[...])
    bc = _silu(_causal_conv(extbc, cwbc_ref, L) + cbbc_ref[...])
    extx[0:SUBLANES, :] = extx[L:L + SUBLANES, :]
    extbc[0:SUBLANES, :] = extbc[L:L + SUBLANES, :]

    dt = _softplus(sm_ref[...] + dtb_ref[...])
    da = dt * (-jnp.exp(alog_ref[...]))
    ri = lax.broadcasted_iota(jnp.int32, (L, L), 0)
    ci = lax.broadcasted_iota(jnp.int32, (L, L), 1)
    tril = ri >= ci
    cs = _mm_hi(tril.astype(F32), da)
    er = lax.broadcasted_iota(jnp.int32, (SMALL_W, BRANCH), 0)
    ec = lax.broadcasted_iota(jnp.int32, (SMALL_W, BRANCH), 1)
    expand = (er == (ec >> 6)).astype(F32)
    cs_full = _mm_hi(cs, expand)
    dt_full = _mm_hi(dt, expand)
    ecs_full = jnp.exp(cs_full)
    cs_last = cs_full[L - 1:L, :]
    w_full = jnp.exp(cs_last - cs_full)
    xdt = xs * dt_full
    xdtw = xdt * w_full
    cs_t = cs.T

    b128 = bc[:, 0:LANES]
    c128 = bc[:, LANES:2 * LANES]
    lane = lax.broadcasted_iota(jnp.int32, (1, LANES), 1)
    cg = [jnp.where(lane < SSM_STATE, c128, 0.0), jnp.where(lane >= SSM_STATE, c128, 0.0)]
    cb = [_mm_nt(cg[g], b128) for g in range(SSM_GROUPS)]

    y_pairs = []
    for p in range(SSM_HEADS // 2):
        xp = xdt[:, p * LANES:(p + 1) * LANES]
        yp = None
        for hh in range(2):
            h = 2 * p + hh
            g = h // (SSM_HEADS // SSM_GROUPS)
            col = cs[:, h:h + 1]
            row = cs_t[h:h + 1, :]
            seg = jnp.exp(jnp.where(tril, col - row, NEG_INF))
            att = cb[g] * seg
            hm = (lane < SSM_HEAD_DIM) if hh == 0 else (lane >= SSM_HEAD_DIM)
            term = _mm(att, jnp.where(hm, xp, 0.0))
            yp = term if yp is None else yp + term
        y_pairs.append(yp)
    y_diag = jnp.concatenate(y_pairs, axis=1)

    y_offs = []
    for g in range(SSM_GROUPS):
        sl = slice(g * 256, (g + 1) * 256)
        s_in = state[g]
        y_offs.append(_mm(cg[g], s_in) * ecs_full[:, sl])
        new = _mm_tn(b128, xdtw[:, sl])
        state[g] = s_in * ecs_full[L - 1:L, sl] + new
    y = y_diag + jnp.concatenate(y_offs, axis=1) + xs * dfull_ref[...]
    y = y * _silu(z_ref[...])
    o_ref[...] = _rms_rows(y, nw_ref[...]).astype(BF16)


def _ssd_call(proj, p, bsz, seq):
    nt = seq // SSD_L
    row = lambda blk: pl.BlockSpec((SSD_L, 512), lambda b, t: (b * nt + t, blk))
    full = lambda shape: pl.BlockSpec(shape, lambda b, t: (0,) * len(shape))
    return pl.pallas_call(
        _ssd_kernel,
        grid=(bsz, nt),
        in_specs=[row(U512["ssm_z"]), row(U512["ssm_x"]),
                  pl.BlockSpec((SSD_L, 256), lambda b, t: (b * nt + t, U256_BC)),
                  pl.BlockSpec((SSD_L, 256), lambda b, t: (b * nt + t, U256_SMALL)),
                  full((CONV_W, 512)), full((1, 512)), full((CONV_W, 256)), full((1, 256)),
                  full((1, SMALL_W)), full((1, SMALL_W)), full((1, 512)), full((1, 512))],
        out_specs=pl.BlockSpec((SSD_L, 512), lambda b, t: (b * nt + t, 0)),
        out_shape=jax.ShapeDtypeStruct((bsz * seq, BRANCH), BF16),
        scratch_shapes=[pltpu.VMEM((SUBLANES + SSD_L, 512), F32),
                        pltpu.VMEM((SUBLANES + SSD_L, 256), F32),
                        pltpu.VMEM((SSM_GROUPS, LANES, 256), F32)],
        compiler_params=_params(("arbitrary", "arbitrary")),
    )(proj, proj, proj, proj, p["cwx"], p["cbx"], p["cwbc"], p["cbbc"], p["dtb"], p["alog"],
      p["dfull"], p["nw"])


GDN_L = 64


def _gdn_kernel(q_ref, k_ref, v_ref, z_ref, sm_ref, cwq_ref, cwk_ref, cwv_ref, dtb_ref, alog_ref,
                nw_ref, o_ref, extq, extk, extv, state):
    L = GDN_L
    t = pl.program_id(1)

    @pl.when(t == 0)
    def _():
        for e in (extq, extk, extv):
            e[0:SUBLANES, :] = jnp.zeros((SUBLANES, BRANCH), F32)
        state[...] = jnp.zeros_like(state)

    outs = []
    for e, r, w in ((extq, q_ref, cwq_ref), (extk, k_ref, cwk_ref), (extv, v_ref, cwv_ref)):
        e[SUBLANES:SUBLANES + L, :] = r[...]
        outs.append(_silu(_causal_conv(e, w, L)))
        e[0:SUBLANES, :] = e[L:L + SUBLANES, :]
    q, k, v = outs

    sm = sm_ref[...]
    beta_all = _sigmoid(sm)
    g_all = -jnp.exp(alog_ref[...]) * _softplus(sm + dtb_ref[...])
    ri = lax.broadcasted_iota(jnp.int32, (L, L), 0)
    ci = lax.broadcasted_iota(jnp.int32, (L, L), 1)
    incl = ri >= ci
    strict = ri > ci
    eye = (ri == ci).astype(F32)
    bx = (ri >> 3) ^ (ci >> 3)
    blk = (bx > 0).astype(jnp.int32) + (bx > 1).astype(jnp.int32) + (bx > 3).astype(jnp.int32)
    gc_all = _mm_hi(incl.astype(F32), g_all)
    gc_t = gc_all.T
    z = z_ref[...]
    nw = nw_ref[...]

    o_heads = []
    for h in range(GDN_HEADS):
        sl = slice(h * GDN_DIM, (h + 1) * GDN_DIM)
        qh, kh, vh = q[:, sl], k[:, sl], v[:, sl]
        qh = qh * lax.rsqrt(jnp.sum(qh * qh, axis=-1, keepdims=True) + EPS) * (GDN_DIM ** -0.5)
        kh = kh * lax.rsqrt(jnp.sum(kh * kh, axis=-1, keepdims=True) + EPS)
        beta = beta_all[:, SM_BETA + h:SM_BETA + h + 1]
        gcol = gc_all[:, SM_DECAY + h:SM_DECAY + h + 1]
        grow = gc_t[SM_DECAY + h:SM_DECAY + h + 1, :]
        decay = jnp.exp(jnp.where(incl, gcol - grow, NEG_INF))
        gcb = jnp.broadcast_to(gcol, (L, GDN_DIM))
        egc = jnp.exp(gcb)
        glast = gcb[L - 1:L, :]
        kb = kh * beta
        m = jnp.where(strict, _mm_nt(kb, kh) * decay, 0.0)
        rhs = jnp.concatenate([vh * beta, kb * egc], axis=1)
        md = jnp.where(blk == 0, m, 0.0)
        p2 = _mm(md, md)
        base = eye - md
        base = base + _mm(base, p2)
        inv = base + _mm(base, _mm(p2, p2))
        for lvl in range(1, int(math.log2(L // SUBLANES)) + 1):
            off = jnp.where(blk == lvl, m, 0.0)
            inv = inv - _mm(inv, _mm(off, inv))
        sol = _mm(inv, rhs)
        u, w = sol[:, :GDN_DIM], sol[:, GDN_DIM:]
        attn = _mm_nt(qh, kh) * decay
        qg = qh * egc
        kd = kh * jnp.exp(glast - gcb)
        s = state[h]
        v_new = u - _mm(w, s)
        o = _mm(qg, s) + _mm(attn, v_new)
        state[h] = s * jnp.exp(glast) + _mm_tn(kd, v_new)
        o = _rms_rows(o, nw[:, sl]) * _silu(z[:, sl])
        o_heads.append(o)
    o_ref[...] = jnp.concatenate(o_heads, axis=1).astype(BF16)


def _gdn_call(proj, p, bsz, seq):
    nt = seq // GDN_L
    row = lambda blk: pl.BlockSpec((GDN_L, 512), lambda b, t: (b * nt + t, blk))
    full = lambda shape: pl.BlockSpec(shape, lambda b, t: (0,) * len(shape))
    return pl.pallas_call(
        _gdn_kernel,
        grid=(bsz, nt),
        in_specs=[row(U512["gdn_q"]), row(U512["gdn_k"]), row(U512["gdn_v"]), row(U512["gdn_z"]),
                  pl.BlockSpec((GDN_L, 256), lambda b, t: (b * nt + t, U256_SMALL)),
                  full((CONV_W, 512)), full((CONV_W, 512)), full((CONV_W, 512)),
                  full((1, SMALL_W)), full((1, SMALL_W)), full((1, 512))],
        out_specs=pl.BlockSpec((GDN_L, 512), lambda b, t: (b * nt + t, 0)),
        out_shape=jax.ShapeDtypeStruct((bsz * seq, BRANCH), BF16),
        scratch_shapes=[pltpu.VMEM((SUBLANES + GDN_L, 512), F32)] * 3
        + [pltpu.VMEM((GDN_HEADS, GDN_DIM, GDN_DIM), F32)],
        compiler_params=_params(("arbitrary", "arbitrary")),
    )(proj, proj, proj, proj, proj, p["cwq"], p["cwk"], p["cwv"], p["dtb"], p["alog"], p["nw"])


GLA_T = 128
GLA_QK = GLA_HEADS * GLA_DK


def _gla_kernel(qk_ref, v_ref, r_ref, sm_ref, wg_ref, bg_ref, nw_ref, o_ref, kbuf, gbuf, vbuf, state):
    T = GLA_T
    C = GLA_CHUNK
    t = pl.program_id(1)

    @pl.when(t == 0)
    def _():
        kbuf[0:C, :] = jnp.zeros((C, GLA_QK), F32)
        gbuf[0:C, :] = jnp.zeros((C, GLA_QK), F32)
        vbuf[0:C, :] = jnp.zeros((C, BRANCH), F32)
        state[...] = jnp.zeros_like(state)

    qk = qk_ref[...]
    q = qk[:, :GLA_QK] * (GLA_DK ** -0.5)
    k = qk[:, GLA_QK:]
    v = v_ref[...]
    pre = _mm(sm_ref[...], wg_ref[...]) + bg_ref[...]
    log_a = (jnp.minimum(pre, 0.0) - jnp.log1p(jnp.exp(-jnp.abs(pre)))) / GLA_TAU
    ri = lax.broadcasted_iota(jnp.int32, (T, T), 0)
    ci = lax.broadcasted_iota(jnp.int32, (T, T), 1)
    blocktri = ((ri >> 4) == (ci >> 4)) & (ri >= ci)
    G = _mm_hi(blocktri.astype(F32), log_a)
    kbuf[C:C + T, :] = k
    gbuf[C:C + T, :] = G
    vbuf[C:C + T, :] = v

    rr = lax.broadcasted_iota(jnp.int32, (GLA_QK, BRANCH), 0)
    rc = lax.broadcasted_iota(jnp.int32, (GLA_QK, BRANCH), 1)
    red = ((rr >> 6) == (rc >> 7)).astype(BF16)
    rmod = lax.broadcasted_iota(jnp.int32, (T, 1), 0) & (C - 1)
    o = jnp.zeros((T, BRANCH), F32)
    for d in range(C):
        ks = kbuf[C - d:C - d + T, :]
        gs = gbuf[C - d:C - d + T, :]
        vs = vbuf[C - d:C - d + T, :]
        prod = jnp.where(rmod >= d, q * ks * jnp.exp(G - gs), 0.0)
        hi = prod.astype(BF16)
        lo = (prod - hi.astype(F32)).astype(BF16)
        a_full = (jnp.dot(hi, red, preferred_element_type=F32)
                  + jnp.dot(lo, red, preferred_element_type=F32))
        o = o + a_full * vs

    sr = lax.broadcasted_iota(jnp.int32, (BRANCH, GLA_QK), 0)
    sc = lax.broadcasted_iota(jnp.int32, (BRANCH, GLA_QK), 1)
    blockdiag = (sr >> 7) == (sc >> 6)
    st = state[...]
    inter = []
    for c in range(T // C):
        rows = slice(c * C, (c + 1) * C)
        gc = G[rows]
        glast = gc[C - 1:C, :]
        qg = q[rows] * jnp.exp(gc)
        kd = k[rows] * jnp.exp(glast - gc)
        inter.append(_mm_nt(qg, st))
        st = jnp.where(blockdiag, st * jnp.exp(glast) + _mm_tn(v[rows], kd), 0.0)
    state[...] = st
    o = o + jnp.concatenate(inter, axis=0)
    r = r_ref[...]
    nw = nw_ref[...]
    outs = []
    for h in range(GLA_HEADS):
        sl = slice(h * GLA_DV, (h + 1) * GLA_DV)
        outs.append(_rms_rows(o[:, sl], nw[:, sl]) * _silu(r[:, sl]))
    o_ref[...] = jnp.concatenate(outs, axis=1).astype(BF16)


def _gla_call(proj, p, bsz, seq):
    nt = seq // GLA_T
    row = lambda blk: pl.BlockSpec((GLA_T, 512), lambda b, t: (b * nt + t, blk))
    full = lambda shape: pl.BlockSpec(shape, lambda b, t: (0,) * len(shape))
    return pl.pallas_call(
        _gla_kernel,
        grid=(bsz, nt),
        in_specs=[row(U512["gla_qk"]), row(U512["gla_v"]), row(U512["gla_r"]),
                  pl.BlockSpec((GLA_T, 256), lambda b, t: (b * nt + t, U256_SMALL)),
                  full((SMALL_W, GLA_QK)), full((1, GLA_QK)), full((1, 512))],
        out_specs=pl.BlockSpec((GLA_T, 512), lambda b, t: (b * nt + t, 0)),
        out_shape=jax.ShapeDtypeStruct((bsz * seq, BRANCH), BF16),
        scratch_shapes=[pltpu.VMEM((GLA_CHUNK + GLA_T, GLA_QK), F32),
                        pltpu.VMEM((GLA_CHUNK + GLA_T, GLA_QK), F32),
                        pltpu.VMEM((GLA_CHUNK + GLA_T, BRANCH), F32),
                        pltpu.VMEM((BRANCH, GLA_QK), F32)],
        compiler_params=_params(("arbitrary", "arbitrary")),
    )(proj, proj, proj, proj, p["wg"], p["bg"], p["nw"])


LRU_T = 256
LRU_PAD = LRU_T // 2


def _lru_kernel(x_ref, gate_ref, cw_ref, cb_ref, wa_ref, ba_ref, wx_ref, bx_ref, lam_ref, o_ref,
                ext, abuf, hbuf, carry):
    T = LRU_T
    P = LRU_PAD
    t = pl.program_id(1)

    @pl.when(t == 0)
    def _():
        ext[0:SUBLANES, :] = jnp.zeros((SUBLANES, BRANCH), F32)
        abuf[0:P, :] = jnp.ones((P, BRANCH), F32)
        hbuf[0:P, :] = jnp.zeros((P, BRANCH), F32)
        carry[...] = jnp.zeros_like(carry)

    ext[SUBLANES:SUBLANES + T, :] = x_ref[...]
    xc = _causal_conv(ext, cw_ref, T) + cb_ref[...]
    ext[0:SUBLANES, :] = ext[T:T + SUBLANES, :]
    gate_r = _sigmoid(_mm(xc, wa_ref[...]) + ba_ref[...])
    gate_i = _sigmoid(_mm(xc, wx_ref[...]) + bx_ref[...])
    log_a = -LRU_C * gate_r * _softplus(-lam_ref[...])
    abuf[P:P + T, :] = jnp.exp(log_a)
    th = jnp.tanh(log_a)
    hbuf[P:P + T, :] = jnp.sqrt(-2.0 * th / (1.0 - th)) * (gate_i * xc)
    s = 1
    while s < T:
        a_cur = abuf[P:P + T, :]
        h_cur = hbuf[P:P + T, :]
        a_sh = abuf[P - s:P - s + T, :]
        h_sh = hbuf[P - s:P - s + T, :]
        hbuf[P:P + T, :] = h_cur + a_cur * h_sh
        abuf[P:P + T, :] = a_cur * a_sh
        s *= 2
    h = hbuf[P:P + T, :] + abuf[P:P + T, :] * carry[0:1, :]
    carry[0:1, :] = h[T - 1:T, :]
    o_ref[...] = (h * _gelu(gate_ref[...])).astype(BF16)


def _lru_call(proj, p, bsz, seq):
    nt = seq // LRU_T
    row = lambda blk: pl.BlockSpec((LRU_T, 512), lambda b, t: (b * nt + t, blk))
    full = lambda shape: pl.BlockSpec(shape, lambda b, t: (0,) * len(shape))
    return pl.pallas_call(
        _lru_kernel,
        grid=(bsz, nt),
        in_specs=[row(U512["lru_x"]), row(U512["lru_gate"]),
                  full((CONV_W, 512)), full((1, 512)), full((512, 512)), full((1, 512)),
                  full((512, 512)), full((1, 512)), full((1, 512))],
        out_specs=pl.BlockSpec((LRU_T, 512), lambda b, t: (b * nt + t, 0)),
        out_shape=jax.ShapeDtypeStruct((bsz * seq, BRANCH), BF16),
        scratch_shapes=[pltpu.VMEM((SUBLANES + LRU_T, 512), F32),
                        pltpu.VMEM((LRU_PAD + LRU_T, 512), F32),
                        pltpu.VMEM((LRU_PAD + LRU_T, 512), F32),
                        pltpu.VMEM((SUBLANES, 512), F32)],
        compiler_params=_params(("arbitrary", "arbitrary")),
    )(proj, proj, p["cw"], p["cb"], p["wa"], p["ba"], p["wx"], p["bx"], p["lam"])


MRG_TM = 256


def _merge_kernel(x_ref, g_ref, lg_ref, y0_ref, y1_ref, y2_ref, y3_ref, wb_ref, wo_ref, o_ref):
    merged = None
    for i, y_ref in enumerate((y0_ref, y1_ref, y2_ref, y3_ref)):
        br = jnp.dot(y_ref[...], wb_ref[i], preferred_element_type=F32)
        term = _sigmoid(lg_ref[:, i * D_MODEL:(i + 1) * D_MODEL]) * br
        merged = term if merged is None else merged + term
    out = jnp.dot(merged.astype(BF16), wo_ref[...], preferred_element_type=F32)
    o_ref[...] = x_ref[...] + g_ref[0] * out


def _merge_call(x2, g1, proj, ys, wb, wo, seq):
    n_tok = x2.shape[0]
    per_b = seq // MRG_TM
    yspec = pl.BlockSpec((MRG_TM, BRANCH), lambda i: (i, 0))
    return pl.pallas_call(
        _merge_kernel,
        grid=(n_tok // MRG_TM,),
        in_specs=[pl.BlockSpec((MRG_TM, D_MODEL), lambda i: (i, 0)),
                  pl.BlockSpec((1, 1, D_MODEL), lambda i: (i // per_b, 0, 0)),
                  pl.BlockSpec((MRG_TM, N_BRANCH * D_MODEL), lambda i: (i, 0)),
                  yspec, yspec, yspec, yspec,
                  pl.BlockSpec((N_BRANCH, BRANCH, D_MODEL), lambda i: (0, 0, 0)),
                  pl.BlockSpec((D_MODEL, D_MODEL), lambda i: (0, 0))],
        out_specs=pl.BlockSpec((MRG_TM, D_MODEL), lambda i: (i, 0)),
        out_shape=jax.ShapeDtypeStruct((n_tok, D_MODEL), F32),
        compiler_params=_params(("arbitrary",)),
    )(x2, g1, proj, *ys, wb, wo)


PS_TB = 512
N_SCORE_ROWS = 2 * PEER_HEADS * PEER_KEYS


def _peer_score_kernel(x_ref, nw_ref, sc_ref, sh_ref, wqt_ref, keys_ref, h2_ref, st_ref):
    h = _rms_rows(x_ref[...], nw_ref[...])
    h = (h * (1.0 + sc_ref[0]) + sh_ref[0]).astype(BF16)
    h2_ref[...] = h
    qt = lax.dot_general(wqt_ref[...], h, (((1,), (1,)), ((), ())),
                         preferred_element_type=F32).astype(BF16)
    for g in range(2 * PEER_HEADS):
        rows = slice(g * PEER_KEYS, (g + 1) * PEER_KEYS)
        st_ref[rows, :] = jnp.dot(keys_ref[g], qt[rows, :], preferred_element_type=F32)


def _peer_score_call(x2, nw, sc, sh, wqt, keys, seq):
    n_tok = x2.shape[0]
    per_b = seq // PS_TB
    return pl.pallas_call(
        _peer_score_kernel,
        grid=(n_tok // PS_TB,),
        in_specs=[pl.BlockSpec((PS_TB, D_MODEL), lambda i: (i, 0)),
                  pl.BlockSpec((1, D_MODEL), lambda i: (0, 0)),
                  pl.BlockSpec((1, 1, D_MODEL), lambda i: (i // per_b, 0, 0)),
                  pl.BlockSpec((1, 1, D_MODEL), lambda i: (i // per_b, 0, 0)),
                  pl.BlockSpec((N_SCORE_ROWS, D_MODEL), lambda i: (0, 0)),
                  pl.BlockSpec((2 * PEER_HEADS, PEER_KEYS, PEER_HALF), lambda i: (0, 0, 0))],
        out_specs=[pl.BlockSpec((PS_TB, D_MODEL), lambda i: (i, 0)),
                   pl.BlockSpec((N_SCORE_ROWS, PS_TB), lambda i: (0, i))],
        out_shape=[jax.ShapeDtypeStruct((n_tok, D_MODEL), BF16),
                   jax.ShapeDtypeStruct((N_SCORE_ROWS, n_tok), F32)],
        compiler_params=_params(("arbitrary",)),
    )(x2, nw, sc, sh, wqt, keys)


PT_TL = 256
AUX_ROWS = 4 * PEER_HEADS
_CAND = [(i, j) for i in range(PEER_TOPK) for j in range(PEER_TOPK) if (i + 1) * (j + 1) <= PEER_TOPK]
N_CAND_ROWS = -(-len(_CAND) // SUBLANES) * SUBLANES


def _pop_max(x, iota):
    m = jnp.max(x, axis=0, keepdims=True)
    first = jnp.min(jnp.where(x == m, iota, float(x.shape[0])), axis=0, keepdims=True)
    return m, jnp.where(iota == first, NEG_INF, x)


def _peer_thresh_kernel(st_ref, aux_ref, cand):
    TL = PT_TL
    iota_k = lax.broadcasted_iota(jnp.int32, (PEER_KEYS, TL), 0).astype(F32)
    iota_c = lax.broadcasted_iota(jnp.int32, (N_CAND_ROWS, TL), 0).astype(F32)
    cand[...] = jnp.full((N_CAND_ROWS, TL), NEG_INF, F32)
    for h in range(PEER_HEADS):
        tops = []
        for half in range(2):
            base = (half * PEER_HEADS + h) * PEER_KEYS
            x = st_ref[base:base + PEER_KEYS, :]
            vals = []
            for _ in range(PEER_TOPK):
                m, x = _pop_max(x, iota_k)
                vals.append(m)
            tops.append(vals)
        for n, (i, j) in enumerate(_CAND):
            cand[n:n + 1, :] = tops[0][i] + tops[1][j]
        c = cand[...]
        x = c
        tau = None
        for _ in range(PEER_TOPK):
            tau, x = _pop_max(x, iota_c)
        m1, m2 = tops[0][0], tops[1][0]
        zsum = jnp.sum(jnp.where(c >= tau, jnp.exp(c - (m1 + m2)), 0.0), axis=0, keepdims=True)
        for stat, val in enumerate((tau, m1, m2, 1.0 / zsum)):
            aux_ref[stat * PEER_HEADS + h:stat * PEER_HEADS + h + 1, :] = val


def _peer_thresh_call(scores_t):
    n_tok = scores_t.shape[1]
    return pl.pallas_call(
        _peer_thresh_kernel,
        grid=(n_tok // PT_TL,),
        in_specs=[pl.BlockSpec((N_SCORE_ROWS, PT_TL), lambda i: (0, i))],
        out_specs=pl.BlockSpec((AUX_ROWS, PT_TL), lambda i: (0, i)),
        out_shape=jax.ShapeDtypeStruct((AUX_ROWS, n_tok), F32),
        scratch_shapes=[pltpu.VMEM((N_CAND_ROWS, PT_TL), F32)],
        compiler_params=_params(("arbitrary",)),
    )(scores_t)


PE_TB = 512
PE_EB = 512
HALF_ROWS = PEER_HEADS * PEER_KEYS


def _peer_expert_kernel(h2_ref, u_ref, vt_ref, s1_ref, s2_ref, aux_ref, x_ref, g_ref, fw_ref,
                        o_ref, acc, e2, zt, *, final_norm):
    j = pl.program_id(1)
    H = PEER_HEADS

    @pl.when(j == 0)
    def _():
        acc[...] = jnp.zeros_like(acc)
        for h in range(H):
            rows = slice(h * PEER_KEYS, (h + 1) * PEER_KEYS)
            e2[rows, :] = jnp.exp(s2_ref[rows, :] - aux_ref[2 * H + h:2 * H + h + 1, :])

    st = lax.dot_general(u_ref[...], h2_ref[...], (((1,), (1,)), ((), ())),
                         preferred_element_type=F32)
    tau_all = aux_ref[0:H, :]
    for q in range(PE_EB // PEER_KEYS):
        s1q = s1_ref[q]
        e1q = jnp.exp(s1q - aux_ref[H:2 * H, :]) * aux_ref[3 * H:4 * H, :]
        for lc in range(PE_TB // LANES):
            ls = slice(lc * LANES, (lc + 1) * LANES)
            w = None
            for h in range(H):
                rows = slice(h * PEER_KEYS, (h + 1) * PEER_KEYS)
                sel = (s2_ref[rows, ls] + s1q[h:h + 1, ls]) >= tau_all[h:h + 1, ls]
                term = jnp.where(sel, e2[rows, ls], 0.0) * e1q[h:h + 1, ls]
                w = term if w is None else w + term
            g = _gelu(st[q * PEER_KEYS:(q + 1) * PEER_KEYS, ls])
            zt[q * PEER_KEYS:(q + 1) * PEER_KEYS, ls] = (g * w).astype(BF16)
    acc[...] += jnp.dot(vt_ref[...], zt[...], preferred_element_type=F32)

    @pl.when(j == pl.num_programs(1) - 1)
    def _():
        xn = x_ref[...] + g_ref[0] * acc[...].T
        if final_norm:
            xn = _rms_rows(xn, fw_ref[...])
        o_ref[...] = xn


def _peer_expert_call(h2, u_bf, vt_bf, scores_t, aux, x2, g2, fw, seq, final_norm):
    n_tok = x2.shape[0]
    per_b = seq // PE_TB
    kern = functools.partial(_peer_expert_kernel, final_norm=final_norm)
    s1_kh = scores_t[:HALF_ROWS].reshape(PEER_HEADS, PEER_KEYS, n_tok).transpose(1, 0, 2)
    return pl.pallas_call(
        kern,
        grid=(n_tok // PE_TB, PEER_EXPERTS // PE_EB),
        in_specs=[pl.BlockSpec((PE_TB, D_MODEL), lambda i, j: (i, 0)),
                  pl.BlockSpec((PE_EB, D_MODEL), lambda i, j: (j, 0)),
                  pl.BlockSpec((D_MODEL, PE_EB), lambda i, j: (0, j)),
                  pl.BlockSpec((PE_EB // PEER_KEYS, PEER_HEADS, PE_TB), lambda i, j: (j, 0, i)),
                  pl.BlockSpec((HALF_ROWS, PE_TB), lambda i, j: (1, i)),
                  pl.BlockSpec((AUX_ROWS, PE_TB), lambda i, j: (0, i)),
                  pl.BlockSpec((PE_TB, D_MODEL), lambda i, j: (i, 0)),
                  pl.BlockSpec((1, 1, D_MODEL), lambda i, j: (i // per_b, 0, 0)),
                  pl.BlockSpec((1, D_MODEL), lambda i, j: (0, 0))],
        out_specs=pl.BlockSpec((PE_TB, D_MODEL), lambda i, j: (i, 0)),
        out_shape=jax.ShapeDtypeStruct((n_tok, D_MODEL), F32),
        scratch_shapes=[pltpu.VMEM((D_MODEL, PE_TB), F32),
                        pltpu.VMEM((HALF_ROWS, PE_TB), F32),
                        pltpu.VMEM((PE_EB, PE_TB), BF16)],
        compiler_params=_params(("arbitrary", "arbitrary")),
    )(h2, u_bf, vt_bf, s1_kh, scores_t, aux, x2, g2, fw)


def _pad_lanes(vec, start, width=SMALL_W):
    out = jnp.zeros((1, width), F32)
    return lax.dynamic_update_slice(out, vec.reshape(1, -1).astype(F32), (0, start))


def _block_diag(w):
    n, d, e = w.shape
    eye = jnp.eye(n, dtype=w.dtype)
    return (eye[:, None, :, None] * w[:, :, None, :]).reshape(n * d, n * e)


def kernel(x, c, w_ada, b_ada, norm_mix_w, norm_ffn_w, w_in, ssm_conv_w, ssm_conv_b, ssm_dt_bias,
           ssm_a_log, ssm_d, ssm_norm_w, gdn_conv_w, gdn_a_log, gdn_dt_bias, gdn_norm_w, gla_w_gate,
           gla_b_gate, gla_norm_w, lru_conv_w, lru_conv_b, lru_w_a, lru_b_a, lru_w_x, lru_b_x,
           lru_lambda, w_branch, w_out, peer_w_q, peer_sub_keys, peer_u, peer_v, final_norm_w):
    bsz, seq, d = x.shape
    n_layers = w_in.shape[0]
    n_tok = bsz * seq
    x2 = x.reshape(n_tok, d)

    c_pad = jnp.zeros((SUBLANES, d), F32).at[:bsz].set(c)
    mod = _ada_call(c_pad, w_ada, b_ada)

    row1 = lambda v: v.reshape(1, -1).astype(F32)

    for l in range(n_layers):
        m6 = mod[l, :bsz].reshape(bsz, 6, 1, d)
        sh1, sc1, g1, sh2, sc2, g2 = (m6[:, i] for i in range(6))
        w_perm = _permute_w_in(w_in[l])
        proj = _inproj_call(x2, row1(norm_mix_w[l]), sc1, sh1, w_perm, seq)

        ssd_p = dict(cwx=ssm_conv_w[l][:, :512], cbx=row1(ssm_conv_b[l][:512]),
                     cwbc=ssm_conv_w[l][:, 512:], cbbc=row1(ssm_conv_b[l][512:]),
                     dtb=_pad_lanes(ssm_dt_bias[l], SM_DT), alog=_pad_lanes(ssm_a_log[l], SM_DT),
                     dfull=row1(jnp.repeat(ssm_d[l], SSM_HEAD_DIM)), nw=row1(ssm_norm_w[l]))
        y_ssd = _ssd_call(proj, ssd_p, bsz, seq)

        gdn_p = dict(cwq=gdn_conv_w[l][:, :512], cwk=gdn_conv_w[l][:, 512:1024],
                     cwv=gdn_conv_w[l][:, 1024:], dtb=_pad_lanes(gdn_dt_bias[l], SM_DECAY),
                     alog=_pad_lanes(gdn_a_log[l], SM_DECAY),
                     nw=row1(jnp.tile(gdn_norm_w[l], GDN_HEADS)))
        y_gdn = _gdn_call(proj, gdn_p, bsz, seq)

        wg = jnp.zeros((SMALL_W, GLA_QK), F32).at[SM_LOW:SM_LOW + GLA_RANK].set(gla_w_gate[l])
        gla_p = dict(wg=wg.astype(BF16), bg=row1(gla_b_gate[l]),
                     nw=row1(jnp.tile(gla_norm_w[l], GLA_HEADS)))
        y_gla = _gla_call(proj, gla_p, bsz, seq)

        lru_p = dict(cw=lru_conv_w[l], cb=row1(lru_conv_b[l]),
                     wa=_block_diag(lru_w_a[l]).astype(BF16), ba=row1(lru_b_a[l]),
                     wx=_block_diag(lru_w_x[l]).astype(BF16), bx=row1(lru_b_x[l]),
                     lam=row1(lru_lambda[l]))
        y_lru = _lru_call(proj, lru_p, bsz, seq)

        x2 = _merge_call(x2, g1, proj, (y_ssd, y_gdn, y_gla, y_lru),
                         w_branch[l].astype(BF16), w_out[l].astype(BF16), seq)

        wqt = peer_w_q[l].reshape(d, PEER_HEADS, 2, PEER_HALF).transpose(2, 1, 3, 0)
        wqt = wqt.reshape(N_SCORE_ROWS, d).astype(BF16)
        keys = peer_sub_keys[l].transpose(1, 0, 2, 3).reshape(2 * PEER_HEADS, PEER_KEYS, PEER_HALF)
        h2, scores_t = _peer_score_call(x2, row1(norm_ffn_w[l]), sc2, sh2, wqt, keys.astype(BF16), seq)
        aux = _peer_thresh_call(scores_t)
        x2 = _peer_expert_call(h2, peer_u[l].astype(BF16), peer_v[l].T.astype(BF16), scores_t, aux,
                               x2, g2, row1(final_norm_w), seq, final_norm=(l == n_layers - 1))
    return x2.reshape(bsz, seq, d)
```

```python
import functools
import math

import jax
import jax.numpy as jnp
from jax import lax
from jax.experimental import pallas as pl
from jax.experimental.pallas import tpu as pltpu

F32 = jnp.float32
BF16 = jnp.bfloat16
HIGHEST = lax.Precision.HIGHEST
NEG_INF = float("-inf")

D_MODEL = 1024
N_LAYERS = 2
EPS = 1e-6
CONV_W = 4
BRANCH = 512
N_BRANCH = 4
SSM_HEADS = 8
SSM_HEAD_DIM = 64
SSM_GROUPS = 2
SSM_STATE = 64
GDN_HEADS = 4
GDN_DIM = 128
GLA_HEADS = 4
GLA_DK = 64
GLA_DV = 128
GLA_RANK = 16
GLA_TAU = 16.0
GLA_CHUNK = 16
LRU_BLOCKS = 8
LRU_BLOCK_DIM = 64
LRU_C = 8.0
PEER_HEADS = 8
PEER_KEYS = 128
PEER_EXPERTS = PEER_KEYS * PEER_KEYS
PEER_HALF = 128
PEER_TOPK = 16

LANES = 128
SUBLANES = 8
VMEM_LIMIT = 48 * 1024 * 1024

_SRC = {}
_off = 0
for _name, _w in (("ssm_z", 512), ("ssm_x", 512), ("ssm_b", 128), ("ssm_c", 128), ("ssm_dt", 8),
                  ("gdn_q", 512), ("gdn_k", 512), ("gdn_v", 512), ("gdn_z", 512), ("gdn_beta", 4),
                  ("gdn_decay", 4), ("gla_q", 256), ("gla_k", 256), ("gla_v", 512), ("gla_r", 512),
                  ("gla_low", 16), ("lru_x", 512), ("lru_gate", 512), ("merge", 4096)):
    _SRC[_name] = (_off, _w)
    _off += _w
D_IN = _off
_DST_ORDER = ("merge", "ssm_z", "ssm_x", "gdn_q", "gdn_k", "gdn_v", "gdn_z", "gla_q", "gla_k",
              "gla_v", "gla_r", "lru_x", "lru_gate", "ssm_b", "ssm_c", "ssm_dt", "gdn_beta",
              "gdn_decay", "gla_low")
SMALL_W = 256
N_COLS = 4096 + 11 * 512 + 256 + SMALL_W
U512 = {"ssm_z": 8, "ssm_x": 9, "gdn_q": 10, "gdn_k": 11, "gdn_v": 12, "gdn_z": 13, "gla_qk": 14,
        "gla_v": 15, "gla_r": 16, "lru_x": 17, "lru_gate": 18}
U256_BC = 38
U256_SMALL = 39
SM_DT = 0
SM_BETA = 8
SM_DECAY = 12
SM_LOW = 16


def _permute_w_in(w):
    parts = [w[:, _SRC[name][0]:_SRC[name][0] + _SRC[name][1]] for name in _DST_ORDER]
    used = sum(_SRC[name][1] for name in _DST_ORDER)
    parts.append(jnp.zeros((w.shape[0], N_COLS - used), w.dtype))
    return jnp.concatenate(parts, axis=1).astype(BF16)


def _mm(a, b):
    return jnp.dot(a.astype(BF16), b.astype(BF16), preferred_element_type=F32)


def _mm_nt(a, b):
    return lax.dot_general(a.astype(BF16), b.astype(BF16), (((1,), (1,)), ((), ())),
                           preferred_element_type=F32)


def _mm_tn(a, b):
    return lax.dot_general(a.astype(BF16), b.astype(BF16), (((0,), (0,)), ((), ())),
                           preferred_element_type=F32)


def _mm_hi(a, b):
    return jnp.dot(a, b, precision=HIGHEST, preferred_element_type=F32)


def _sigmoid(x):
    return 1.0 / (1.0 + jnp.exp(-x))


def _silu(x):
    return x * _sigmoid(x)


def _softplus(x):
    return jnp.maximum(x, 0.0) + jnp.log1p(jnp.exp(-jnp.abs(x)))


def _gelu(x):
    c = math.sqrt(2.0 / math.pi)
    return 0.5 * x * (1.0 + jnp.tanh(c * (x + 0.044715 * (x * x * x))))


def _rms_rows(x, w):
    ms = jnp.mean(x * x, axis=-1, keepdims=True)
    return x * lax.rsqrt(ms + EPS) * w


def _causal_conv(ext_ref, w_ref, n_rows):
    acc = None
    for k in range(CONV_W):
        term = w_ref[k:k + 1, :] * ext_ref[SUBLANES - CONV_W + 1 + k:SUBLANES - CONV_W + 1 + k + n_rows, :]
        acc = term if acc is None else acc + term
    return acc


def _params(sem, flags=None):
    return pltpu.CompilerParams(dimension_semantics=sem, vmem_limit_bytes=VMEM_LIMIT, flags=flags)


ADA_TN = 1536


def _ada_kernel(c_ref, w_ref, b_ref, o_ref):
    c = c_ref[...]
    o_ref[0] = _mm(_silu(c), w_ref[0]) + b_ref[0]


def _ada_call(c_pad, w_ada, b_ada):
    n_l = w_ada.shape[0]
    n_out = w_ada.shape[2]
    return pl.pallas_call(
        _ada_kernel,
        grid=(n_l, n_out // ADA_TN),
        in_specs=[pl.BlockSpec((SUBLANES, D_MODEL), lambda l, j: (0, 0)),
                  pl.BlockSpec((1, D_MODEL, ADA_TN), lambda l, j: (l, 0, j)),
                  pl.BlockSpec((1, 1, ADA_TN), lambda l, j: (l, 0, j))],
        out_specs=pl.BlockSpec((1, SUBLANES, ADA_TN), lambda l, j: (l, 0, j)),
        out_shape=jax.ShapeDtypeStruct((n_l, SUBLANES, n_out), F32),
        compiler_params=_params(("arbitrary", "arbitrary")),
    )(c_pad, w_ada, b_ada.reshape(n_l, 1, n_out))


INP_TM = 512
INP_TN = 2048


def _inproj_kernel(x_ref, nw_ref, sc_ref, sh_ref, w_ref, o_ref, h_scr):
    @pl.when(pl.program_id(1) == 0)
    def _():
        h = _rms_rows(x_ref[...], nw_ref[...])
        h = h * (1.0 + sc_ref[0]) + sh_ref[0]
        h_scr[...] = h.astype(BF16)

    o_ref[...] = jnp.dot(h_scr[...], w_ref[...], preferred_element_type=F32)


def _inproj_call(x2, nw, sc, sh, w_perm, seq):
    n_tok = x2.shape[0]
    per_b = seq // INP_TM
    return pl.pallas_call(
        _inproj_kernel,
        grid=(n_tok // INP_TM, N_COLS // INP_TN),
        in_specs=[pl.BlockSpec((INP_TM, D_MODEL), lambda i, j: (i, 0)),
                  pl.BlockSpec((1, D_MODEL), lambda i, j: (0, 0)),
                  pl.BlockSpec((1, 1, D_MODEL), lambda i, j: (i // per_b, 0, 0)),
                  pl.BlockSpec((1, 1, D_MODEL), lambda i, j: (i // per_b, 0, 0)),
                  pl.BlockSpec((D_MODEL, INP_TN), lambda i, j: (0, j))],
        out_specs=pl.BlockSpec((INP_TM, INP_TN), lambda i, j: (i, j)),
        out_shape=jax.ShapeDtypeStruct((n_tok, N_COLS), F32),
        scratch_shapes=[pltpu.VMEM((INP_TM, D_MODEL), BF16)],
        compiler_params=_params(("arbitrary", "arbitrary")),
    )(x2, nw, sc, sh, w_perm)


SSD_L = 128


def _ssd_kernel(z_ref, xs_ref, bc_ref, sm_ref, cwx_ref, cbx_ref, cwbc_ref, cbbc_ref, dtb_ref,
                alog_ref, dfull_ref, nw_ref, o_ref, extx, extbc, state):
    L = SSD_L
    t = pl.program_id(1)

    @pl.when(t == 0)
    def _():
        extx[0:SUBLANES, :] = jnp.zeros((SUBLANES, BRANCH), F32)
        extbc[0:SUBLANES, :] = jnp.zeros((SUBLANES, 256), F32)
        state[...] = jnp.zeros_like(state)

    extx[SUBLANES:SUBLANES + L, :] = xs_ref[...]
    extbc[SUBLANES:SUBLANES + L, :] = bc_ref[...]
    xs = _silu(_causal_conv(extx, cwx_ref, L) + cbx_ref[...])
    bc = _silu(_causal_conv(extbc, cwbc_ref, L) + cbbc_ref[...])
    extx[0:SUBLANES, :] = extx[L:L + SUBLANES, :]
    extbc[0:SUBLANES, :] = extbc[L:L + SUBLANES, :]

    dt = _softplus(sm_ref[...] + dtb_ref[...])
    da = dt * (-jnp.exp(alog_ref[...]))
    ri = lax.broadcasted_iota(jnp.int32, (L, L), 0)
    ci = lax.broadcasted_iota(jnp.int32, (L, L), 1)
    tril = ri >= ci
    cs = _mm_hi(tril.astype(F32), da)
    er = lax.broadcasted_iota(jnp.int32, (SMALL_W, BRANCH), 0)
    ec = lax.broadcasted_iota(jnp.int32, (SMALL_W, BRANCH), 1)
    expand = (er == (ec >> 6)).astype(F32)
    cs_full = _mm_hi(cs, expand)
    dt_full = _mm_hi(dt, expand)
    ecs_full = jnp.exp(cs_full)
    cs_last = cs_full[L - 1:L, :]
    w_full = jnp.exp(cs_last - cs_full)
    xdt = xs * dt_full
    xdtw = xdt * w_full
    cs_t = cs.T

    b128 = bc[:, 0:LANES]
    c128 = bc[:, LANES:2 * LANES]
    lane = lax.broadcasted_iota(jnp.int32, (1, LANES), 1)
    cg = [jnp.where(lane < SSM_STATE, c128, 0.0), jnp.where(lane >= SSM_STATE, c128, 0.0)]
    cb = [_mm_nt(cg[g], b128) for g in range(SSM_GROUPS)]

    y_pairs = []
    for p in range(SSM_HEADS // 2):
        xp = xdt[:, p * LANES:(p + 1) * LANES]
        yp = None
        for hh in range(2):
            h = 2 * p + hh
            g = h // (SSM_HEADS // SSM_GROUPS)
            col = cs[:, h:h + 1]
            row = cs_t[h:h + 1, :]
            seg = jnp.exp(jnp.where(tril, col - row, NEG_INF))
            att = cb[g] * seg
            hm = (lane < SSM_HEAD_DIM) if hh == 0 else (lane >= SSM_HEAD_DIM)
            term = _mm(att, jnp.where(hm, xp, 0.0))
            yp = term if yp is None else yp + term
        y_pairs.append(yp)
    y_diag = jnp.concatenate(y_pairs, axis=1)

    y_offs = []
    for g in range(SSM_GROUPS):
        sl = slice(g * 256, (g + 1) * 256)
        s_in = state[g]
        y_offs.append(_mm(cg[g], s_in) * ecs_full[:, sl])
        new = _mm_tn(b128, xdtw[:, sl])
        state[g] = s_in * ecs_full[L - 1:L, sl] + new
    y = y_diag + jnp.concatenate(y_offs, axis=1) + xs * dfull_ref[...]
    y = y * _silu(z_ref[...])
    o_ref[...] = _rms_rows(y, nw_ref[...]).astype(BF16)


def _ssd_call(proj, p, bsz, seq):
    nt = seq // SSD_L
    row = lambda blk: pl.BlockSpec((SSD_L, 512), lambda b, t: (b * nt + t, blk))
    full = lambda shape: pl.BlockSpec(shape, lambda b, t: (0,) * len(shape))
    return pl.pallas_call(
        _ssd_kernel,
        grid=(bsz, nt),
        in_specs=[row(U512["ssm_z"]), row(U512["ssm_x"]),
                  pl.BlockSpec((SSD_L, 256), lambda b, t: (b * nt + t, U256_BC)),
                  pl.BlockSpec((SSD_L, 256), lambda b, t: (b * nt + t, U256_SMALL)),
                  full((CONV_W, 512)), full((1, 512)), full((CONV_W, 256)), full((1, 256)),
                  full((1, SMALL_W)), full((1, SMALL_W)), full((1, 512)), full((1, 512))],
        out_specs=pl.BlockSpec((SSD_L, 512), lambda b, t: (b * nt + t, 0)),
        out_shape=jax.ShapeDtypeStruct((bsz * seq, BRANCH), BF16),
        scratch_shapes=[pltpu.VMEM((SUBLANES + SSD_L, 512), F32),
                        pltpu.VMEM((SUBLANES + SSD_L, 256), F32),
                        pltpu.VMEM((SSM_GROUPS, LANES, 256), F32)],
        compiler_params=_params(("arbitrary", "arbitrary")),
    )(proj, proj, proj, proj, p["cwx"], p["cbx"], p["cwbc"], p["cbbc"], p["dtb"], p["alog"],
      p["dfull"], p["nw"])


GDN_L = 64


def _gdn_kernel(q_ref, k_ref, v_ref, z_ref, sm_ref, cwq_ref, cwk_ref, cwv_ref, dtb_ref, alog_ref,
                nw_ref, o_ref, ext, state, *, bsz):
    L = GDN_L
    H = GDN_HEADS
    t = pl.program_id(0)

    @pl.when(t == 0)
    def _():
        ext[:, :, 0:SUBLANES, :] = jnp.zeros((3, bsz, SUBLANES, BRANCH), F32)
        state[...] = jnp.zeros_like(state)

    ri = lax.broadcasted_iota(jnp.int32, (L, L), 0)
    ci = lax.broadcasted_iota(jnp.int32, (L, L), 1)
    incl = ri >= ci
    strict = ri > ci
    eye = (ri == ci).astype(F32)
    bx = (ri >> 3) ^ (ci >> 3)
    blk = (bx > 0).astype(jnp.int32) + (bx > 1).astype(jnp.int32) + (bx > 3).astype(jnp.int32)
    tri = incl.astype(F32)
    nw = nw_ref[...]
    chains = [(b, h) for b in range(bsz) for h in range(H)]

    qkv, beta_all, gc_all, gc_t = [], [], [], []
    for b in range(bsz):
        outs = []
        for i, (r, w) in enumerate(((q_ref, cwq_ref), (k_ref, cwk_ref), (v_ref, cwv_ref))):
            e = ext.at[i, b]
            e[SUBLANES:SUBLANES + L, :] = r[b]
            outs.append(_silu(_causal_conv(e, w, L)))
            e[0:SUBLANES, :] = e[L:L + SUBLANES, :]
        qkv.append(outs)
        sm = sm_ref[b]
        beta_all.append(_sigmoid(sm))
        g_all = -jnp.exp(alog_ref[...]) * _softplus(sm + dtb_ref[...])
        gc = _mm_hi(tri, g_all)
        gc_all.append(gc)
        gc_t.append(gc.T)

    qs, ks, kbs, rhss, decays, egcs, glasts, gcbs = [], [], [], [], [], [], [], []
    for b, h in chains:
        sl = slice(h * GDN_DIM, (h + 1) * GDN_DIM)
        qh, kh, vh = (a[:, sl] for a in qkv[b])
        qh = qh * lax.rsqrt(jnp.sum(qh * qh, axis=-1, keepdims=True) + EPS) * (GDN_DIM ** -0.5)
        kh = kh * lax.rsqrt(jnp.sum(kh * kh, axis=-1, keepdims=True) + EPS)
        beta = beta_all[b][:, SM_BETA + h:SM_BETA + h + 1]
        gcol = gc_all[b][:, SM_DECAY + h:SM_DECAY + h + 1]
        grow = gc_t[b][SM_DECAY + h:SM_DECAY + h + 1, :]
        decays.append(jnp.exp(jnp.where(incl, gcol - grow, NEG_INF)))
        gcb = jnp.broadcast_to(gcol, (L, GDN_DIM))
        egc = jnp.exp(gcb)
        kb = kh * beta
        qs.append(qh); ks.append(kh); kbs.append(kb); gcbs.append(gcb); egcs.append(egc)
        glasts.append(gcb[L - 1:L, :])
        rhss.append(jnp.concatenate([vh * beta, kb * egc], axis=1))

    n = len(chains)
    rng = range(n)
    kk = [_mm_nt(kbs[c], ks[c]) for c in rng]
    qk = [_mm_nt(qs[c], ks[c]) for c in rng]
    ms = [jnp.where(strict, kk[c] * decays[c], 0.0) for c in rng]
    mds = [jnp.where(blk == 0, ms[c], 0.0) for c in rng]
    p2 = [_mm(mds[c], mds[c]) for c in rng]
    base = [eye - mds[c] for c in rng]
    bp = [_mm(base[c], p2[c]) for c in rng]
    p4 = [_mm(p2[c], p2[c]) for c in rng]
    base = [base[c] + bp[c] for c in rng]
    bq = [_mm(base[c], p4[c]) for c in rng]
    inv = [base[c] + bq[c] for c in rng]
    for lvl in range(1, int(math.log2(L // SUBLANES)) + 1):
        oi = [_mm(jnp.where(blk == lvl, ms[c], 0.0), inv[c]) for c in rng]
        ioi = [_mm(inv[c], oi[c]) for c in rng]
        inv = [inv[c] - ioi[c] for c in rng]
    sol = [_mm(inv[c], rhss[c]) for c in rng]
    s_in = [state[c] for c in rng]
    ws = [_mm(sol[c][:, GDN_DIM:], s_in[c]) for c in rng]
    qgs = [_mm(qs[c] * egcs[c], s_in[c]) for c in rng]
    v_new = [sol[c][:, :GDN_DIM] - ws[c] for c in rng]
    av = [_mm(qk[c] * decays[c], v_new[c]) for c in rng]
    kv = [_mm_tn(ks[c] * jnp.exp(glasts[c] - gcbs[c]), v_new[c]) for c in rng]
    for c in rng:
        state[c] = s_in[c] * jnp.exp(glasts[c]) + kv[c]
    for b in range(bsz):
        z = z_ref[b]
        outs = []
        for h in range(H):
            c = b * H + h
            sl = slice(h * GDN_DIM, (h + 1) * GDN_DIM)
            outs.append(_rms_rows(qgs[c] + av[c], nw[:, sl]) * _silu(z[:, sl]))
        o_ref[b] = jnp.concatenate(outs, axis=1).astype(BF16)


def _gdn_call(proj3, p, bsz, seq):
    nt = seq // GDN_L
    row = lambda blk, w=512: pl.BlockSpec((bsz, GDN_L, w), lambda t: (0, t, blk))
    full = lambda shape: pl.BlockSpec(shape, lambda t: (0,) * len(shape))
    return pl.pallas_call(
        functools.partial(_gdn_kernel, bsz=bsz),
        grid=(nt,),
        in_specs=[row(U512["gdn_q"]), row(U512["gdn_k"]), row(U512["gdn_v"]), row(U512["gdn_z"]),
                  row(U256_SMALL, 256),
                  full((CONV_W, 512)), full((CONV_W, 512)), full((CONV_W, 512)),
                  full((1, SMALL_W)), full((1, SMALL_W)), full((1, 512))],
        out_specs=pl.BlockSpec((bsz, GDN_L, 512), lambda t: (0, t, 0)),
        out_shape=jax.ShapeDtypeStruct((bsz, seq, BRANCH), BF16),
        scratch_shapes=[pltpu.VMEM((3, bsz, SUBLANES + GDN_L, 512), F32),
                        pltpu.VMEM((bsz * GDN_HEADS, GDN_DIM, GDN_DIM), F32)],
        compiler_params=_params(("arbitrary",)),
    )(proj3, proj3, proj3, proj3, proj3, p["cwq"], p["cwk"], p["cwv"], p["dtb"], p["alog"], p["nw"])


GLA_T = 128
GLA_QK = GLA_HEADS * GLA_DK


def _gla_kernel(qk_ref, v_ref, r_ref, sm_ref, wg_ref, bg_ref, nw_ref, o_ref, kbuf, gbuf, vbuf, state):
    T = GLA_T
    C = GLA_CHUNK
    t = pl.program_id(1)

    @pl.when(t == 0)
    def _():
        kbuf[0:C, :] = jnp.zeros((C, GLA_QK), F32)
        gbuf[0:C, :] = jnp.zeros((C, GLA_QK), F32)
        vbuf[0:C, :] = jnp.zeros((C, BRANCH), F32)
        state[...] = jnp.zeros_like(state)

    qk = qk_ref[...]
    q = qk[:, :GLA_QK] * (GLA_DK ** -0.5)
    k = qk[:, GLA_QK:]
    v = v_ref[...]
    pre = _mm(sm_ref[...], wg_ref[...]) + bg_ref[...]
    log_a = (jnp.minimum(pre, 0.0) - jnp.log1p(jnp.exp(-jnp.abs(pre)))) / GLA_TAU
    ri = lax.broadcasted_iota(jnp.int32, (T, T), 0)
    ci = lax.broadcasted_iota(jnp.int32, (T, T), 1)
    blocktri = ((ri >> 4) == (ci >> 4)) & (ri >= ci)
    G = _mm_hi(blocktri.astype(F32), log_a)
    kbuf[C:C + T, :] = k
    gbuf[C:C + T, :] = G
    vbuf[C:C + T, :] = v

    rr = lax.broadcasted_iota(jnp.int32, (GLA_QK, BRANCH), 0)
    rc = lax.broadcasted_iota(jnp.int32, (GLA_QK, BRANCH), 1)
    red = ((rr >> 6) == (rc >> 7)).astype(BF16)
    rmod = lax.broadcasted_iota(jnp.int32, (T, 1), 0) & (C - 1)
    o = jnp.zeros((T, BRANCH), F32)
    for d in range(C):
        ks = kbuf[C - d:C - d + T, :]
        gs = gbuf[C - d:C - d + T, :]
        vs = vbuf[C - d:C - d + T, :]
        prod = jnp.where(rmod >= d, q * ks * jnp.exp(G - gs), 0.0)
        hi = prod.astype(BF16)
        lo = (prod - hi.astype(F32)).astype(BF16)
        a_full = (jnp.dot(hi, red, preferred_element_type=F32)
                  + jnp.dot(lo, red, preferred_element_type=F32))
        o = o + a_full * vs

    sr = lax.broadcasted_iota(jnp.int32, (BRANCH, GLA_QK), 0)
    sc = lax.broadcasted_iota(jnp.int32, (BRANCH, GLA_QK), 1)
    blockdiag = (sr >> 7) == (sc >> 6)
    st = state[...]
    inter = []
    for c in range(T // C):
        rows = slice(c * C, (c + 1) * C)
        gc = G[rows]
        glast = gc[C - 1:C, :]
        qg = q[rows] * jnp.exp(gc)
        kd = k[rows] * jnp.exp(glast - gc)
        inter.append(_mm_nt(qg, st))
        st = jnp.where(blockdiag, st * jnp.exp(glast) + _mm_tn(v[rows], kd), 0.0)
    state[...] = st
    o = o + jnp.concatenate(inter, axis=0)
    r = r_ref[...]
    nw = nw_ref[...]
    outs = []
    for h in range(GLA_HEADS):
        sl = slice(h * GLA_DV, (h + 1) * GLA_DV)
        outs.append(_rms_rows(o[:, sl], nw[:, sl]) * _silu(r[:, sl]))
    o_ref[...] = jnp.concatenate(outs, axis=1).astype(BF16)


def _gla_call(proj, p, bsz, seq):
    nt = seq // GLA_T
    row = lambda blk: pl.BlockSpec((GLA_T, 512), lambda b, t: (b * nt + t, blk))
    full = lambda shape: pl.BlockSpec(shape, lambda b, t: (0,) * len(shape))
    return pl.pallas_call(
        _gla_kernel,
        grid=(bsz, nt),
        in_specs=[row(U512["gla_qk"]), row(U512["gla_v"]), row(U512["gla_r"]),
                  pl.BlockSpec((GLA_T, 256), lambda b, t: (b * nt + t, U256_SMALL)),
                  full((SMALL_W, GLA_QK)), full((1, GLA_QK)), full((1, 512))],
        out_specs=pl.BlockSpec((GLA_T, 512), lambda b, t: (b * nt + t, 0)),
        out_shape=jax.ShapeDtypeStruct((bsz * seq, BRANCH), BF16),
        scratch_shapes=[pltpu.VMEM((GLA_CHUNK + GLA_T, GLA_QK), F32),
                        pltpu.VMEM((GLA_CHUNK + GLA_T, GLA_QK), F32),
                        pltpu.VMEM((GLA_CHUNK + GLA_T, BRANCH), F32),
                        pltpu.VMEM((BRANCH, GLA_QK), F32)],
        compiler_params=_params(("arbitrary", "arbitrary")),
    )(proj, proj, proj, proj, p["wg"], p["bg"], p["nw"])


LRU_T = 256
LRU_PAD = LRU_T // 2


def _lru_kernel(x_ref, gate_ref, cw_ref, cb_ref, wa_ref, ba_ref, wx_ref, bx_ref, lam_ref, o_ref,
                ext, abuf, hbuf, carry):
    T = LRU_T
    P = LRU_PAD
    t = pl.program_id(1)

    @pl.when(t == 0)
    def _():
        ext[0:SUBLANES, :] = jnp.zeros((SUBLANES, BRANCH), F32)
        abuf[0:P, :] = jnp.ones((P, BRANCH), F32)
        hbuf[0:P, :] = jnp.zeros((P, BRANCH), F32)
        carry[...] = jnp.zeros_like(carry)

    ext[SUBLANES:SUBLANES + T, :] = x_ref[...]
    xc = _causal_conv(ext, cw_ref, T) + cb_ref[...]
    ext[0:SUBLANES, :] = ext[T:T + SUBLANES, :]
    gate_r = _sigmoid(_mm(xc, wa_ref[...]) + ba_ref[...])
    gate_i = _sigmoid(_mm(xc, wx_ref[...]) + bx_ref[...])
    log_a = -LRU_C * gate_r * _softplus(-lam_ref[...])
    abuf[P:P + T, :] = jnp.exp(log_a)
    th = jnp.tanh(log_a)
    hbuf[P:P + T, :] = jnp.sqrt(-2.0 * th / (1.0 - th)) * (gate_i * xc)
    s = 1
    while s < T:
        a_cur = abuf[P:P + T, :]
        h_cur = hbuf[P:P + T, :]
        a_sh = abuf[P - s:P - s + T, :]
        h_sh = hbuf[P - s:P - s + T, :]
        hbuf[P:P + T, :] = h_cur + a_cur * h_sh
        abuf[P:P + T, :] = a_cur * a_sh
        s *= 2
    h = hbuf[P:P + T, :] + abuf[P:P + T, :] * carry[0:1, :]
    carry[0:1, :] = h[T - 1:T, :]
    o_ref[...] = (h * _gelu(gate_ref[...])).astype(BF16)


def _lru_call(proj, p, bsz, seq):
    nt = seq // LRU_T
    row = lambda blk: pl.BlockSpec((LRU_T, 512), lambda b, t: (b * nt + t, blk))
    full = lambda shape: pl.BlockSpec(shape, lambda b, t: (0,) * len(shape))
    return pl.pallas_call(
        _lru_kernel,
        grid=(bsz, nt),
        in_specs=[row(U512["lru_x"]), row(U512["lru_gate"]),
                  full((CONV_W, 512)), full((1, 512)), full((512, 512)), full((1, 512)),
                  full((512, 512)), full((1, 512)), full((1, 512))],
        out_specs=pl.BlockSpec((LRU_T, 512), lambda b, t: (b * nt + t, 0)),
        out_shape=jax.ShapeDtypeStruct((bsz * seq, BRANCH), BF16),
        scratch_shapes=[pltpu.VMEM((SUBLANES + LRU_T, 512), F32),
                        pltpu.VMEM((LRU_PAD + LRU_T, 512), F32),
                        pltpu.VMEM((LRU_PAD + LRU_T, 512), F32),
                        pltpu.VMEM((SUBLANES, 512), F32)],
        compiler_params=_params(("arbitrary", "arbitrary")),
    )(proj, proj, p["cw"], p["cb"], p["wa"], p["ba"], p["wx"], p["bx"], p["lam"])


MRG_TM = 256


def _merge_kernel(x_ref, g_ref, lg_ref, y0_ref, y1_ref, y2_ref, y3_ref, wb_ref, wo_ref, o_ref):
    merged = None
    for i, y_ref in enumerate((y0_ref, y1_ref, y2_ref, y3_ref)):
        br = jnp.dot(y_ref[...], wb_ref[i], preferred_element_type=F32)
        term = _sigmoid(lg_ref[:, i * D_MODEL:(i + 1) * D_MODEL]) * br
        merged = term if merged is None else merged + term
    out = jnp.dot(merged.astype(BF16), wo_ref[...], preferred_element_type=F32)
    o_ref[...] = x_ref[...] + g_ref[0] * out


def _merge_call(x2, g1, proj, ys, wb, wo, seq):
    n_tok = x2.shape[0]
    per_b = seq // MRG_TM
    yspec = pl.BlockSpec((MRG_TM, BRANCH), lambda i: (i, 0))
    return pl.pallas_call(
        _merge_kernel,
        grid=(n_tok // MRG_TM,),
        in_specs=[pl.BlockSpec((MRG_TM, D_MODEL), lambda i: (i, 0)),
                  pl.BlockSpec((1, 1, D_MODEL), lambda i: (i // per_b, 0, 0)),
                  pl.BlockSpec((MRG_TM, N_BRANCH * D_MODEL), lambda i: (i, 0)),
                  yspec, yspec, yspec, yspec,
                  pl.BlockSpec((N_BRANCH, BRANCH, D_MODEL), lambda i: (0, 0, 0)),
                  pl.BlockSpec((D_MODEL, D_MODEL), lambda i: (0, 0))],
        out_specs=pl.BlockSpec((MRG_TM, D_MODEL), lambda i: (i, 0)),
        out_shape=jax.ShapeDtypeStruct((n_tok, D_MODEL), F32),
        compiler_params=_params(("arbitrary",)),
    )(x2, g1, proj, *ys, wb, wo)


PS_TB = 512
N_SCORE_ROWS = 2 * PEER_HEADS * PEER_KEYS


def _peer_score_kernel(x_ref, nw_ref, sc_ref, sh_ref, wqt_ref, keys_ref, h2t_ref, st_ref):
    h = _rms_rows(x_ref[...], nw_ref[...])
    h = h * (1.0 + sc_ref[0]) + sh_ref[0]
    ht = h.T.astype(BF16)
    h2t_ref[...] = ht
    qt = jnp.dot(wqt_ref[...], ht, preferred_element_type=F32).astype(BF16)
    for g in range(2 * PEER_HEADS):
        rows = slice(g * PEER_KEYS, (g + 1) * PEER_KEYS)
        st_ref[rows, :] = jnp.dot(keys_ref[g], qt[rows, :], preferred_element_type=F32)


def _peer_score_call(x2, nw, sc, sh, wqt, keys, seq):
    n_tok = x2.shape[0]
    per_b = seq // PS_TB
    return pl.pallas_call(
        _peer_score_kernel,
        grid=(n_tok // PS_TB,),
        in_specs=[pl.BlockSpec((PS_TB, D_MODEL), lambda i: (i, 0)),
                  pl.BlockSpec((1, D_MODEL), lambda i: (0, 0)),
                  pl.BlockSpec((1, 1, D_MODEL), lambda i: (i // per_b, 0, 0)),
                  pl.BlockSpec((1, 1, D_MODEL), lambda i: (i // per_b, 0, 0)),
                  pl.BlockSpec((N_SCORE_ROWS, D_MODEL), lambda i: (0, 0)),
                  pl.BlockSpec((2 * PEER_HEADS, PEER_KEYS, PEER_HALF), lambda i: (0, 0, 0))],
        out_specs=[pl.BlockSpec((D_MODEL, PS_TB), lambda i: (0, i)),
                   pl.BlockSpec((N_SCORE_ROWS, PS_TB), lambda i: (0, i))],
        out_shape=[jax.ShapeDtypeStruct((D_MODEL, n_tok), BF16),
                   jax.ShapeDtypeStruct((N_SCORE_ROWS, n_tok), F32)],
        compiler_params=_params(("arbitrary",)),
    )(x2, nw, sc, sh, wqt, keys)


PT_TL = 256
NOT_RANKED = 255.0
_CAND = [(i, j) for i in range(PEER_TOPK) for j in range(PEER_TOPK) if (i + 1) * (j + 1) <= PEER_TOPK]
N_CAND_ROWS = -(-len(_CAND) // SUBLANES) * SUBLANES


def _pop_max(x, iota):
    m = jnp.max(x, axis=0, keepdims=True)
    first = jnp.min(jnp.where(x == m, iota, float(x.shape[0])), axis=0, keepdims=True)
    return m, first, jnp.where(iota == first, NEG_INF, x)


def _peer_gate_kernel(st_ref, ce_ref, r2_ref, e2_ref, cand):
    TL = PT_TL
    iota_k = lax.broadcasted_iota(jnp.int32, (PEER_KEYS, TL), 0).astype(F32)
    iota_c = lax.broadcasted_iota(jnp.int32, (N_CAND_ROWS, TL), 0).astype(F32)
    cand[...] = jnp.full((N_CAND_ROWS, TL), NEG_INF, F32)
    for h in range(PEER_HEADS):
        rows1 = slice(h * PEER_KEYS, (h + 1) * PEER_KEYS)
        rows2 = slice((PEER_HEADS + h) * PEER_KEYS, (PEER_HEADS + h + 1) * PEER_KEYS)
        x = st_ref[rows1, :]
        t1, first1 = [], []
        for _ in range(PEER_TOPK):
            m, f, x = _pop_max(x, iota_k)
            t1.append(m)
            first1.append(f)
        x = st_ref[rows2, :]
        t2 = []
        rank2 = jnp.full((PEER_KEYS, TL), NOT_RANKED, F32)
        for r in range(PEER_TOPK):
            m, f, x = _pop_max(x, iota_k)
            t2.append(m)
            rank2 = jnp.where(iota_k == f, float(r), rank2)
        for n, (i, j) in enumerate(_CAND):
            cand[n:n + 1, :] = t1[i] + t2[j]
        c = cand[...]
        x = c
        tau = None
        for _ in range(PEER_TOPK):
            tau, _, x = _pop_max(x, iota_c)
        m1, m2 = t1[0], t2[0]
        zsum = jnp.sum(jnp.where(c >= tau, jnp.exp(c - (m1 + m2)), 0.0), axis=0, keepdims=True)
        count1 = jnp.zeros((PEER_KEYS, TL), F32)
        for i in range(PEER_TOPK):
            cnt = None
            for j in range(PEER_TOPK):
                if (i + 1) * (j + 1) <= PEER_TOPK:
                    hit = jnp.where(t1[i] + t2[j] >= tau, 1.0, 0.0)
                    cnt = hit if cnt is None else cnt + hit
            count1 = jnp.where(iota_k == first1[i], cnt, count1)
        ce_ref[rows1, :] = count1
        ce_ref[rows2, :] = jnp.exp(st_ref[rows1, :] - m1) * (0.5 / zsum)
        r2_ref[rows1, :] = rank2.astype(BF16)
        e2_ref[rows1, :] = jnp.exp(st_ref[rows2, :] - m2).astype(BF16)


def _peer_gate_call(scores_t):
    n_tok = scores_t.shape[1]
    half = PEER_HEADS * PEER_KEYS
    return pl.pallas_call(
        _peer_gate_kernel,
        grid=(n_tok // PT_TL,),
        in_specs=[pl.BlockSpec((N_SCORE_ROWS, PT_TL), lambda i: (0, i))],
        out_specs=[pl.BlockSpec((N_SCORE_ROWS, PT_TL), lambda i: (0, i)),
                   pl.BlockSpec((half, PT_TL), lambda i: (0, i)),
                   pl.BlockSpec((half, PT_TL), lambda i: (0, i))],
        out_shape=[jax.ShapeDtypeStruct((N_SCORE_ROWS, n_tok), F32),
                   jax.ShapeDtypeStruct((half, n_tok), BF16),
                   jax.ShapeDtypeStruct((half, n_tok), BF16)],
        scratch_shapes=[pltpu.VMEM((N_CAND_ROWS, PT_TL), F32)],
        compiler_params=_params(("arbitrary",)),
    )(scores_t)


PE_TB = 512
PE_EB = 512
PE_SPLIT = 1
HALF_ROWS = PEER_HEADS * PEER_KEYS


def _bcast_rows_bf16(row):
    r16 = jnp.broadcast_to(row, (2 * SUBLANES, LANES)).astype(BF16)
    return jnp.concatenate([r16] * (PEER_KEYS // (2 * SUBLANES)), axis=0)


def _peer_expert_kernel(h2t_ref, u_ref, vt_ref, ce_ref, r2_ref, e2_ref, x_ref, g_ref, fw_ref,
                        o_ref, acc, zt, *, final_norm):
    j = pl.program_id(1)
    H = PEER_HEADS

    @pl.when(j == 0)
    def _():
        acc[...] = jnp.zeros_like(acc)

    c0 = math.sqrt(2.0 / math.pi)
    zero = jnp.zeros((PEER_KEYS, LANES), BF16)
    hw = PE_TB // PE_SPLIT
    for hf in range(PE_SPLIT):
        hs = slice(hf * hw, (hf + 1) * hw)
        st = jnp.dot(u_ref[...], h2t_ref[:, hs], preferred_element_type=F32)
        for q in range(PE_EB // PEER_KEYS):
            ceq = ce_ref[q]
            rs = slice(q * PEER_KEYS, (q + 1) * PEER_KEYS)
            for lc in range(hw // LANES):
                ls = slice(hf * hw + lc * LANES, hf * hw + (lc + 1) * LANES)
                w = None
                for h in range(H):
                    rows = slice(h * PEER_KEYS, (h + 1) * PEER_KEYS)
                    cnt = _bcast_rows_bf16(ceq[h:h + 1, ls])
                    g1 = _bcast_rows_bf16(ceq[H + h:H + h + 1, ls])
                    sel_g1 = jnp.minimum(jnp.maximum(cnt - r2_ref[rows, ls], zero), g1)
                    term = sel_g1 * e2_ref[rows, ls]
                    w = term if w is None else w + term
                x = st[rs, lc * LANES:(lc + 1) * LANES]
                th = jnp.tanh(x * (c0 + (c0 * 0.044715) * (x * x)))
                zt[rs, ls] = (x + x * th).astype(BF16) * w
        acc[:, hs] += jnp.dot(vt_ref[...], zt[:, hs], preferred_element_type=F32)

    @pl.when(j == pl.num_programs(1) - 1)
    def _():
        xn = x_ref[...] + g_ref[0] * acc[...].T
        if final_norm:
            xn = _rms_rows(xn, fw_ref[...])
        o_ref[...] = xn


def _peer_expert_call(h2, u_bf, vt_bf, ce, r2, e2, x2, g2, fw, seq, final_norm):
    n_tok = x2.shape[0]
    per_b = seq // PE_TB
    kern = functools.partial(_peer_expert_kernel, final_norm=final_norm)
    ce_k = ce.reshape(2, PEER_HEADS, PEER_KEYS, n_tok).transpose(2, 0, 1, 3)
    ce_k = ce_k.reshape(PEER_KEYS, 2 * PEER_HEADS, n_tok)
    return pl.pallas_call(
        kern,
        grid=(n_tok // PE_TB, PEER_EXPERTS // PE_EB),
        in_specs=[pl.BlockSpec((D_MODEL, PE_TB), lambda i, j: (0, i)),
                  pl.BlockSpec((PE_EB, D_MODEL), lambda i, j: (j, 0)),
                  pl.BlockSpec((D_MODEL, PE_EB), lambda i, j: (0, j)),
                  pl.BlockSpec((PE_EB // PEER_KEYS, 2 * PEER_HEADS, PE_TB), lambda i, j: (j, 0, i)),
                  pl.BlockSpec((HALF_ROWS, PE_TB), lambda i, j: (0, i)),
                  pl.BlockSpec((HALF_ROWS, PE_TB), lambda i, j: (0, i)),
                  pl.BlockSpec((PE_TB, D_MODEL), lambda i, j: (i, 0)),
                  pl.BlockSpec((1, 1, D_MODEL), lambda i, j: (i // per_b, 0, 0)),
                  pl.BlockSpec((1, D_MODEL), lambda i, j: (0, 0))],
        out_specs=pl.BlockSpec((PE_TB, D_MODEL), lambda i, j: (i, 0)),
        out_shape=jax.ShapeDtypeStruct((n_tok, D_MODEL), F32),
        scratch_shapes=[pltpu.VMEM((D_MODEL, PE_TB), F32),
                        pltpu.VMEM((PE_EB, PE_TB), BF16)],
        compiler_params=_params(("arbitrary", "arbitrary")),
    )(h2, u_bf, vt_bf, ce_k, r2, e2, x2, g2, fw)


def _pad_lanes(vec, start, width=SMALL_W):
    out = jnp.zeros((1, width), F32)
    return lax.dynamic_update_slice(out, vec.reshape(1, -1).astype(F32), (0, start))


def _block_diag(w):
    n, d, e = w.shape
    eye = jnp.eye(n, dtype=w.dtype)
    return (eye[:, None, :, None] * w[:, :, None, :]).reshape(n * d, n * e)


def kernel(x, c, w_ada, b_ada, norm_mix_w, norm_ffn_w, w_in, ssm_conv_w, ssm_conv_b, ssm_dt_bias,
           ssm_a_log, ssm_d, ssm_norm_w, gdn_conv_w, gdn_a_log, gdn_dt_bias, gdn_norm_w, gla_w_gate,
           gla_b_gate, gla_norm_w, lru_conv_w, lru_conv_b, lru_w_a, lru_b_a, lru_w_x, lru_b_x,
           lru_lambda, w_branch, w_out, peer_w_q, peer_sub_keys, peer_u, peer_v, final_norm_w):
    bsz, seq, d = x.shape
    n_layers = w_in.shape[0]
    n_tok = bsz * seq
    x2 = x.reshape(n_tok, d)

    c_pad = jnp.zeros((SUBLANES, d), F32).at[:bsz].set(c)
    mod = _ada_call(c_pad, w_ada, b_ada)

    row1 = lambda v: v.reshape(1, -1).astype(F32)

    for l in range(n_layers):
        m6 = mod[l, :bsz].reshape(bsz, 6, 1, d)
        sh1, sc1, g1, sh2, sc2, g2 = (m6[:, i] for i in range(6))
        w_perm = _permute_w_in(w_in[l])
        proj = _inproj_call(x2, row1(norm_mix_w[l]), sc1, sh1, w_perm, seq)

        ssd_p = dict(cwx=ssm_conv_w[l][:, :512], cbx=row1(ssm_conv_b[l][:512]),
                     cwbc=ssm_conv_w[l][:, 512:], cbbc=row1(ssm_conv_b[l][512:]),
                     dtb=_pad_lanes(ssm_dt_bias[l], SM_DT), alog=_pad_lanes(ssm_a_log[l], SM_DT),
                     dfull=row1(jnp.repeat(ssm_d[l], SSM_HEAD_DIM)), nw=row1(ssm_norm_w[l]))
        y_ssd = _ssd_call(proj, ssd_p, bsz, seq)

        gdn_p = dict(cwq=gdn_conv_w[l][:, :512], cwk=gdn_conv_w[l][:, 512:1024],
                     cwv=gdn_conv_w[l][:, 1024:], dtb=_pad_lanes(gdn_dt_bias[l], SM_DECAY),
                     alog=_pad_lanes(gdn_a_log[l], SM_DECAY),
                     nw=row1(jnp.tile(gdn_norm_w[l], GDN_HEADS)))
        y_gdn = _gdn_call(proj.reshape(bsz, seq, N_COLS), gdn_p, bsz, seq).reshape(n_tok, BRANCH)

        wg = jnp.zeros((SMALL_W, GLA_QK), F32).at[SM_LOW:SM_LOW + GLA_RANK].set(gla_w_gate[l])
        gla_p = dict(wg=wg.astype(BF16), bg=row1(gla_b_gate[l]),
                     nw=row1(jnp.tile(gla_norm_w[l], GLA_HEADS)))
        y_gla = _gla_call(proj, gla_p, bsz, seq)

        lru_p = dict(cw=lru_conv_w[l], cb=row1(lru_conv_b[l]),
                     wa=_block_diag(lru_w_a[l]).astype(BF16), ba=row1(lru_b_a[l]),
                     wx=_block_diag(lru_w_x[l]).astype(BF16), bx=row1(lru_b_x[l]),
                     lam=row1(lru_lambda[l]))
        y_lru = _lru_call(proj, lru_p, bsz, seq)

        x2 = _merge_call(x2, g1, proj, (y_ssd, y_gdn, y_gla, y_lru),
                         w_branch[l].astype(BF16), w_out[l].astype(BF16), seq)

        wqt = peer_w_q[l].reshape(d, PEER_HEADS, 2, PEER_HALF).transpose(2, 1, 3, 0)
        wqt = wqt.reshape(N_SCORE_ROWS, d).astype(BF16)
        keys = peer_sub_keys[l].transpose(1, 0, 2, 3).reshape(2 * PEER_HEADS, PEER_KEYS, PEER_HALF)
        h2, scores_t = _peer_score_call(x2, row1(norm_ffn_w[l]), sc2, sh2, wqt, keys.astype(BF16), seq)
        ce, r2, e2 = _peer_gate_call(scores_t)
        x2 = _peer_expert_call(h2, peer_u[l].astype(BF16), peer_v[l].T.astype(BF16), ce, r2, e2,
                               x2, g2, row1(final_norm_w), seq, final_norm=(l == n_layers - 1))
    return x2.reshape(bsz, seq, d)
```

```python
import functools
import math

import jax
import jax.numpy as jnp
from jax import lax
from jax.experimental import pallas as pl
from jax.experimental.pallas import tpu as pltpu

F32 = jnp.float32
BF16 = jnp.bfloat16
HIGHEST = lax.Precision.HIGHEST
NEG_INF = float("-inf")

D_MODEL = 1024
N_LAYERS = 2
EPS = 1e-6
CONV_W = 4
BRANCH = 512
N_BRANCH = 4
SSM_HEADS = 8
SSM_HEAD_DIM = 64
SSM_GROUPS = 2
SSM_STATE = 64
GDN_HEADS = 4
GDN_DIM = 128
GLA_HEADS = 4
GLA_DK = 64
GLA_DV = 128
GLA_RANK = 16
GLA_TAU = 16.0
GLA_CHUNK = 16
LRU_BLOCKS = 8
LRU_BLOCK_DIM = 64
LRU_C = 8.0
PEER_HEADS = 8
PEER_KEYS = 128
PEER_EXPERTS = PEER_KEYS * PEER_KEYS
PEER_HALF = 128
PEER_TOPK = 16

LANES = 128
SUBLANES = 8
VMEM_LIMIT = 48 * 1024 * 1024

_SRC = {}
_off = 0
for _name, _w in (("ssm_z", 512), ("ssm_x", 512), ("ssm_b", 128), ("ssm_c", 128), ("ssm_dt", 8),
                  ("gdn_q", 512), ("gdn_k", 512), ("gdn_v", 512), ("gdn_z", 512), ("gdn_beta", 4),
                  ("gdn_decay", 4), ("gla_q", 256), ("gla_k", 256), ("gla_v", 512), ("gla_r", 512),
                  ("gla_low", 16), ("lru_x", 512), ("lru_gate", 512), ("merge", 4096)):
    _SRC[_name] = (_off, _w)
    _off += _w
D_IN = _off
_DST_ORDER = ("merge", "ssm_z", "ssm_x", "gdn_q", "gdn_k", "gdn_v", "gdn_z", "gla_q", "gla_k",
              "gla_v", "gla_r", "lru_x", "lru_gate", "ssm_b", "ssm_c", "ssm_dt", "gdn_beta",
              "gdn_decay", "gla_low")
SMALL_W = 256
N_COLS = 4096 + 11 * 512 + 256 + SMALL_W
U512 = {"ssm_z": 8, "ssm_x": 9, "gdn_q": 10, "gdn_k": 11, "gdn_v": 12, "gdn_z": 13, "gla_qk": 14,
        "gla_v": 15, "gla_r": 16, "lru_x": 17, "lru_gate": 18}
U256_BC = 38
U256_SMALL = 39
SM_DT = 0
SM_BETA = 8
SM_DECAY = 12
SM_LOW = 16


def _permute_w_in(w):
    parts = [w[:, _SRC[name][0]:_SRC[name][0] + _SRC[name][1]] for name in _DST_ORDER]
    used = sum(_SRC[name][1] for name in _DST_ORDER)
    parts.append(jnp.zeros((w.shape[0], N_COLS - used), w.dtype))
    return jnp.concatenate(parts, axis=1).astype(BF16)


def _mm(a, b):
    return jnp.dot(a.astype(BF16), b.astype(BF16), preferred_element_type=F32)


def _mm_nt(a, b):
    return lax.dot_general(a.astype(BF16), b.astype(BF16), (((1,), (1,)), ((), ())),
                           preferred_element_type=F32)


def _mm_tn(a, b):
    return lax.dot_general(a.astype(BF16), b.astype(BF16), (((0,), (0,)), ((), ())),
                           preferred_element_type=F32)


def _mm_hi(a, b):
    return jnp.dot(a, b, precision=HIGHEST, preferred_element_type=F32)


def _sigmoid(x):
    return 1.0 / (1.0 + jnp.exp(-x))


def _silu(x):
    return x * _sigmoid(x)


def _softplus(x):
    return jnp.maximum(x, 0.0) + jnp.log1p(jnp.exp(-jnp.abs(x)))


def _gelu(x):
    c = math.sqrt(2.0 / math.pi)
    return 0.5 * x * (1.0 + jnp.tanh(c * (x + 0.044715 * (x * x * x))))


def _rms_rows(x, w):
    ms = jnp.mean(x * x, axis=-1, keepdims=True)
    return x * lax.rsqrt(ms + EPS) * w


def _causal_conv(ext_ref, w_ref, n_rows):
    acc = None
    for k in range(CONV_W):
        term = w_ref[k:k + 1, :] * ext_ref[SUBLANES - CONV_W + 1 + k:SUBLANES - CONV_W + 1 + k + n_rows, :]
        acc = term if acc is None else acc + term
    return acc


def _params(sem, flags=None):
    return pltpu.CompilerParams(dimension_semantics=sem, vmem_limit_bytes=VMEM_LIMIT, flags=flags)


ADA_TN = 1536


def _ada_kernel(c_ref, w_ref, b_ref, o_ref):
    c = c_ref[...]
    o_ref[0] = _mm(_silu(c), w_ref[0]) + b_ref[0]


def _ada_call(c_pad, w_ada, b_ada):
    n_l = w_ada.shape[0]
    n_out = w_ada.shape[2]
    return pl.pallas_call(
        _ada_kernel,
        grid=(n_l, n_out // ADA_TN),
        in_specs=[pl.BlockSpec((SUBLANES, D_MODEL), lambda l, j: (0, 0)),
                  pl.BlockSpec((1, D_MODEL, ADA_TN), lambda l, j: (l, 0, j)),
                  pl.BlockSpec((1, 1, ADA_TN), lambda l, j: (l, 0, j))],
        out_specs=pl.BlockSpec((1, SUBLANES, ADA_TN), lambda l, j: (l, 0, j)),
        out_shape=jax.ShapeDtypeStruct((n_l, SUBLANES, n_out), F32),
        compiler_params=_params(("arbitrary", "arbitrary")),
    )(c_pad, w_ada, b_ada.reshape(n_l, 1, n_out))


INP_TM = 512
INP_TN = 2048


def _inproj_kernel(x_ref, nw_ref, sc_ref, sh_ref, w_ref, o_ref, h_scr):
    @pl.when(pl.program_id(1) == 0)
    def _():
        h = _rms_rows(x_ref[...], nw_ref[...])
        h = h * (1.0 + sc_ref[0]) + sh_ref[0]
        h_scr[...] = h.astype(BF16)

    o_ref[...] = jnp.dot(h_scr[...], w_ref[...], preferred_element_type=F32)


def _inproj_call(x2, nw, sc, sh, w_perm, seq):
    n_tok = x2.shape[0]
    per_b = seq // INP_TM
    return pl.pallas_call(
        _inproj_kernel,
        grid=(n_tok // INP_TM, N_COLS // INP_TN),
        in_specs=[pl.BlockSpec((INP_TM, D_MODEL), lambda i, j: (i, 0)),
                  pl.BlockSpec((1, D_MODEL), lambda i, j: (0, 0)),
                  pl.BlockSpec((1, 1, D_MODEL), lambda i, j: (i // per_b, 0, 0)),
                  pl.BlockSpec((1, 1, D_MODEL), lambda i, j: (i // per_b, 0, 0)),
                  pl.BlockSpec((D_MODEL, INP_TN), lambda i, j: (0, j))],
        out_specs=pl.BlockSpec((INP_TM, INP_TN), lambda i, j: (i, j)),
        out_shape=jax.ShapeDtypeStruct((n_tok, N_COLS), F32),
        scratch_shapes=[pltpu.VMEM((INP_TM, D_MODEL), BF16)],
        compiler_params=_params(("arbitrary", "arbitrary")),
    )(x2, nw, sc, sh, w_perm)


SSD_L = 128


def _ssd_kernel(z_ref, xs_ref, bc_ref, sm_ref, cwx_ref, cbx_ref, cwbc_ref, cbbc_ref, dtb_ref,
                alog_ref, dfull_ref, nw_ref, o_ref, extx, extbc, state):
    L = SSD_L
    t = pl.program_id(1)

    @pl.when(t == 0)
    def _():
        extx[0:SUBLANES, :] = jnp.zeros((SUBLANES, BRANCH), F32)
        extbc[0:SUBLANES, :] = jnp.zeros((SUBLANES, 256), F32)
        state[...] = jnp.zeros_like(state)

    extx[SUBLANES:SUBLANES + L, :] = xs_ref[...]
    extbc[SUBLANES:SUBLANES + L, :] = bc_ref[...]
    xs = _silu(_causal_conv(extx, cwx_ref, L) + cbx_ref[...])
    bc = _silu(_causal_conv(extbc, cwbc_ref, L) + cbbc_ref[...])
    extx[0:SUBLANES, :] = extx[L:L + SUBLANES, :]
    extbc[0:SUBLANES, :] = extbc[L:L + SUBLANES, :]

    dt = _softplus(sm_ref[...] + dtb_ref[...])
    da = dt * (-jnp.exp(alog_ref[...]))
    ri = lax.broadcasted_iota(jnp.int32, (L, L), 0)
    ci = lax.broadcasted_iota(jnp.int32, (L, L), 1)
    tril = ri >= ci
    cs = _mm_hi(tril.astype(F32), da)
    er = lax.broadcasted_iota(jnp.int32, (SMALL_W, BRANCH), 0)
    ec = lax.broadcasted_iota(jnp.int32, (SMALL_W, BRANCH), 1)
    expand = (er == (ec >> 6)).astype(F32)
    cs_full = _mm_hi(cs, expand)
    dt_full = _mm_hi(dt, expand)
    ecs_full = jnp.exp(cs_full)
    cs_last = cs_full[L - 1:L, :]
    w_full = jnp.exp(cs_last - cs_full)
    xdt = xs * dt_full
    xdtw = xdt * w_full
    cs_t = cs.T

    b128 = bc[:, 0:LANES]
    c128 = bc[:, LANES:2 * LANES]
    lane = lax.broadcasted_iota(jnp.int32, (1, LANES), 1)
    cg = [jnp.where(lane < SSM_STATE, c128, 0.0), jnp.where(lane >= SSM_STATE, c128, 0.0)]
    cb = [_mm_nt(cg[g], b128) for g in range(SSM_GROUPS)]

    y_pairs = []
    for p in range(SSM_HEADS // 2):
        xp = xdt[:, p * LANES:(p + 1) * LANES]
        yp = None
        for hh in range(2):
            h = 2 * p + hh
            g = h // (SSM_HEADS // SSM_GROUPS)
            col = cs[:, h:h + 1]
            row = cs_t[h:h + 1, :]
            seg = jnp.exp(jnp.where(tril, col - row, NEG_INF))
            att = cb[g] * seg
            hm = (lane < SSM_HEAD_DIM) if hh == 0 else (lane >= SSM_HEAD_DIM)
            term = _mm(att, jnp.where(hm, xp, 0.0))
            yp = term if yp is None else yp + term
        y_pairs.append(yp)
    y_diag = jnp.concatenate(y_pairs, axis=1)

    y_offs = []
    for g in range(SSM_GROUPS):
        sl = slice(g * 256, (g + 1) * 256)
        s_in = state[g]
        y_offs.append(_mm(cg[g], s_in) * ecs_full[:, sl])
        new = _mm_tn(b128, xdtw[:, sl])
        state[g] = s_in * ecs_full[L - 1:L, sl] + new
    y = y_diag + jnp.concatenate(y_offs, axis=1) + xs * dfull_ref[...]
    y = y * _silu(z_ref[...])
    o_ref[...] = _rms_rows(y, nw_ref[...]).astype(BF16)


def _ssd_call(proj, p, bsz, seq):
    nt = seq // SSD_L
    row = lambda blk: pl.BlockSpec((SSD_L, 512), lambda b, t: (b * nt + t, blk))
    full = lambda shape: pl.BlockSpec(shape, lambda b, t: (0,) * len(shape))
    return pl.pallas_call(
        _ssd_kernel,
        grid=(bsz, nt),
        in_specs=[row(U512["ssm_z"]), row(U512["ssm_x"]),
                  pl.BlockSpec((SSD_L, 256), lambda b, t: (b * nt + t, U256_BC)),
                  pl.BlockSpec((SSD_L, 256), lambda b, t: (b * nt + t, U256_SMALL)),
                  full((CONV_W, 512)), full((1, 512)), full((CONV_W, 256)), full((1, 256)),
                  full((1, SMALL_W)), full((1, SMALL_W)), full((1, 512)), full((1, 512))],
        out_specs=pl.BlockSpec((SSD_L, 512), lambda b, t: (b * nt + t, 0)),
        out_shape=jax.ShapeDtypeStruct((bsz * seq, BRANCH), BF16),
        scratch_shapes=[pltpu.VMEM((SUBLANES + SSD_L, 512), F32),
                        pltpu.VMEM((SUBLANES + SSD_L, 256), F32),
                        pltpu.VMEM((SSM_GROUPS, LANES, 256), F32)],
        compiler_params=_params(("arbitrary", "arbitrary")),
    )(proj, proj, proj, proj, p["cwx"], p["cbx"], p["cwbc"], p["cbbc"], p["dtb"], p["alog"],
      p["dfull"], p["nw"])


GDN_L = 64


def _gdn_kernel(q_ref, k_ref, v_ref, z_ref, sm_ref, cwq_ref, cwk_ref, cwv_ref, dtb_ref, alog_ref,
                nw_ref, o_ref, ext, state, *, bsz):
    L = GDN_L
    H = GDN_HEADS
    t = pl.program_id(0)

    @pl.when(t == 0)
    def _():
        ext[:, :, 0:SUBLANES, :] = jnp.zeros((3, bsz, SUBLANES, BRANCH), F32)
        state[...] = jnp.zeros_like(state)

    ri = lax.broadcasted_iota(jnp.int32, (L, L), 0)
    ci = lax.broadcasted_iota(jnp.int32, (L, L), 1)
    incl = ri >= ci
    strict = ri > ci
    eye = (ri == ci).astype(F32)
    bx = (ri >> 3) ^ (ci >> 3)
    blk = (bx > 0).astype(jnp.int32) + (bx > 1).astype(jnp.int32) + (bx > 3).astype(jnp.int32)
    tri = incl.astype(F32)
    nw = nw_ref[...]
    chains = [(b, h) for b in range(bsz) for h in range(H)]

    qkv, beta_all, gc_all, gc_t = [], [], [], []
    for b in range(bsz):
        outs = []
        for i, (r, w) in enumerate(((q_ref, cwq_ref), (k_ref, cwk_ref), (v_ref, cwv_ref))):
            e = ext.at[i, b]
            e[SUBLANES:SUBLANES + L, :] = r[b]
            outs.append(_silu(_causal_conv(e, w, L)))
            e[0:SUBLANES, :] = e[L:L + SUBLANES, :]
        qkv.append(outs)
        sm = sm_ref[b]
        beta_all.append(_sigmoid(sm))
        g_all = -jnp.exp(alog_ref[...]) * _softplus(sm + dtb_ref[...])
        gc = _mm_hi(tri, g_all)
        gc_all.append(gc)
        gc_t.append(gc.T)

    qs, ks, kbs, rhss, decays, egcs, glasts, gcbs = [], [], [], [], [], [], [], []
    for b, h in chains:
        sl = slice(h * GDN_DIM, (h + 1) * GDN_DIM)
        qh, kh, vh = (a[:, sl] for a in qkv[b])
        qh = qh * lax.rsqrt(jnp.sum(qh * qh, axis=-1, keepdims=True) + EPS) * (GDN_DIM ** -0.5)
        kh = kh * lax.rsqrt(jnp.sum(kh * kh, axis=-1, keepdims=True) + EPS)
        beta = beta_all[b][:, SM_BETA + h:SM_BETA + h + 1]
        gcol = gc_all[b][:, SM_DECAY + h:SM_DECAY + h + 1]
        grow = gc_t[b][SM_DECAY + h:SM_DECAY + h + 1, :]
        decays.append(jnp.exp(jnp.where(incl, gcol - grow, NEG_INF)))
        gcb = jnp.broadcast_to(gcol, (L, GDN_DIM))
        egc = jnp.exp(gcb)
        kb = kh * beta
        qs.append(qh); ks.append(kh); kbs.append(kb); gcbs.append(gcb); egcs.append(egc)
        glasts.append(gcb[L - 1:L, :])
        rhss.append(jnp.concatenate([vh * beta, kb * egc], axis=1))

    n = len(chains)
    rng = range(n)
    kk = [_mm_nt(kbs[c], ks[c]) for c in rng]
    qk = [_mm_nt(qs[c], ks[c]) for c in rng]
    ms = [jnp.where(strict, kk[c] * decays[c], 0.0) for c in rng]
    mds = [jnp.where(blk == 0, ms[c], 0.0) for c in rng]
    p2 = [_mm(mds[c], mds[c]) for c in rng]
    base = [eye - mds[c] for c in rng]
    bp = [_mm(base[c], p2[c]) for c in rng]
    p4 = [_mm(p2[c], p2[c]) for c in rng]
    base = [base[c] + bp[c] for c in rng]
    bq = [_mm(base[c], p4[c]) for c in rng]
    inv = [base[c] + bq[c] for c in rng]
    for lvl in range(1, int(math.log2(L // SUBLANES)) + 1):
        oi = [_mm(jnp.where(blk == lvl, ms[c], 0.0), inv[c]) for c in rng]
        ioi = [_mm(inv[c], oi[c]) for c in rng]
        inv = [inv[c] - ioi[c] for c in rng]
    sol = [_mm(inv[c], rhss[c]) for c in rng]
    s_in = [state[c] for c in rng]
    ws = [_mm(sol[c][:, GDN_DIM:], s_in[c]) for c in rng]
    qgs = [_mm(qs[c] * egcs[c], s_in[c]) for c in rng]
    v_new = [sol[c][:, :GDN_DIM] - ws[c] for c in rng]
    av = [_mm(qk[c] * decays[c], v_new[c]) for c in rng]
    kv = [_mm_tn(ks[c] * jnp.exp(glasts[c] - gcbs[c]), v_new[c]) for c in rng]
    for c in rng:
        state[c] = s_in[c] * jnp.exp(glasts[c]) + kv[c]
    for b in range(bsz):
        z = z_ref[b]
        outs = []
        for h in range(H):
            c = b * H + h
            sl = slice(h * GDN_DIM, (h + 1) * GDN_DIM)
            outs.append(_rms_rows(qgs[c] + av[c], nw[:, sl]) * _silu(z[:, sl]))
        o_ref[b] = jnp.concatenate(outs, axis=1).astype(BF16)


def _gdn_call(proj3, p, bsz, seq):
    nt = seq // GDN_L
    row = lambda blk, w=512: pl.BlockSpec((bsz, GDN_L, w), lambda t: (0, t, blk))
    full = lambda shape: pl.BlockSpec(shape, lambda t: (0,) * len(shape))
    return pl.pallas_call(
        functools.partial(_gdn_kernel, bsz=bsz),
        grid=(nt,),
        in_specs=[row(U512["gdn_q"]), row(U512["gdn_k"]), row(U512["gdn_v"]), row(U512["gdn_z"]),
                  row(U256_SMALL, 256),
                  full((CONV_W, 512)), full((CONV_W, 512)), full((CONV_W, 512)),
                  full((1, SMALL_W)), full((1, SMALL_W)), full((1, 512))],
        out_specs=pl.BlockSpec((bsz, GDN_L, 512), lambda t: (0, t, 0)),
        out_shape=jax.ShapeDtypeStruct((bsz, seq, BRANCH), BF16),
        scratch_shapes=[pltpu.VMEM((3, bsz, SUBLANES + GDN_L, 512), F32),
                        pltpu.VMEM((bsz * GDN_HEADS, GDN_DIM, GDN_DIM), F32)],
        compiler_params=_params(("arbitrary",)),
    )(proj3, proj3, proj3, proj3, proj3, p["cwq"], p["cwk"], p["cwv"], p["dtb"], p["alog"], p["nw"])


GLA_T = 128
GLA_QK = GLA_HEADS * GLA_DK


def _gla_kernel(qk_ref, v_ref, r_ref, sm_ref, wg_ref, bg_ref, nw_ref, o_ref, state):
    T = GLA_T
    C = GLA_CHUNK
    t = pl.program_id(1)

    @pl.when(t == 0)
    def _():
        state[...] = jnp.zeros_like(state)

    qk = qk_ref[...]
    q = qk[:, :GLA_QK] * (GLA_DK ** -0.5)
    k = qk[:, GLA_QK:]
    v = v_ref[...]
    pre = _mm(sm_ref[...], wg_ref[...]) + bg_ref[...]
    log_a = (jnp.minimum(pre, 0.0) - jnp.log1p(jnp.exp(-jnp.abs(pre)))) / GLA_TAU
    ri = lax.broadcasted_iota(jnp.int32, (T, T), 0)
    ci = lax.broadcasted_iota(jnp.int32, (T, T), 1)
    blocktri = ((ri >> 4) == (ci >> 4)) & (ri >= ci)
    G = _mm_hi(blocktri.astype(F32), log_a)

    rr = lax.broadcasted_iota(jnp.int32, (GLA_QK, BRANCH), 0)
    rc = lax.broadcasted_iota(jnp.int32, (GLA_QK, BRANCH), 1)
    red = ((rr >> 6) == (rc >> 7)).astype(BF16)
    rmod = lax.broadcasted_iota(jnp.int32, (T, GLA_QK), 0) & (C - 1)
    o = jnp.zeros((T, BRANCH), F32)
    nc = T // C

    def chunk_row(x, jl):
        w = x.shape[1]
        x3 = x.reshape(nc, C, w)
        return jnp.broadcast_to(x3[:, jl:jl + 1, :], (nc, C, w)).reshape(T, w)

    for jl in range(C):
        ks = chunk_row(k, jl)
        gs = chunk_row(G, jl)
        vs = chunk_row(v, jl)
        qd = jnp.where(rmod >= jl, q, 0.0)
        prod = qd * ks * jnp.exp(jnp.minimum(G - gs, 0.0))
        hi = prod.astype(BF16)
        lo = (prod - hi.astype(F32)).astype(BF16)
        a_full = (jnp.dot(hi, red, preferred_element_type=F32)
                  + jnp.dot(lo, red, preferred_element_type=F32))
        o = o + a_full * vs

    sr = lax.broadcasted_iota(jnp.int32, (BRANCH, GLA_QK), 0)
    sc = lax.broadcasted_iota(jnp.int32, (BRANCH, GLA_QK), 1)
    blockdiag = (sr >> 7) == (sc >> 6)
    st = state[...]
    inter = []
    for c in range(T // C):
        rows = slice(c * C, (c + 1) * C)
        gc = G[rows]
        glast = gc[C - 1:C, :]
        qg = q[rows] * jnp.exp(gc)
        kd = k[rows] * jnp.exp(glast - gc)
        inter.append(_mm_nt(qg, st))
        st = jnp.where(blockdiag, st * jnp.exp(glast) + _mm_tn(v[rows], kd), 0.0)
    state[...] = st
    o = o + jnp.concatenate(inter, axis=0)
    r = r_ref[...]
    nw = nw_ref[...]
    outs = []
    for h in range(GLA_HEADS):
        sl = slice(h * GLA_DV, (h + 1) * GLA_DV)
        outs.append(_rms_rows(o[:, sl], nw[:, sl]) * _silu(r[:, sl]))
    o_ref[...] = jnp.concatenate(outs, axis=1).astype(BF16)


def _gla_call(proj, p, bsz, seq):
    nt = seq // GLA_T
    row = lambda blk: pl.BlockSpec((GLA_T, 512), lambda b, t: (b * nt + t, blk))
    full = lambda shape: pl.BlockSpec(shape, lambda b, t: (0,) * len(shape))
    return pl.pallas_call(
        _gla_kernel,
        grid=(bsz, nt),
        in_specs=[row(U512["gla_qk"]), row(U512["gla_v"]), row(U512["gla_r"]),
                  pl.BlockSpec((GLA_T, 256), lambda b, t: (b * nt + t, U256_SMALL)),
                  full((SMALL_W, GLA_QK)), full((1, GLA_QK)), full((1, 512))],
        out_specs=pl.BlockSpec((GLA_T, 512), lambda b, t: (b * nt + t, 0)),
        out_shape=jax.ShapeDtypeStruct((bsz * seq, BRANCH), BF16),
        scratch_shapes=[pltpu.VMEM((BRANCH, GLA_QK), F32)],
        compiler_params=_params(("arbitrary", "arbitrary")),
    )(proj, proj, proj, proj, p["wg"], p["bg"], p["nw"])


LRU_T = 256
LRU_PAD = LRU_T // 2


def _lru_kernel(x_ref, gate_ref, cw_ref, cb_ref, wa_ref, ba_ref, wx_ref, bx_ref, lam_ref, o_ref,
                ext, abuf, hbuf, carry):
    T = LRU_T
    P = LRU_PAD
    t = pl.program_id(1)

    @pl.when(t == 0)
    def _():
        ext[0:SUBLANES, :] = jnp.zeros((SUBLANES, BRANCH), F32)
        abuf[0:P, :] = jnp.ones((P, BRANCH), F32)
        hbuf[0:P, :] = jnp.zeros((P, BRANCH), F32)
        carry[...] = jnp.zeros_like(carry)

    ext[SUBLANES:SUBLANES + T, :] = x_ref[...]
    xc = _causal_conv(ext, cw_ref, T) + cb_ref[...]
    ext[0:SUBLANES, :] = ext[T:T + SUBLANES, :]
    gate_r = _sigmoid(_mm(xc, wa_ref[...]) + ba_ref[...])
    gate_i = _sigmoid(_mm(xc, wx_ref[...]) + bx_ref[...])
    log_a = -LRU_C * gate_r * _softplus(-lam_ref[...])
    abuf[P:P + T, :] = jnp.exp(log_a)
    th = jnp.tanh(log_a)
    hbuf[P:P + T, :] = jnp.sqrt(-2.0 * th / (1.0 - th)) * (gate_i * xc)
    s = 1
    while s < T:
        a_cur = abuf[P:P + T, :]
        h_cur = hbuf[P:P + T, :]
        a_sh = abuf[P - s:P - s + T, :]
        h_sh = hbuf[P - s:P - s + T, :]
        hbuf[P:P + T, :] = h_cur + a_cur * h_sh
        abuf[P:P + T, :] = a_cur * a_sh
        s *= 2
    h = hbuf[P:P + T, :] + abuf[P:P + T, :] * carry[0:1, :]
    carry[0:1, :] = h[T - 1:T, :]
    o_ref[...] = (h * _gelu(gate_ref[...])).astype(BF16)


def _lru_call(proj, p, bsz, seq):
    nt = seq // LRU_T
    row = lambda blk: pl.BlockSpec((LRU_T, 512), lambda b, t: (b * nt + t, blk))
    full = lambda shape: pl.BlockSpec(shape, lambda b, t: (0,) * len(shape))
    return pl.pallas_call(
        _lru_kernel,
        grid=(bsz, nt),
        in_specs=[row(U512["lru_x"]), row(U512["lru_gate"]),
                  full((CONV_W, 512)), full((1, 512)), full((512, 512)), full((1, 512)),
                  full((512, 512)), full((1, 512)), full((1, 512))],
        out_specs=pl.BlockSpec((LRU_T, 512), lambda b, t: (b * nt + t, 0)),
        out_shape=jax.ShapeDtypeStruct((bsz * seq, BRANCH), BF16),
        scratch_shapes=[pltpu.VMEM((SUBLANES + LRU_T, 512), F32),
                        pltpu.VMEM((LRU_PAD + LRU_T, 512), F32),
                        pltpu.VMEM((LRU_PAD + LRU_T, 512), F32),
                        pltpu.VMEM((SUBLANES, 512), F32)],
        compiler_params=_params(("arbitrary", "arbitrary")),
    )(proj, proj, p["cw"], p["cb"], p["wa"], p["ba"], p["wx"], p["bx"], p["lam"])


MRG_TM = 256


def _merge_kernel(x_ref, g_ref, lg_ref, y0_ref, y1_ref, y2_ref, y3_ref, wb_ref, wo_ref, o_ref):
    merged = None
    for i, y_ref in enumerate((y0_ref, y1_ref, y2_ref, y3_ref)):
        br = jnp.dot(y_ref[...], wb_ref[i], preferred_element_type=F32)
        term = _sigmoid(lg_ref[:, i * D_MODEL:(i + 1) * D_MODEL]) * br
        merged = term if merged is None else merged + term
    out = jnp.dot(merged.astype(BF16), wo_ref[...], preferred_element_type=F32)
    o_ref[...] = x_ref[...] + g_ref[0] * out


def _merge_call(x2, g1, proj, ys, wb, wo, seq):
    n_tok = x2.shape[0]
    per_b = seq // MRG_TM
    yspec = pl.BlockSpec((MRG_TM, BRANCH), lambda i: (i, 0))
    return pl.pallas_call(
        _merge_kernel,
        grid=(n_tok // MRG_TM,),
        in_specs=[pl.BlockSpec((MRG_TM, D_MODEL), lambda i: (i, 0)),
                  pl.BlockSpec((1, 1, D_MODEL), lambda i: (i // per_b, 0, 0)),
                  pl.BlockSpec((MRG_TM, N_BRANCH * D_MODEL), lambda i: (i, 0)),
                  yspec, yspec, yspec, yspec,
                  pl.BlockSpec((N_BRANCH, BRANCH, D_MODEL), lambda i: (0, 0, 0)),
                  pl.BlockSpec((D_MODEL, D_MODEL), lambda i: (0, 0))],
        out_specs=pl.BlockSpec((MRG_TM, D_MODEL), lambda i: (i, 0)),
        out_shape=jax.ShapeDtypeStruct((n_tok, D_MODEL), F32),
        compiler_params=_params(("arbitrary",)),
    )(x2, g1, proj, *ys, wb, wo)


PS_TB = 512
N_SCORE_ROWS = 2 * PEER_HEADS * PEER_KEYS


def _peer_score_kernel(x_ref, nw_ref, sc_ref, sh_ref, wqt_ref, keys_ref, h2t_ref, st_ref):
    h = _rms_rows(x_ref[...], nw_ref[...])
    h = h * (1.0 + sc_ref[0]) + sh_ref[0]
    ht = h.T.astype(BF16)
    h2t_ref[...] = ht
    qt = jnp.dot(wqt_ref[...], ht, preferred_element_type=F32).astype(BF16)
    for g in range(2 * PEER_HEADS):
        rows = slice(g * PEER_KEYS, (g + 1) * PEER_KEYS)
        st_ref[rows, :] = jnp.dot(keys_ref[g], qt[rows, :], preferred_element_type=F32)


def _peer_score_call(x2, nw, sc, sh, wqt, keys, seq):
    n_tok = x2.shape[0]
    per_b = seq // PS_TB
    return pl.pallas_call(
        _peer_score_kernel,
        grid=(n_tok // PS_TB,),
        in_specs=[pl.BlockSpec((PS_TB, D_MODEL), lambda i: (i, 0)),
                  pl.BlockSpec((1, D_MODEL), lambda i: (0, 0)),
                  pl.BlockSpec((1, 1, D_MODEL), lambda i: (i // per_b, 0, 0)),
                  pl.BlockSpec((1, 1, D_MODEL), lambda i: (i // per_b, 0, 0)),
                  pl.BlockSpec((N_SCORE_ROWS, D_MODEL), lambda i: (0, 0)),
                  pl.BlockSpec((2 * PEER_HEADS, PEER_KEYS, PEER_HALF), lambda i: (0, 0, 0))],
        out_specs=[pl.BlockSpec((D_MODEL, PS_TB), lambda i: (0, i)),
                   pl.BlockSpec((N_SCORE_ROWS, PS_TB), lambda i: (0, i))],
        out_shape=[jax.ShapeDtypeStruct((D_MODEL, n_tok), BF16),
                   jax.ShapeDtypeStruct((N_SCORE_ROWS, n_tok), F32)],
        compiler_params=_params(("arbitrary",)),
    )(x2, nw, sc, sh, wqt, keys)


PT_TL = 256
NOT_RANKED = 255.0
_CAND = [(i, j) for i in range(PEER_TOPK) for j in range(PEER_TOPK) if (i + 1) * (j + 1) <= PEER_TOPK]
N_CAND_ROWS = -(-len(_CAND) // SUBLANES) * SUBLANES


def _pop_max(x, iota):
    m = jnp.max(x, axis=0, keepdims=True)
    first = jnp.min(jnp.where(x == m, iota, float(x.shape[0])), axis=0, keepdims=True)
    return m, first, jnp.where(iota == first, NEG_INF, x)


def _peer_gate_kernel(st_ref, ce_ref, r2_ref, e2_ref, cand):
    TL = PT_TL
    iota_k = lax.broadcasted_iota(jnp.int32, (PEER_KEYS, TL), 0).astype(F32)
    iota_c = lax.broadcasted_iota(jnp.int32, (N_CAND_ROWS, TL), 0).astype(F32)
    cand[...] = jnp.full((N_CAND_ROWS, TL), NEG_INF, F32)
    for h in range(PEER_HEADS):
        rows1 = slice(h * PEER_KEYS, (h + 1) * PEER_KEYS)
        rows2 = slice((PEER_HEADS + h) * PEER_KEYS, (PEER_HEADS + h + 1) * PEER_KEYS)
        x = st_ref[rows1, :]
        t1, first1 = [], []
        for _ in range(PEER_TOPK):
            m, f, x = _pop_max(x, iota_k)
            t1.append(m)
            first1.append(f)
        x = st_ref[rows2, :]
        t2 = []
        rank2 = jnp.full((PEER_KEYS, TL), NOT_RANKED, F32)
        for r in range(PEER_TOPK):
            m, f, x = _pop_max(x, iota_k)
            t2.append(m)
            rank2 = jnp.where(iota_k == f, float(r), rank2)
        for n, (i, j) in enumerate(_CAND):
            cand[n:n + 1, :] = t1[i] + t2[j]
        c = cand[...]
        x = c
        tau = None
        for _ in range(PEER_TOPK):
            tau, _, x = _pop_max(x, iota_c)
        m1, m2 = t1[0], t2[0]
        zsum = jnp.sum(jnp.where(c >= tau, jnp.exp(c - (m1 + m2)), 0.0), axis=0, keepdims=True)
        count1 = jnp.zeros((PEER_KEYS, TL), F32)
        for i in range(PEER_TOPK):
            cnt = None
            for j in range(PEER_TOPK):
                if (i + 1) * (j + 1) <= PEER_TOPK:
                    hit = jnp.where(t1[i] + t2[j] >= tau, 1.0, 0.0)
                    cnt = hit if cnt is None else cnt + hit
            count1 = jnp.where(iota_k == first1[i], cnt, count1)
        ce_ref[rows1, :] = count1
        ce_ref[rows2, :] = jnp.exp(st_ref[rows1, :] - m1) * (0.5 / zsum)
        r2_ref[rows1, :] = rank2.astype(BF16)
        e2_ref[rows1, :] = jnp.exp(st_ref[rows2, :] - m2).astype(BF16)


def _peer_gate_call(scores_t):
    n_tok = scores_t.shape[1]
    half = PEER_HEADS * PEER_KEYS
    return pl.pallas_call(
        _peer_gate_kernel,
        grid=(n_tok // PT_TL,),
        in_specs=[pl.BlockSpec((N_SCORE_ROWS, PT_TL), lambda i: (0, i))],
        out_specs=[pl.BlockSpec((N_SCORE_ROWS, PT_TL), lambda i: (0, i)),
                   pl.BlockSpec((half, PT_TL), lambda i: (0, i)),
                   pl.BlockSpec((half, PT_TL), lambda i: (0, i))],
        out_shape=[jax.ShapeDtypeStruct((N_SCORE_ROWS, n_tok), F32),
                   jax.ShapeDtypeStruct((half, n_tok), BF16),
                   jax.ShapeDtypeStruct((half, n_tok), BF16)],
        scratch_shapes=[pltpu.VMEM((N_CAND_ROWS, PT_TL), F32)],
        compiler_params=_params(("arbitrary",)),
    )(scores_t)


PE_TB = 512
PE_EB = 512
HALF_ROWS = PEER_HEADS * PEER_KEYS


def _bcast_rows_bf16(row):
    r16 = jnp.broadcast_to(row, (2 * SUBLANES, LANES)).astype(BF16)
    return jnp.concatenate([r16] * (PEER_KEYS // (2 * SUBLANES)), axis=0)


def _peer_expert_kernel(h2t_ref, u_ref, vt_ref, ce_ref, r2_ref, e2_ref, x_ref, g_ref, fw_ref,
                        o_ref, acc, zt, *, final_norm):
    j = pl.program_id(1)
    H = PEER_HEADS
    cur = j % 2
    prv = 1 - cur

    @pl.when(j == 0)
    def _():
        acc[...] = jnp.zeros_like(acc)
        zt[1] = jnp.zeros((PE_EB, PE_TB), BF16)

    c0 = math.sqrt(2.0 / math.pi)
    zero = jnp.zeros((PEER_KEYS, LANES), BF16)
    R16 = 2 * SUBLANES
    deps = []
    for q in range(PE_EB // PEER_KEYS):
        ceq = ce_ref[q]
        rs = slice(q * PEER_KEYS, (q + 1) * PEER_KEYS)
        for lc in range(PE_TB // LANES):
            ls = slice(lc * LANES, (lc + 1) * LANES)
            w = None
            for h in range(H):
                rows = slice(h * PEER_KEYS, (h + 1) * PEER_KEYS)
                cnt = _bcast_rows_bf16(ceq[h:h + 1, ls])
                g1 = _bcast_rows_bf16(ceq[H + h:H + h + 1, ls])
                sel_g1 = jnp.minimum(jnp.maximum(cnt - r2_ref[rows, ls], zero), g1)
                term = sel_g1 * e2_ref[rows, ls]
                w = term if w is None else w + term
                deps.append(w[h * R16:(h + 1) * R16, :] * zero[0:R16, :])
            zt[cur, rs, ls] = w

    def released(ref, n_rows, n_cols, units, rows_major):
        n_rg = n_rows // R16
        n_kt = n_cols // (2 * LANES)
        n_slabs = n_rg * n_kt
        slabs = []
        for r in range(n_rg):
            pieces = []
            for c in range(n_cols // LANES):
                piece = ref[r * R16:(r + 1) * R16, c * LANES:(c + 1) * LANES]
                if c % 2 == 0:
                    slab = r * n_kt + c // 2 if rows_major else (c // 2) * n_rg + r
                    piece = piece + units[slab * len(units) // n_slabs]
                pieces.append(piece)
            slabs.append(jnp.concatenate(pieces, axis=1))
        return jnp.concatenate(slabs, axis=0)

    n_a = (len(deps) * 7) // 16
    n_c = len(deps) // 2
    st = jnp.dot(released(u_ref, PE_EB, D_MODEL, deps[:n_a], False), h2t_ref[...],
                 preferred_element_type=F32)
    acc[...] += jnp.dot(released(vt_ref, D_MODEL, PE_EB, deps[n_a:n_a + n_c], True), zt[prv],
                        preferred_element_type=F32)
    n_lc = PE_TB // LANES
    for q in range(PE_EB // PEER_KEYS):
        rs = slice(q * PEER_KEYS, (q + 1) * PEER_KEYS)
        for lc in range(n_lc):
            ls = slice(lc * LANES, (lc + 1) * LANES)
            x = st[rs, ls]
            th = jnp.tanh(x * (c0 + (c0 * 0.044715) * (x * x)))
            zt[cur, rs, ls] = (x + x * th).astype(BF16) * zt[cur, rs, ls]

    @pl.when(j == pl.num_programs(1) - 1)
    def _():
        xn = x_ref[...] + g_ref[0] * acc[...].T
        if final_norm:
            xn = _rms_rows(xn, fw_ref[...])
        o_ref[...] = xn


def _peer_expert_call(h2, u_bf, vt_bf, ce, r2, e2, x2, g2, fw, seq, final_norm):
    n_tok = x2.shape[0]
    per_b = seq // PE_TB
    kern = functools.partial(_peer_expert_kernel, final_norm=final_norm)
    ce_k = ce.reshape(2, PEER_HEADS, PEER_KEYS, n_tok).transpose(2, 0, 1, 3)
    ce_k = ce_k.reshape(PEER_KEYS, 2 * PEER_HEADS, n_tok)
    n_blk = PEER_EXPERTS // PE_EB
    blk = lambda j, lag: jnp.clip(j - lag, 0, n_blk - 1)
    return pl.pallas_call(
        kern,
        grid=(n_tok // PE_TB, n_blk + 1),
        in_specs=[pl.BlockSpec((D_MODEL, PE_TB), lambda i, j: (0, i)),
                  pl.BlockSpec((PE_EB, D_MODEL), lambda i, j: (blk(j, 0), 0)),
                  pl.BlockSpec((D_MODEL, PE_EB), lambda i, j: (0, blk(j, 1))),
                  pl.BlockSpec((PE_EB // PEER_KEYS, 2 * PEER_HEADS, PE_TB), lambda i, j: (blk(j, 0), 0, i)),
                  pl.BlockSpec((HALF_ROWS, PE_TB), lambda i, j: (0, i)),
                  pl.BlockSpec((HALF_ROWS, PE_TB), lambda i, j: (0, i)),
                  pl.BlockSpec((PE_TB, D_MODEL), lambda i, j: (i, 0)),
                  pl.BlockSpec((1, 1, D_MODEL), lambda i, j: (i // per_b, 0, 0)),
                  pl.BlockSpec((1, D_MODEL), lambda i, j: (0, 0))],
        out_specs=pl.BlockSpec((PE_TB, D_MODEL), lambda i, j: (i, 0)),
        out_shape=jax.ShapeDtypeStruct((n_tok, D_MODEL), F32),
        scratch_shapes=[pltpu.VMEM((D_MODEL, PE_TB), F32),
                        pltpu.VMEM((2, PE_EB, PE_TB), BF16)],
        compiler_params=_params(("arbitrary", "arbitrary")),
    )(h2, u_bf, vt_bf, ce_k, r2, e2, x2, g2, fw)


def _pad_lanes(vec, start, width=SMALL_W):
    out = jnp.zeros((1, width), F32)
    return lax.dynamic_update_slice(out, vec.reshape(1, -1).astype(F32), (0, start))


def _block_diag(w):
    n, d, e = w.shape
    eye = jnp.eye(n, dtype=w.dtype)
    return (eye[:, None, :, None] * w[:, :, None, :]).reshape(n * d, n * e)


def kernel(x, c, w_ada, b_ada, norm_mix_w, norm_ffn_w, w_in, ssm_conv_w, ssm_conv_b, ssm_dt_bias,
           ssm_a_log, ssm_d, ssm_norm_w, gdn_conv_w, gdn_a_log, gdn_dt_bias, gdn_norm_w, gla_w_gate,
           gla_b_gate, gla_norm_w, lru_conv_w, lru_conv_b, lru_w_a, lru_b_a, lru_w_x, lru_b_x,
           lru_lambda, w_branch, w_out, peer_w_q, peer_sub_keys, peer_u, peer_v, final_norm_w):
    bsz, seq, d = x.shape
    n_layers = w_in.shape[0]
    n_tok = bsz * seq
    x2 = x.reshape(n_tok, d)

    c_pad = jnp.zeros((SUBLANES, d), F32).at[:bsz].set(c)
    mod = _ada_call(c_pad, w_ada, b_ada)

    row1 = lambda v: v.reshape(1, -1).astype(F32)

    for l in range(n_layers):
        m6 = mod[l, :bsz].reshape(bsz, 6, 1, d)
        sh1, sc1, g1, sh2, sc2, g2 = (m6[:, i] for i in range(6))
        w_perm = _permute_w_in(w_in[l])
        proj = _inproj_call(x2, row1(norm_mix_w[l]), sc1, sh1, w_perm, seq)

        ssd_p = dict(cwx=ssm_conv_w[l][:, :512], cbx=row1(ssm_conv_b[l][:512]),
                     cwbc=ssm_conv_w[l][:, 512:], cbbc=row1(ssm_conv_b[l][512:]),
                     dtb=_pad_lanes(ssm_dt_bias[l], SM_DT), alog=_pad_lanes(ssm_a_log[l], SM_DT),
                     dfull=row1(jnp.repeat(ssm_d[l], SSM_HEAD_DIM)), nw=row1(ssm_norm_w[l]))
        y_ssd = _ssd_call(proj, ssd_p, bsz, seq)

        gdn_p = dict(cwq=gdn_conv_w[l][:, :512], cwk=gdn_conv_w[l][:, 512:1024],
                     cwv=gdn_conv_w[l][:, 1024:], dtb=_pad_lanes(gdn_dt_bias[l], SM_DECAY),
                     alog=_pad_lanes(gdn_a_log[l], SM_DECAY),
                     nw=row1(jnp.tile(gdn_norm_w[l], GDN_HEADS)))
        y_gdn = _gdn_call(proj.reshape(bsz, seq, N_COLS), gdn_p, bsz, seq).reshape(n_tok, BRANCH)

        wg = jnp.zeros((SMALL_W, GLA_QK), F32).at[SM_LOW:SM_LOW + GLA_RANK].set(gla_w_gate[l])
        gla_p = dict(wg=wg.astype(BF16), bg=row1(gla_b_gate[l]),
                     nw=row1(jnp.tile(gla_norm_w[l], GLA_HEADS)))
        y_gla = _gla_call(proj, gla_p, bsz, seq)

        lru_p = dict(cw=lru_conv_w[l], cb=row1(lru_conv_b[l]),
                     wa=_block_diag(lru_w_a[l]).astype(BF16), ba=row1(lru_b_a[l]),
                     wx=_block_diag(lru_w_x[l]).astype(BF16), bx=row1(lru_b_x[l]),
                     lam=row1(lru_lambda[l]))
        y_lru = _lru_call(proj, lru_p, bsz, seq)

        x2 = _merge_call(x2, g1, proj, (y_ssd, y_gdn, y_gla, y_lru),
                         w_branch[l].astype(BF16), w_out[l].astype(BF16), seq)

        wqt = peer_w_q[l].reshape(d, PEER_HEADS, 2, PEER_HALF).transpose(2, 1, 3, 0)
        wqt = wqt.reshape(N_SCORE_ROWS, d).astype(BF16)
        keys = peer_sub_keys[l].transpose(1, 0, 2, 3).reshape(2 * PEER_HEADS, PEER_KEYS, PEER_HALF)
        h2, scores_t = _peer_score_call(x2, row1(norm_ffn_w[l]), sc2, sh2, wqt, keys.astype(BF16), seq)
        ce, r2, e2 = _peer_gate_call(scores_t)
        x2 = _peer_expert_call(h2, peer_u[l].astype(BF16), peer_v[l].T.astype(BF16), ce, r2, e2,
                               x2, g2, row1(final_norm_w), seq, final_norm=(l == n_layers - 1))
    return x2.reshape(bsz, seq, d)
```

```python
import functools
import math

import jax
import jax.numpy as jnp
from jax import lax
from jax.experimental import pallas as pl
from jax.experimental.pallas import tpu as pltpu

F32 = jnp.float32
BF16 = jnp.bfloat16
HIGHEST = lax.Precision.HIGHEST
NEG_INF = float("-inf")

D_MODEL = 1024
N_LAYERS = 2
EPS = 1e-6
CONV_W = 4
BRANCH = 512
N_BRANCH = 4
SSM_HEADS = 8
SSM_HEAD_DIM = 64
SSM_GROUPS = 2
SSM_STATE = 64
GDN_HEADS = 4
GDN_DIM = 128
GLA_HEADS = 4
GLA_DK = 64
GLA_DV = 128
GLA_RANK = 16
GLA_TAU = 16.0
GLA_CHUNK = 16
LRU_BLOCKS = 8
LRU_BLOCK_DIM = 64
LRU_C = 8.0
PEER_HEADS = 8
PEER_KEYS = 128
PEER_EXPERTS = PEER_KEYS * PEER_KEYS
PEER_HALF = 128
PEER_TOPK = 16

LANES = 128
SUBLANES = 8
VMEM_LIMIT = 48 * 1024 * 1024

_SRC = {}
_off = 0
for _name, _w in (("ssm_z", 512), ("ssm_x", 512), ("ssm_b", 128), ("ssm_c", 128), ("ssm_dt", 8),
                  ("gdn_q", 512), ("gdn_k", 512), ("gdn_v", 512), ("gdn_z", 512), ("gdn_beta", 4),
                  ("gdn_decay", 4), ("gla_q", 256), ("gla_k", 256), ("gla_v", 512), ("gla_r", 512),
                  ("gla_low", 16), ("lru_x", 512), ("lru_gate", 512), ("merge", 4096)):
    _SRC[_name] = (_off, _w)
    _off += _w
D_IN = _off
_DST_ORDER = ("merge", "ssm_z", "ssm_x", "gdn_q", "gdn_k", "gdn_v", "gdn_z", "gla_q", "gla_k",
              "gla_v", "gla_r", "lru_x", "lru_gate", "ssm_b", "ssm_c", "ssm_dt", "gdn_beta",
              "gdn_decay", "gla_low")
SMALL_W = 256
N_COLS = 4096 + 11 * 512 + 256 + SMALL_W
U512 = {"ssm_z": 8, "ssm_x": 9, "gdn_q": 10, "gdn_k": 11, "gdn_v": 12, "gdn_z": 13, "gla_qk": 14,
        "gla_v": 15, "gla_r": 16, "lru_x": 17, "lru_gate": 18}
U256_BC = 38
U256_SMALL = 39
SM_DT = 0
SM_BETA = 8
SM_DECAY = 12
SM_LOW = 16


def _permute_w_in(w):
    parts = [w[:, _SRC[name][0]:_SRC[name][0] + _SRC[name][1]] for name in _DST_ORDER]
    used = sum(_SRC[name][1] for name in _DST_ORDER)
    parts.append(jnp.zeros((w.shape[0], N_COLS - used), w.dtype))
    return jnp.concatenate(parts, axis=1).astype(BF16)


def _mm(a, b):
    return jnp.dot(a.astype(BF16), b.astype(BF16), preferred_element_type=F32)


def _mm_nt(a, b):
    return lax.dot_general(a.astype(BF16), b.astype(BF16), (((1,), (1,)), ((), ())),
                           preferred_element_type=F32)


def _mm_tn(a, b):
    return lax.dot_general(a.astype(BF16), b.astype(BF16), (((0,), (0,)), ((), ())),
                           preferred_element_type=F32)


def _mm_hi(a, b):
    return jnp.dot(a, b, precision=HIGHEST, preferred_element_type=F32)


def _sigmoid(x):
    return 1.0 / (1.0 + jnp.exp(-x))


def _silu(x):
    return x * _sigmoid(x)


def _softplus(x):
    return jnp.maximum(x, 0.0) + jnp.log1p(jnp.exp(-jnp.abs(x)))


def _gelu(x):
    c = math.sqrt(2.0 / math.pi)
    return 0.5 * x * (1.0 + jnp.tanh(c * (x + 0.044715 * (x * x * x))))


def _rms_rows(x, w):
    ms = jnp.mean(x * x, axis=-1, keepdims=True)
    return x * lax.rsqrt(ms + EPS) * w


def _causal_conv(ext_ref, w_ref, n_rows):
    acc = None
    for k in range(CONV_W):
        term = w_ref[k:k + 1, :] * ext_ref[SUBLANES - CONV_W + 1 + k:SUBLANES - CONV_W + 1 + k + n_rows, :]
        acc = term if acc is None else acc + term
    return acc


def _params(sem, flags=None):
    return pltpu.CompilerParams(dimension_semantics=sem, vmem_limit_bytes=VMEM_LIMIT, flags=flags)


ADA_TN = 1536


def _ada_kernel(c_ref, w_ref, b_ref, o_ref):
    c = c_ref[...]
    o_ref[0] = _mm(_silu(c), w_ref[0]) + b_ref[0]


def _ada_call(c_pad, w_ada, b_ada):
    n_l = w_ada.shape[0]
    n_out = w_ada.shape[2]
    return pl.pallas_call(
        _ada_kernel,
        grid=(n_l, n_out // ADA_TN),
        in_specs=[pl.BlockSpec((SUBLANES, D_MODEL), lambda l, j: (0, 0)),
                  pl.BlockSpec((1, D_MODEL, ADA_TN), lambda l, j: (l, 0, j)),
                  pl.BlockSpec((1, 1, ADA_TN), lambda l, j: (l, 0, j))],
        out_specs=pl.BlockSpec((1, SUBLANES, ADA_TN), lambda l, j: (l, 0, j)),
        out_shape=jax.ShapeDtypeStruct((n_l, SUBLANES, n_out), F32),
        compiler_params=_params(("arbitrary", "arbitrary")),
    )(c_pad, w_ada, b_ada.reshape(n_l, 1, n_out))


INP_TM = 1024
INP_TN = 2048


def _inproj_kernel(x_ref, nw_ref, sc_ref, sh_ref, w_ref, o_ref, h_scr):
    @pl.when(pl.program_id(1) == 0)
    def _():
        h = _rms_rows(x_ref[...], nw_ref[...])
        h = h * (1.0 + sc_ref[0]) + sh_ref[0]
        h_scr[...] = h.astype(BF16)

    o_ref[...] = jnp.dot(h_scr[...], w_ref[...], preferred_element_type=F32)


def _inproj_call(x2, nw, sc, sh, w_perm, seq):
    n_tok = x2.shape[0]
    per_b = seq // INP_TM
    return pl.pallas_call(
        _inproj_kernel,
        grid=(n_tok // INP_TM, N_COLS // INP_TN),
        in_specs=[pl.BlockSpec((INP_TM, D_MODEL), lambda i, j: (i, 0)),
                  pl.BlockSpec((1, D_MODEL), lambda i, j: (0, 0)),
                  pl.BlockSpec((1, 1, D_MODEL), lambda i, j: (i // per_b, 0, 0)),
                  pl.BlockSpec((1, 1, D_MODEL), lambda i, j: (i // per_b, 0, 0)),
                  pl.BlockSpec((D_MODEL, INP_TN), lambda i, j: (0, j))],
        out_specs=pl.BlockSpec((INP_TM, INP_TN), lambda i, j: (i, j)),
        out_shape=jax.ShapeDtypeStruct((n_tok, N_COLS), F32),
        scratch_shapes=[pltpu.VMEM((INP_TM, D_MODEL), BF16)],
        compiler_params=_params(("arbitrary", "arbitrary")),
    )(x2, nw, sc, sh, w_perm)


SSD_L = 128


def _ssd_kernel(z_ref, xs_ref, bc_ref, sm_ref, cwx_ref, cbx_ref, cwbc_ref, cbbc_ref, dtb_ref,
                alog_ref, dfull_ref, nw_ref, o_ref, extx, extbc, state):
    L = SSD_L
    t = pl.program_id(1)

    @pl.when(t == 0)
    def _():
        extx[0:SUBLANES, :] = jnp.zeros((SUBLANES, BRANCH), F32)
        extbc[0:SUBLANES, :] = jnp.zeros((SUBLANES, 256), F32)
        state[...] = jnp.zeros_like(state)

    extx[SUBLANES:SUBLANES + L, :] = xs_ref[...]
    extbc[SUBLANES:SUBLANES + L, :] = bc_ref[...]
    xs = _silu(_causal_conv(extx, cwx_ref, L) + cbx_ref[...])
    bc = _silu(_causal_conv(extbc, cwbc_ref, L) + cbbc_ref[...])
    extx[0:SUBLANES, :] = extx[L:L + SUBLANES, :]
    extbc[0:SUBLANES, :] = extbc[L:L + SUBLANES, :]

    dt = _softplus(sm_ref[...] + dtb_ref[...])
    da = dt * (-jnp.exp(alog_ref[...]))
    ri = lax.broadcasted_iota(jnp.int32, (L, L), 0)
    ci = lax.broadcasted_iota(jnp.int32, (L, L), 1)
    tril = ri >= ci
    cs = _mm_hi(tril.astype(F32), da)
    er = lax.broadcasted_iota(jnp.int32, (SMALL_W, BRANCH), 0)
    ec = lax.broadcasted_iota(jnp.int32, (SMALL_W, BRANCH), 1)
    expand = (er == (ec >> 6)).astype(F32)
    cs_full = _mm_hi(cs, expand)
    dt_full = _mm_hi(dt, expand)
    ecs_full = jnp.exp(cs_full)
    cs_last = cs_full[L - 1:L, :]
    w_full = jnp.exp(cs_last - cs_full)
    xdt = xs * dt_full
    xdtw = xdt * w_full
    cs_t = cs.T

    b128 = bc[:, 0:LANES]
    c128 = bc[:, LANES:2 * LANES]
    lane = lax.broadcasted_iota(jnp.int32, (1, LANES), 1)
    cg = [jnp.where(lane < SSM_STATE, c128, 0.0), jnp.where(lane >= SSM_STATE, c128, 0.0)]
    cb = [_mm_nt(cg[g], b128) for g in range(SSM_GROUPS)]

    y_pairs = []
    for p in range(SSM_HEADS // 2):
        xp = xdt[:, p * LANES:(p + 1) * LANES]
        yp = None
        for hh in range(2):
            h = 2 * p + hh
            g = h // (SSM_HEADS // SSM_GROUPS)
            col = cs[:, h:h + 1]
            row = cs_t[h:h + 1, :]
            seg = jnp.exp(jnp.where(tril, col - row, NEG_INF))
            att = cb[g] * seg
            hm = (lane < SSM_HEAD_DIM) if hh == 0 else (lane >= SSM_HEAD_DIM)
            term = _mm(att, jnp.where(hm, xp, 0.0))
            yp = term if yp is None else yp + term
        y_pairs.append(yp)
    y_diag = jnp.concatenate(y_pairs, axis=1)

    y_offs = []
    for g in range(SSM_GROUPS):
        sl = slice(g * 256, (g + 1) * 256)
        s_in = state[g]
        y_offs.append(_mm(cg[g], s_in) * ecs_full[:, sl])
        new = _mm_tn(b128, xdtw[:, sl])
        state[g] = s_in * ecs_full[L - 1:L, sl] + new
    y = y_diag + jnp.concatenate(y_offs, axis=1) + xs * dfull_ref[...]
    y = y * _silu(z_ref[...])
    o_ref[...] = _rms_rows(y, nw_ref[...]).astype(BF16)


def _ssd_call(proj, p, bsz, seq):
    nt = seq // SSD_L
    row = lambda blk: pl.BlockSpec((SSD_L, 512), lambda b, t: (b * nt + t, blk))
    full = lambda shape: pl.BlockSpec(shape, lambda b, t: (0,) * len(shape))
    return pl.pallas_call(
        _ssd_kernel,
        grid=(bsz, nt),
        in_specs=[row(U512["ssm_z"]), row(U512["ssm_x"]),
                  pl.BlockSpec((SSD_L, 256), lambda b, t: (b * nt + t, U256_BC)),
                  pl.BlockSpec((SSD_L, 256), lambda b, t: (b * nt + t, U256_SMALL)),
                  full((CONV_W, 512)), full((1, 512)), full((CONV_W, 256)), full((1, 256)),
                  full((1, SMALL_W)), full((1, SMALL_W)), full((1, 512)), full((1, 512))],
        out_specs=pl.BlockSpec((SSD_L, 512), lambda b, t: (b * nt + t, 0)),
        out_shape=jax.ShapeDtypeStruct((bsz * seq, BRANCH), BF16),
        scratch_shapes=[pltpu.VMEM((SUBLANES + SSD_L, 512), F32),
                        pltpu.VMEM((SUBLANES + SSD_L, 256), F32),
                        pltpu.VMEM((SSM_GROUPS, LANES, 256), F32)],
        compiler_params=_params(("arbitrary", "arbitrary")),
    )(proj, proj, proj, proj, p["cwx"], p["cbx"], p["cwbc"], p["cbbc"], p["dtb"], p["alog"],
      p["dfull"], p["nw"])


GDN_L = 64


def _gdn_kernel(q_ref, k_ref, v_ref, z_ref, sm_ref, cwq_ref, cwk_ref, cwv_ref, dtb_ref, alog_ref,
                nw_ref, o_ref, ext, state, *, bsz):
    L = GDN_L
    H = GDN_HEADS
    t = pl.program_id(0)

    @pl.when(t == 0)
    def _():
        ext[:, :, 0:SUBLANES, :] = jnp.zeros((3, bsz, SUBLANES, BRANCH), F32)
        state[...] = jnp.zeros_like(state)

    ri = lax.broadcasted_iota(jnp.int32, (L, L), 0)
    ci = lax.broadcasted_iota(jnp.int32, (L, L), 1)
    incl = ri >= ci
    strict = ri > ci
    eye = (ri == ci).astype(F32)
    bx = (ri >> 3) ^ (ci >> 3)
    blk = (bx > 0).astype(jnp.int32) + (bx > 1).astype(jnp.int32) + (bx > 3).astype(jnp.int32)
    tri = incl.astype(F32)
    nw = nw_ref[...]
    chains = [(b, h) for b in range(bsz) for h in range(H)]

    qkv, beta_all, gc_all, gc_t = [], [], [], []
    for b in range(bsz):
        outs = []
        for i, (r, w) in enumerate(((q_ref, cwq_ref), (k_ref, cwk_ref), (v_ref, cwv_ref))):
            e = ext.at[i, b]
            e[SUBLANES:SUBLANES + L, :] = r[b]
            outs.append(_silu(_causal_conv(e, w, L)))
            e[0:SUBLANES, :] = e[L:L + SUBLANES, :]
        qkv.append(outs)
        sm = sm_ref[b]
        beta_all.append(_sigmoid(sm))
        g_all = -jnp.exp(alog_ref[...]) * _softplus(sm + dtb_ref[...])
        gc = _mm_hi(tri, g_all)
        gc_all.append(gc)
        gc_t.append(gc.T)

    qs, ks, kbs, rhss, decays, egcs, glasts, gcbs = [], [], [], [], [], [], [], []
    for b, h in chains:
        sl = slice(h * GDN_DIM, (h + 1) * GDN_DIM)
        qh, kh, vh = (a[:, sl] for a in qkv[b])
        qh = qh * lax.rsqrt(jnp.sum(qh * qh, axis=-1, keepdims=True) + EPS) * (GDN_DIM ** -0.5)
        kh = kh * lax.rsqrt(jnp.sum(kh * kh, axis=-1, keepdims=True) + EPS)
        beta = beta_all[b][:, SM_BETA + h:SM_BETA + h + 1]
        gcol = gc_all[b][:, SM_DECAY + h:SM_DECAY + h + 1]
        grow = gc_t[b][SM_DECAY + h:SM_DECAY + h + 1, :]
        decays.append(jnp.exp(jnp.where(incl, gcol - grow, NEG_INF)))
        gcb = jnp.broadcast_to(gcol, (L, GDN_DIM))
        egc = jnp.exp(gcb)
        kb = kh * beta
        qs.append(qh); ks.append(kh); kbs.append(kb); gcbs.append(gcb); egcs.append(egc)
        glasts.append(gcb[L - 1:L, :])
        rhss.append(jnp.concatenate([vh * beta, kb * egc], axis=1))

    n = len(chains)
    rng = range(n)
    kk = [_mm_nt(kbs[c], ks[c]) for c in rng]
    qk = [_mm_nt(qs[c], ks[c]) for c in rng]
    ms = [jnp.where(strict, kk[c] * decays[c], 0.0) for c in rng]
    mds = [jnp.where(blk == 0, ms[c], 0.0) for c in rng]
    p2 = [_mm(mds[c], mds[c]) for c in rng]
    base = [eye - mds[c] for c in rng]
    bp = [_mm(base[c], p2[c]) for c in rng]
    p4 = [_mm(p2[c], p2[c]) for c in rng]
    base = [base[c] + bp[c] for c in rng]
    bq = [_mm(base[c], p4[c]) for c in rng]
    inv = [base[c] + bq[c] for c in rng]
    for lvl in range(1, int(math.log2(L // SUBLANES)) + 1):
        oi = [_mm(jnp.where(blk == lvl, ms[c], 0.0), inv[c]) for c in rng]
        ioi = [_mm(inv[c], oi[c]) for c in rng]
        inv = [inv[c] - ioi[c] for c in rng]
    sol = [_mm(inv[c], rhss[c]) for c in rng]
    s_in = [state[c] for c in rng]
    ws = [_mm(sol[c][:, GDN_DIM:], s_in[c]) for c in rng]
    qgs = [_mm(qs[c] * egcs[c], s_in[c]) for c in rng]
    v_new = [sol[c][:, :GDN_DIM] - ws[c] for c in rng]
    av = [_mm(qk[c] * decays[c], v_new[c]) for c in rng]
    kv = [_mm_tn(ks[c] * jnp.exp(glasts[c] - gcbs[c]), v_new[c]) for c in rng]
    for c in rng:
        state[c] = s_in[c] * jnp.exp(glasts[c]) + kv[c]
    for b in range(bsz):
        z = z_ref[b]
        outs = []
        for h in range(H):
            c = b * H + h
            sl = slice(h * GDN_DIM, (h + 1) * GDN_DIM)
            outs.append(_rms_rows(qgs[c] + av[c], nw[:, sl]) * _silu(z[:, sl]))
        o_ref[b] = jnp.concatenate(outs, axis=1).astype(BF16)


def _gdn_call(proj3, p, bsz, seq):
    nt = seq // GDN_L
    row = lambda blk, w=512: pl.BlockSpec((bsz, GDN_L, w), lambda t: (0, t, blk))
    full = lambda shape: pl.BlockSpec(shape, lambda t: (0,) * len(shape))
    return pl.pallas_call(
        functools.partial(_gdn_kernel, bsz=bsz),
        grid=(nt,),
        in_specs=[row(U512["gdn_q"]), row(U512["gdn_k"]), row(U512["gdn_v"]), row(U512["gdn_z"]),
                  row(U256_SMALL, 256),
                  full((CONV_W, 512)), full((CONV_W, 512)), full((CONV_W, 512)),
                  full((1, SMALL_W)), full((1, SMALL_W)), full((1, 512))],
        out_specs=pl.BlockSpec((bsz, GDN_L, 512), lambda t: (0, t, 0)),
        out_shape=jax.ShapeDtypeStruct((bsz, seq, BRANCH), BF16),
        scratch_shapes=[pltpu.VMEM((3, bsz, SUBLANES + GDN_L, 512), F32),
                        pltpu.VMEM((bsz * GDN_HEADS, GDN_DIM, GDN_DIM), F32)],
        compiler_params=_params(("arbitrary",)),
    )(proj3, proj3, proj3, proj3, proj3, p["cwq"], p["cwk"], p["cwv"], p["dtb"], p["alog"], p["nw"])


GLA_T = 128
GLA_QK = GLA_HEADS * GLA_DK


def _gla_kernel(qk_ref, v_ref, r_ref, sm_ref, wg_ref, bg_ref, nw_ref, o_ref, state):
    T = GLA_T
    C = GLA_CHUNK
    t = pl.program_id(1)

    @pl.when(t == 0)
    def _():
        state[...] = jnp.zeros_like(state)

    qk = qk_ref[...]
    q = qk[:, :GLA_QK] * (GLA_DK ** -0.5)
    k = qk[:, GLA_QK:]
    v = v_ref[...]
    pre = _mm(sm_ref[...], wg_ref[...]) + bg_ref[...]
    log_a = (jnp.minimum(pre, 0.0) - jnp.log1p(jnp.exp(-jnp.abs(pre)))) / GLA_TAU
    ri = lax.broadcasted_iota(jnp.int32, (T, T), 0)
    ci = lax.broadcasted_iota(jnp.int32, (T, T), 1)
    blocktri = ((ri >> 4) == (ci >> 4)) & (ri >= ci)
    G = _mm_hi(blocktri.astype(F32), log_a)

    rr = lax.broadcasted_iota(jnp.int32, (GLA_QK, BRANCH), 0)
    rc = lax.broadcasted_iota(jnp.int32, (GLA_QK, BRANCH), 1)
    red = ((rr >> 6) == (rc >> 7)).astype(BF16)
    rmod = lax.broadcasted_iota(jnp.int32, (T, GLA_QK), 0) & (C - 1)
    o = jnp.zeros((T, BRANCH), F32)
    nc = T // C

    def chunk_row(x, jl):
        w = x.shape[1]
        x3 = x.reshape(nc, C, w)
        return jnp.broadcast_to(x3[:, jl:jl + 1, :], (nc, C, w)).reshape(T, w)

    for jl in range(C):
        ks = chunk_row(k, jl)
        gs = chunk_row(G, jl)
        vs = chunk_row(v, jl)
        qd = jnp.where(rmod >= jl, q, 0.0)
        prod = qd * ks * jnp.exp(jnp.minimum(G - gs, 0.0))
        hi = prod.astype(BF16)
        lo = (prod - hi.astype(F32)).astype(BF16)
        a_full = (jnp.dot(hi, red, preferred_element_type=F32)
                  + jnp.dot(lo, red, preferred_element_type=F32))
        o = o + a_full * vs

    sr = lax.broadcasted_iota(jnp.int32, (BRANCH, GLA_QK), 0)
    sc = lax.broadcasted_iota(jnp.int32, (BRANCH, GLA_QK), 1)
    blockdiag = (sr >> 7) == (sc >> 6)
    st = state[...]
    inter = []
    for c in range(T // C):
        rows = slice(c * C, (c + 1) * C)
        gc = G[rows]
        glast = gc[C - 1:C, :]
        qg = q[rows] * jnp.exp(gc)
        kd = k[rows] * jnp.exp(glast - gc)
        inter.append(_mm_nt(qg, st))
        st = jnp.where(blockdiag, st * jnp.exp(glast) + _mm_tn(v[rows], kd), 0.0)
    state[...] = st
    o = o + jnp.concatenate(inter, axis=0)
    r = r_ref[...]
    nw = nw_ref[...]
    outs = []
    for h in range(GLA_HEADS):
        sl = slice(h * GLA_DV, (h + 1) * GLA_DV)
        outs.append(_rms_rows(o[:, sl], nw[:, sl]) * _silu(r[:, sl]))
    o_ref[...] = jnp.concatenate(outs, axis=1).astype(BF16)


def _gla_call(proj, p, bsz, seq):
    nt = seq // GLA_T
    row = lambda blk: pl.BlockSpec((GLA_T, 512), lambda b, t: (b * nt + t, blk))
    full = lambda shape: pl.BlockSpec(shape, lambda b, t: (0,) * len(shape))
    return pl.pallas_call(
        _gla_kernel,
        grid=(bsz, nt),
        in_specs=[row(U512["gla_qk"]), row(U512["gla_v"]), row(U512["gla_r"]),
                  pl.BlockSpec((GLA_T, 256), lambda b, t: (b * nt + t, U256_SMALL)),
                  full((SMALL_W, GLA_QK)), full((1, GLA_QK)), full((1, 512))],
        out_specs=pl.BlockSpec((GLA_T, 512), lambda b, t: (b * nt + t, 0)),
        out_shape=jax.ShapeDtypeStruct((bsz * seq, BRANCH), BF16),
        scratch_shapes=[pltpu.VMEM((BRANCH, GLA_QK), F32)],
        compiler_params=_params(("arbitrary", "arbitrary")),
    )(proj, proj, proj, proj, p["wg"], p["bg"], p["nw"])


LRU_T = 256
LRU_PAD = LRU_T // 2


def _lru_kernel(x_ref, gate_ref, cw_ref, cb_ref, wa_ref, ba_ref, wx_ref, bx_ref, lam_ref, o_ref,
                ext, abuf, hbuf, carry):
    T = LRU_T
    P = LRU_PAD
    t = pl.program_id(1)

    @pl.when(t == 0)
    def _():
        ext[0:SUBLANES, :] = jnp.zeros((SUBLANES, BRANCH), F32)
        abuf[0:P, :] = jnp.ones((P, BRANCH), F32)
        hbuf[0:P, :] = jnp.zeros((P, BRANCH), F32)
        carry[...] = jnp.zeros_like(carry)

    ext[SUBLANES:SUBLANES + T, :] = x_ref[...]
    xc = _causal_conv(ext, cw_ref, T) + cb_ref[...]
    ext[0:SUBLANES, :] = ext[T:T + SUBLANES, :]
    gate_r = _sigmoid(_mm(xc, wa_ref[...]) + ba_ref[...])
    gate_i = _sigmoid(_mm(xc, wx_ref[...]) + bx_ref[...])
    log_a = -LRU_C * gate_r * _softplus(-lam_ref[...])
    abuf[P:P + T, :] = jnp.exp(log_a)
    th = jnp.tanh(log_a)
    hbuf[P:P + T, :] = jnp.sqrt(-2.0 * th / (1.0 - th)) * (gate_i * xc)
    s = 1
    while s < T:
        a_cur = abuf[P:P + T, :]
        h_cur = hbuf[P:P + T, :]
        a_sh = abuf[P - s:P - s + T, :]
        h_sh = hbuf[P - s:P - s + T, :]
        hbuf[P:P + T, :] = h_cur + a_cur * h_sh
        abuf[P:P + T, :] = a_cur * a_sh
        s *= 2
    h = hbuf[P:P + T, :] + abuf[P:P + T, :] * carry[0:1, :]
    carry[0:1, :] = h[T - 1:T, :]
    o_ref[...] = (h * _gelu(gate_ref[...])).astype(BF16)


def _lru_call(proj, p, bsz, seq):
    nt = seq // LRU_T
    row = lambda blk: pl.BlockSpec((LRU_T, 512), lambda b, t: (b * nt + t, blk))
    full = lambda shape: pl.BlockSpec(shape, lambda b, t: (0,) * len(shape))
    return pl.pallas_call(
        _lru_kernel,
        grid=(bsz, nt),
        in_specs=[row(U512["lru_x"]), row(U512["lru_gate"]),
                  full((CONV_W, 512)), full((1, 512)), full((512, 512)), full((1, 512)),
                  full((512, 512)), full((1, 512)), full((1, 512))],
        out_specs=pl.BlockSpec((LRU_T, 512), lambda b, t: (b * nt + t, 0)),
        out_shape=jax.ShapeDtypeStruct((bsz * seq, BRANCH), BF16),
        scratch_shapes=[pltpu.VMEM((SUBLANES + LRU_T, 512), F32),
                        pltpu.VMEM((LRU_PAD + LRU_T, 512), F32),
                        pltpu.VMEM((LRU_PAD + LRU_T, 512), F32),
                        pltpu.VMEM((SUBLANES, 512), F32)],
        compiler_params=_params(("arbitrary", "arbitrary")),
    )(proj, proj, p["cw"], p["cb"], p["wa"], p["ba"], p["wx"], p["bx"], p["lam"])


MRG_TM = 256


def _merge_kernel(x_ref, g_ref, lg_ref, y0_ref, y1_ref, y2_ref, y3_ref, wb_ref, wo_ref, o_ref):
    merged = None
    for i, y_ref in enumerate((y0_ref, y1_ref, y2_ref, y3_ref)):
        br = jnp.dot(y_ref[...], wb_ref[i], preferred_element_type=F32)
        term = _sigmoid(lg_ref[:, i * D_MODEL:(i + 1) * D_MODEL]) * br
        merged = term if merged is None else merged + term
    out = jnp.dot(merged.astype(BF16), wo_ref[...], preferred_element_type=F32)
    o_ref[...] = x_ref[...] + g_ref[0] * out


def _merge_call(x2, g1, proj, ys, wb, wo, seq):
    n_tok = x2.shape[0]
    per_b = seq // MRG_TM
    yspec = pl.BlockSpec((MRG_TM, BRANCH), lambda i: (i, 0))
    return pl.pallas_call(
        _merge_kernel,
        grid=(n_tok // MRG_TM,),
        in_specs=[pl.BlockSpec((MRG_TM, D_MODEL), lambda i: (i, 0)),
                  pl.BlockSpec((1, 1, D_MODEL), lambda i: (i // per_b, 0, 0)),
                  pl.BlockSpec((MRG_TM, N_BRANCH * D_MODEL), lambda i: (i, 0)),
                  yspec, yspec, yspec, yspec,
                  pl.BlockSpec((N_BRANCH, BRANCH, D_MODEL), lambda i: (0, 0, 0)),
                  pl.BlockSpec((D_MODEL, D_MODEL), lambda i: (0, 0))],
        out_specs=pl.BlockSpec((MRG_TM, D_MODEL), lambda i: (i, 0)),
        out_shape=jax.ShapeDtypeStruct((n_tok, D_MODEL), F32),
        compiler_params=_params(("arbitrary",)),
    )(x2, g1, proj, *ys, wb, wo)


PS_TB = 512
N_SCORE_ROWS = 2 * PEER_HEADS * PEER_KEYS


def _peer_score_kernel(x_ref, nw_ref, sc_ref, sh_ref, wqt_ref, keys_ref, h2t_ref, st_ref):
    h = _rms_rows(x_ref[...], nw_ref[...])
    h = h * (1.0 + sc_ref[0]) + sh_ref[0]
    ht = h.T.astype(BF16)
    h2t_ref[...] = ht
    qt = jnp.dot(wqt_ref[...], ht, preferred_element_type=F32).astype(BF16)
    for g in range(2 * PEER_HEADS):
        rows = slice(g * PEER_KEYS, (g + 1) * PEER_KEYS)
        st_ref[rows, :] = jnp.dot(keys_ref[g], qt[rows, :], preferred_element_type=F32)


def _peer_score_call(x2, nw, sc, sh, wqt, keys, seq):
    n_tok = x2.shape[0]
    per_b = seq // PS_TB
    return pl.pallas_call(
        _peer_score_kernel,
        grid=(n_tok // PS_TB,),
        in_specs=[pl.BlockSpec((PS_TB, D_MODEL), lambda i: (i, 0)),
                  pl.BlockSpec((1, D_MODEL), lambda i: (0, 0)),
                  pl.BlockSpec((1, 1, D_MODEL), lambda i: (i // per_b, 0, 0)),
                  pl.BlockSpec((1, 1, D_MODEL), lambda i: (i // per_b, 0, 0)),
                  pl.BlockSpec((N_SCORE_ROWS, D_MODEL), lambda i: (0, 0)),
                  pl.BlockSpec((2 * PEER_HEADS, PEER_KEYS, PEER_HALF), lambda i: (0, 0, 0))],
        out_specs=[pl.BlockSpec((D_MODEL, PS_TB), lambda i: (0, i)),
                   pl.BlockSpec((N_SCORE_ROWS, PS_TB), lambda i: (0, i))],
        out_shape=[jax.ShapeDtypeStruct((D_MODEL, n_tok), BF16),
                   jax.ShapeDtypeStruct((N_SCORE_ROWS, n_tok), F32)],
        compiler_params=_params(("arbitrary",)),
    )(x2, nw, sc, sh, wqt, keys)


PT_TL = 128
NOT_RANKED = 255.0
_CAND = [(i, j) for i in range(PEER_TOPK) for j in range(PEER_TOPK) if (i + 1) * (j + 1) <= PEER_TOPK]
N_CAND_ROWS = -(-len(_CAND) // SUBLANES) * SUBLANES


def _pop_max(x, iota):
    m = jnp.max(x, axis=0, keepdims=True)
    first = jnp.min(jnp.where(x == m, iota, float(x.shape[0])), axis=0, keepdims=True)
    return m, first, jnp.where(iota == first, NEG_INF, x)


def _gate_tables_head(st_ref, ce_ref, r2_ref, e2_ref, cand, h, exact_ties):
    TL = PT_TL
    iota_k = lax.broadcasted_iota(jnp.int32, (PEER_KEYS, TL), 0).astype(F32)
    iota_c = lax.broadcasted_iota(jnp.int32, (N_CAND_ROWS, TL), 0).astype(F32)
    rows1 = slice(h * PEER_KEYS, (h + 1) * PEER_KEYS)
    rows2 = slice((PEER_HEADS + h) * PEER_KEYS, (PEER_HEADS + h + 1) * PEER_KEYS)
    s1 = st_ref[rows1, :]
    s2 = st_ref[rows2, :]
    x = s1
    t1, pick1 = [], []
    for _ in range(PEER_TOPK):
        if exact_ties:
            m, f, x = _pop_max(x, iota_k)
            pick1.append(f)
        else:
            m = jnp.max(x, axis=0, keepdims=True)
            x = jnp.where(x == m, NEG_INF, x)
        t1.append(m)
    x = s2
    t2 = []
    rank2 = jnp.full((PEER_KEYS, TL), NOT_RANKED, F32)
    for r in range(PEER_TOPK):
        if exact_ties:
            m, f, x = _pop_max(x, iota_k)
            hit_rows = iota_k == f
        else:
            m = jnp.max(x, axis=0, keepdims=True)
            hit_rows = x == m
            x = jnp.where(hit_rows, NEG_INF, x)
        t2.append(m)
        rank2 = jnp.where(hit_rows, float(r), rank2)
    for n, (i, j) in enumerate(_CAND):
        cand[n:n + 1, :] = t1[i] + t2[j]
    c = cand[...]
    x = c
    tau = None
    for _ in range(PEER_TOPK):
        tau, _, x = _pop_max(x, iota_c)
    m1, m2 = t1[0], t2[0]
    zsum = jnp.sum(jnp.where(c >= tau, jnp.exp(c - (m1 + m2)), 0.0), axis=0, keepdims=True)
    count1 = jnp.zeros((PEER_KEYS, TL), F32)
    for i in range(PEER_TOPK):
        cnt = None
        for j in range(PEER_TOPK):
            if (i + 1) * (j + 1) <= PEER_TOPK:
                hit = jnp.where(t1[i] + t2[j] >= tau, 1.0, 0.0)
                cnt = hit if cnt is None else cnt + hit
        sel = (iota_k == pick1[i]) if exact_ties else (s1 == t1[i])
        count1 = jnp.where(sel, cnt, count1)
    ce_ref[rows1, :] = count1
    ce_ref[rows2, :] = jnp.exp(s1 - m1) * (0.5 / zsum)
    r2_ref[rows1, :] = rank2.astype(BF16)
    e2_ref[rows1, :] = jnp.exp(s2 - m2).astype(BF16)
    if exact_ties:
        return None
    n1 = jnp.sum(jnp.where(s1 >= t1[-1], 1.0, 0.0), axis=0, keepdims=True)
    n2 = jnp.sum(jnp.where(s2 >= t2[-1], 1.0, 0.0), axis=0, keepdims=True)
    return jnp.abs(n1 - PEER_TOPK) + jnp.abs(n2 - PEER_TOPK)


def _peer_gate_kernel(st_ref, ce_ref, r2_ref, e2_ref, cand):
    cand[...] = jnp.full((N_CAND_ROWS, PT_TL), NEG_INF, F32)
    tied = None
    for h in range(PEER_HEADS):
        t = _gate_tables_head(st_ref, ce_ref, r2_ref, e2_ref, cand, h, exact_ties=False)
        tied = t if tied is None else tied + t

    @pl.when(jnp.max(tied) > 0.0)
    def _():
        for h in range(PEER_HEADS):
            _gate_tables_head(st_ref, ce_ref, r2_ref, e2_ref, cand, h, exact_ties=True)


def _peer_gate_call(scores_t):
    n_tok = scores_t.shape[1]
    half = PEER_HEADS * PEER_KEYS
    return pl.pallas_call(
        _peer_gate_kernel,
        grid=(n_tok // PT_TL,),
        in_specs=[pl.BlockSpec((N_SCORE_ROWS, PT_TL), lambda i: (0, i))],
        out_specs=[pl.BlockSpec((N_SCORE_ROWS, PT_TL), lambda i: (0, i)),
                   pl.BlockSpec((half, PT_TL), lambda i: (0, i)),
                   pl.BlockSpec((half, PT_TL), lambda i: (0, i))],
        out_shape=[jax.ShapeDtypeStruct((N_SCORE_ROWS, n_tok), F32),
                   jax.ShapeDtypeStruct((half, n_tok), BF16),
                   jax.ShapeDtypeStruct((half, n_tok), BF16)],
        scratch_shapes=[pltpu.VMEM((N_CAND_ROWS, PT_TL), F32)],
        compiler_params=_params(("arbitrary",)),
    )(scores_t)


PE_TB = 512
PE_EB = 2048
HALF_ROWS = PEER_HEADS * PEER_KEYS


def _bcast_rows_bf16(row):
    r16 = jnp.broadcast_to(row, (2 * SUBLANES, LANES)).astype(BF16)
    return jnp.concatenate([r16] * (PEER_KEYS // (2 * SUBLANES)), axis=0)


def _peer_expert_kernel(h2t_ref, u_ref, vt_ref, ce_ref, r2_ref, e2_ref, x_ref, g_ref, fw_ref,
                        o_ref, acc, zt, *, final_norm):
    j = pl.program_id(1)
    H = PEER_HEADS

    @pl.when(j == 0)
    def _():
        acc[...] = jnp.zeros_like(acc)

    c0 = math.sqrt(2.0 / math.pi)
    zero = jnp.zeros((PEER_KEYS, LANES), BF16)
    st = jnp.dot(u_ref[...], h2t_ref[...], preferred_element_type=F32)
    for q in range(PE_EB // PEER_KEYS):
        ceq = ce_ref[q]
        rs = slice(q * PEER_KEYS, (q + 1) * PEER_KEYS)
        for lc in range(PE_TB // LANES):
            ls = slice(lc * LANES, (lc + 1) * LANES)
            w = None
            for h in range(H):
                rows = slice(h * PEER_KEYS, (h + 1) * PEER_KEYS)
                cnt = _bcast_rows_bf16(ceq[h:h + 1, ls])
                g1 = _bcast_rows_bf16(ceq[H + h:H + h + 1, ls])
                sel_g1 = jnp.minimum(jnp.maximum(cnt - r2_ref[rows, ls], zero), g1)
                term = sel_g1 * e2_ref[rows, ls]
                w = term if w is None else w + term
            x = st[rs, ls]
            th = jnp.tanh(x * (c0 + (c0 * 0.044715) * (x * x)))
            zt[rs, ls] = (x + x * th).astype(BF16) * w
    acc[...] += jnp.dot(vt_ref[...], zt[...], preferred_element_type=F32)

    @pl.when(j == pl.num_programs(1) - 1)
    def _():
        xn = x_ref[...] + g_ref[0] * acc[...].T
        if final_norm:
            xn = _rms_rows(xn, fw_ref[...])
        o_ref[...] = xn


def _peer_expert_call(h2, u_bf, vt_bf, ce, r2, e2, x2, g2, fw, seq, final_norm):
    n_tok = x2.shape[0]
    per_b = seq // PE_TB
    kern = functools.partial(_peer_expert_kernel, final_norm=final_norm)
    ce_k = ce.reshape(2, PEER_HEADS, PEER_KEYS, n_tok).transpose(2, 0, 1, 3)
    ce_k = ce_k.reshape(PEER_KEYS, 2 * PEER_HEADS, n_tok)
    return pl.pallas_call(
        kern,
        grid=(n_tok // PE_TB, PEER_EXPERTS // PE_EB),
        in_specs=[pl.BlockSpec((D_MODEL, PE_TB), lambda i, j: (0, i)),
                  pl.BlockSpec((PE_EB, D_MODEL), lambda i, j: (j, 0)),
                  pl.BlockSpec((D_MODEL, PE_EB), lambda i, j: (0, j)),
                  pl.BlockSpec((PE_EB // PEER_KEYS, 2 * PEER_HEADS, PE_TB), lambda i, j: (j, 0, i)),
                  pl.BlockSpec((HALF_ROWS, PE_TB), lambda i, j: (0, i)),
                  pl.BlockSpec((HALF_ROWS, PE_TB), lambda i, j: (0, i)),
                  pl.BlockSpec((PE_TB, D_MODEL), lambda i, j: (i, 0)),
                  pl.BlockSpec((1, 1, D_MODEL), lambda i, j: (i // per_b, 0, 0)),
                  pl.BlockSpec((1, D_MODEL), lambda i, j: (0, 0))],
        out_specs=pl.BlockSpec((PE_TB, D_MODEL), lambda i, j: (i, 0)),
        out_shape=jax.ShapeDtypeStruct((n_tok, D_MODEL), F32),
        scratch_shapes=[pltpu.VMEM((D_MODEL, PE_TB), F32),
                        pltpu.VMEM((PE_EB, PE_TB), BF16)],
        compiler_params=_params(("arbitrary", "arbitrary")),
    )(h2, u_bf, vt_bf, ce_k, r2, e2, x2, g2, fw)


def _pad_lanes(vec, start, width=SMALL_W):
    out = jnp.zeros((1, width), F32)
    return lax.dynamic_update_slice(out, vec.reshape(1, -1).astype(F32), (0, start))


def _block_diag(w):
    n, d, e = w.shape
    eye = jnp.eye(n, dtype=w.dtype)
    return (eye[:, None, :, None] * w[:, :, None, :]).reshape(n * d, n * e)


def kernel(x, c, w_ada, b_ada, norm_mix_w, norm_ffn_w, w_in, ssm_conv_w, ssm_conv_b, ssm_dt_bias,
           ssm_a_log, ssm_d, ssm_norm_w, gdn_conv_w, gdn_a_log, gdn_dt_bias, gdn_norm_w, gla_w_gate,
           gla_b_gate, gla_norm_w, lru_conv_w, lru_conv_b, lru_w_a, lru_b_a, lru_w_x, lru_b_x,
           lru_lambda, w_branch, w_out, peer_w_q, peer_sub_keys, peer_u, peer_v, final_norm_w):
    bsz, seq, d = x.shape
    n_layers = w_in.shape[0]
    n_tok = bsz * seq
    x2 = x.reshape(n_tok, d)

    c_pad = jnp.zeros((SUBLANES, d), F32).at[:bsz].set(c)
    mod = _ada_call(c_pad, w_ada, b_ada)

    row1 = lambda v: v.reshape(1, -1).astype(F32)

    for l in range(n_layers):
        m6 = mod[l, :bsz].reshape(bsz, 6, 1, d)
        sh1, sc1, g1, sh2, sc2, g2 = (m6[:, i] for i in range(6))
        w_perm = _permute_w_in(w_in[l])
        proj = _inproj_call(x2, row1(norm_mix_w[l]), sc1, sh1, w_perm, seq)

        ssd_p = dict(cwx=ssm_conv_w[l][:, :512], cbx=row1(ssm_conv_b[l][:512]),
                     cwbc=ssm_conv_w[l][:, 512:], cbbc=row1(ssm_conv_b[l][512:]),
                     dtb=_pad_lanes(ssm_dt_bias[l], SM_DT), alog=_pad_lanes(ssm_a_log[l], SM_DT),
                     dfull=row1(jnp.repeat(ssm_d[l], SSM_HEAD_DIM)), nw=row1(ssm_norm_w[l]))
        y_ssd = _ssd_call(proj, ssd_p, bsz, seq)

        gdn_p = dict(cwq=gdn_conv_w[l][:, :512], cwk=gdn_conv_w[l][:, 512:1024],
                     cwv=gdn_conv_w[l][:, 1024:], dtb=_pad_lanes(gdn_dt_bias[l], SM_DECAY),
                     alog=_pad_lanes(gdn_a_log[l], SM_DECAY),
                     nw=row1(jnp.tile(gdn_norm_w[l], GDN_HEADS)))
        y_gdn = _gdn_call(proj.reshape(bsz, seq, N_COLS), gdn_p, bsz, seq).reshape(n_tok, BRANCH)

        wg = jnp.zeros((SMALL_W, GLA_QK), F32).at[SM_LOW:SM_LOW + GLA_RANK].set(gla_w_gate[l])
        gla_p = dict(wg=wg.astype(BF16), bg=row1(gla_b_gate[l]),
                     nw=row1(jnp.tile(gla_norm_w[l], GLA_HEADS)))
        y_gla = _gla_call(proj, gla_p, bsz, seq)

        lru_p = dict(cw=lru_conv_w[l], cb=row1(lru_conv_b[l]),
                     wa=_block_diag(lru_w_a[l]).astype(BF16), ba=row1(lru_b_a[l]),
                     wx=_block_diag(lru_w_x[l]).astype(BF16), bx=row1(lru_b_x[l]),
                     lam=row1(lru_lambda[l]))
        y_lru = _lru_call(proj, lru_p, bsz, seq)

        x2 = _merge_call(x2, g1, proj, (y_ssd, y_gdn, y_gla, y_lru),
                         w_branch[l].astype(BF16), w_out[l].astype(BF16), seq)

        wqt = peer_w_q[l].reshape(d, PEER_HEADS, 2, PEER_HALF).transpose(2, 1, 3, 0)
        wqt = wqt.reshape(N_SCORE_ROWS, d).astype(BF16)
        keys = peer_sub_keys[l].transpose(1, 0, 2, 3).reshape(2 * PEER_HEADS, PEER_KEYS, PEER_HALF)
        h2, scores_t = _peer_score_call(x2, row1(norm_ffn_w[l]), sc2, sh2, wqt, keys.astype(BF16), seq)
        ce, r2, e2 = _peer_gate_call(scores_t)
        x2 = _peer_expert_call(h2, peer_u[l].astype(BF16), peer_v[l].T.astype(BF16), ce, r2, e2,
                               x2, g2, row1(final_norm_w), seq, final_norm=(l == n_layers - 1))
    return x2.reshape(bsz, seq, d)
```

```python
import functools
import math

import jax
import jax.numpy as jnp
from jax import lax
from jax.experimental import pallas as pl
from jax.experimental.pallas import tpu as pltpu

F32 = jnp.float32
BF16 = jnp.bfloat16
HIGHEST = lax.Precision.HIGHEST
NEG_INF = float("-inf")

D_MODEL = 1024
N_LAYERS = 2
EPS = 1e-6
CONV_W = 4
BRANCH = 512
N_BRANCH = 4
SSM_HEADS = 8
SSM_HEAD_DIM = 64
SSM_GROUPS = 2
SSM_STATE = 64
GDN_HEADS = 4
GDN_DIM = 128
GLA_HEADS = 4
GLA_DK = 64
GLA_DV = 128
GLA_RANK = 16
GLA_TAU = 16.0
GLA_CHUNK = 16
LRU_BLOCKS = 8
LRU_BLOCK_DIM = 64
LRU_C = 8.0
PEER_HEADS = 8
PEER_KEYS = 128
PEER_EXPERTS = PEER_KEYS * PEER_KEYS
PEER_HALF = 128
PEER_TOPK = 16

LANES = 128
SUBLANES = 8
VMEM_LIMIT = 48 * 1024 * 1024

_SRC = {}
_off = 0
for _name, _w in (("ssm_z", 512), ("ssm_x", 512), ("ssm_b", 128), ("ssm_c", 128), ("ssm_dt", 8),
                  ("gdn_q", 512), ("gdn_k", 512), ("gdn_v", 512), ("gdn_z", 512), ("gdn_beta", 4),
                  ("gdn_decay", 4), ("gla_q", 256), ("gla_k", 256), ("gla_v", 512), ("gla_r", 512),
                  ("gla_low", 16), ("lru_x", 512), ("lru_gate", 512), ("merge", 4096)):
    _SRC[_name] = (_off, _w)
    _off += _w
D_IN = _off
_DST_ORDER = ("merge", "ssm_z", "ssm_x", "gdn_q", "gdn_k", "gdn_v", "gdn_z", "gla_q", "gla_k",
              "gla_v", "gla_r", "lru_x", "lru_gate", "ssm_b", "ssm_c", "ssm_dt", "gdn_beta",
              "gdn_decay", "gla_low")
SMALL_W = 256
N_COLS = 4096 + 11 * 512 + 256 + SMALL_W
U512 = {"ssm_z": 8, "ssm_x": 9, "gdn_q": 10, "gdn_k": 11, "gdn_v": 12, "gdn_z": 13, "gla_qk": 14,
        "gla_v": 15, "gla_r": 16, "lru_x": 17, "lru_gate": 18}
U256_BC = 38
U256_SMALL = 39
SM_DT = 0
SM_BETA = 8
SM_DECAY = 12
SM_LOW = 16


def _permute_w_in(w):
    w = w.astype(BF16)
    parts = [w[..., _SRC[name][0]:_SRC[name][0] + _SRC[name][1]] for name in _DST_ORDER]
    used = sum(_SRC[name][1] for name in _DST_ORDER)
    parts.append(jnp.zeros(w.shape[:-1] + (N_COLS - used,), w.dtype))
    return jnp.concatenate(parts, axis=-1)


def _mm(a, b):
    return jnp.dot(a.astype(BF16), b.astype(BF16), preferred_element_type=F32)


def _mm_nt(a, b):
    return lax.dot_general(a.astype(BF16), b.astype(BF16), (((1,), (1,)), ((), ())),
                           preferred_element_type=F32)


def _mm_tn(a, b):
    return lax.dot_general(a.astype(BF16), b.astype(BF16), (((0,), (0,)), ((), ())),
                           preferred_element_type=F32)


def _mm_hi(a, b):
    return jnp.dot(a, b, precision=HIGHEST, preferred_element_type=F32)


def _sigmoid(x):
    return 1.0 / (1.0 + jnp.exp(-x))


def _silu(x):
    return x * _sigmoid(x)


def _softplus(x):
    return jnp.maximum(x, 0.0) + jnp.log1p(jnp.exp(-jnp.abs(x)))


def _gelu(x):
    c = math.sqrt(2.0 / math.pi)
    return 0.5 * x * (1.0 + jnp.tanh(c * (x + 0.044715 * (x * x * x))))


def _rms_rows(x, w):
    ms = jnp.mean(x * x, axis=-1, keepdims=True)
    return x * lax.rsqrt(ms + EPS) * w


def _causal_conv(ext_ref, w_ref, n_rows):
    acc = None
    for k in range(CONV_W):
        term = w_ref[k:k + 1, :] * ext_ref[SUBLANES - CONV_W + 1 + k:SUBLANES - CONV_W + 1 + k + n_rows, :]
        acc = term if acc is None else acc + term
    return acc


def _params(sem, flags=None):
    return pltpu.CompilerParams(dimension_semantics=sem, vmem_limit_bytes=VMEM_LIMIT, flags=flags)


ADA_TN = 1536


def _ada_kernel(c_ref, w_ref, b_ref, o_ref):
    c = c_ref[...]
    o_ref[0] = _mm(_silu(c), w_ref[0]) + b_ref[0]


def _ada_call(c_pad, w_ada, b_ada):
    n_l = w_ada.shape[0]
    n_out = w_ada.shape[2]
    return pl.pallas_call(
        _ada_kernel,
        grid=(n_l, n_out // ADA_TN),
        in_specs=[pl.BlockSpec((SUBLANES, D_MODEL), lambda l, j: (0, 0)),
                  pl.BlockSpec((1, D_MODEL, ADA_TN), lambda l, j: (l, 0, j)),
                  pl.BlockSpec((1, 1, ADA_TN), lambda l, j: (l, 0, j))],
        out_specs=pl.BlockSpec((1, SUBLANES, ADA_TN), lambda l, j: (l, 0, j)),
        out_shape=jax.ShapeDtypeStruct((n_l, SUBLANES, n_out), F32),
        compiler_params=_params(("arbitrary", "arbitrary")),
    )(c_pad, w_ada, b_ada.reshape(n_l, 1, n_out))


INP_TM = 1024
INP_TN = 2048


def _inproj_kernel(x_ref, nw_ref, sc_ref, sh_ref, w_ref, o_ref, h_scr):
    @pl.when(pl.program_id(1) == 0)
    def _():
        h = _rms_rows(x_ref[...], nw_ref[...])
        h = h * (1.0 + sc_ref[0]) + sh_ref[0]
        h_scr[...] = h.astype(BF16)

    o_ref[...] = jnp.dot(h_scr[...], w_ref[0], preferred_element_type=F32)


def _inproj_call(x2, nw, sc, sh, w_perm, layer, seq):
    n_tok = x2.shape[0]
    per_b = seq // INP_TM
    return pl.pallas_call(
        _inproj_kernel,
        grid=(n_tok // INP_TM, N_COLS // INP_TN),
        in_specs=[pl.BlockSpec((INP_TM, D_MODEL), lambda i, j: (i, 0)),
                  pl.BlockSpec((1, D_MODEL), lambda i, j: (0, 0)),
                  pl.BlockSpec((1, 1, D_MODEL), lambda i, j: (i // per_b, 0, 0)),
                  pl.BlockSpec((1, 1, D_MODEL), lambda i, j: (i // per_b, 0, 0)),
                  pl.BlockSpec((1, D_MODEL, INP_TN), lambda i, j: (layer, 0, j))],
        out_specs=pl.BlockSpec((INP_TM, INP_TN), lambda i, j: (i, j)),
        out_shape=jax.ShapeDtypeStruct((n_tok, N_COLS), F32),
        scratch_shapes=[pltpu.VMEM((INP_TM, D_MODEL), BF16)],
        compiler_params=_params(("arbitrary", "arbitrary")),
    )(x2, nw, sc, sh, w_perm)


SSD_L = 128


def _ssd_kernel(z_ref, xs_ref, bc_ref, sm_ref, cwx_ref, cbx_ref, cwbc_ref, cbbc_ref, dtb_ref,
                alog_ref, dfull_ref, nw_ref, o_ref, extx, extbc, state):
    L = SSD_L
    t = pl.program_id(1)

    @pl.when(t == 0)
    def _():
        extx[0:SUBLANES, :] = jnp.zeros((SUBLANES, BRANCH), F32)
        extbc[0:SUBLANES, :] = jnp.zeros((SUBLANES, 256), F32)
        state[...] = jnp.zeros_like(state)

    extx[SUBLANES:SUBLANES + L, :] = xs_ref[...]
    extbc[SUBLANES:SUBLANES + L, :] = bc_ref[...]
    xs = _silu(_causal_conv(extx, cwx_ref, L) + cbx_ref[...])
    bc = _silu(_causal_conv(extbc, cwbc_ref, L) + cbbc_ref[...])
    extx[0:SUBLANES, :] = extx[L:L + SUBLANES, :]
    extbc[0:SUBLANES, :] = extbc[L:L + SUBLANES, :]

    dt = _softplus(sm_ref[...] + dtb_ref[...])
    da = dt * (-jnp.exp(alog_ref[...]))
    ri = lax.broadcasted_iota(jnp.int32, (L, L), 0)
    ci = lax.broadcasted_iota(jnp.int32, (L, L), 1)
    tril = ri >= ci
    cs = _mm_hi(tril.astype(F32), da)
    er = lax.broadcasted_iota(jnp.int32, (SMALL_W, BRANCH), 0)
    ec = lax.broadcasted_iota(jnp.int32, (SMALL_W, BRANCH), 1)
    expand = (er == (ec >> 6)).astype(F32)
    cs_full = _mm_hi(cs, expand)
    dt_full = _mm_hi(dt, expand)
    ecs_full = jnp.exp(cs_full)
    cs_last = cs_full[L - 1:L, :]
    w_full = jnp.exp(cs_last - cs_full)
    xdt = xs * dt_full
    xdtw = xdt * w_full
    cs_t = cs.T

    b128 = bc[:, 0:LANES]
    c128 = bc[:, LANES:2 * LANES]
    lane = lax.broadcasted_iota(jnp.int32, (1, LANES), 1)
    cg = [jnp.where(lane < SSM_STATE, c128, 0.0), jnp.where(lane >= SSM_STATE, c128, 0.0)]
    cb = [_mm_nt(cg[g], b128) for g in range(SSM_GROUPS)]

    y_pairs = []
    for p in range(SSM_HEADS // 2):
        xp = xdt[:, p * LANES:(p + 1) * LANES]
        yp = None
        for hh in range(2):
            h = 2 * p + hh
            g = h // (SSM_HEADS // SSM_GROUPS)
            col = cs[:, h:h + 1]
            row = cs_t[h:h + 1, :]
            seg = jnp.exp(jnp.where(tril, col - row, NEG_INF))
            att = cb[g] * seg
            hm = (lane < SSM_HEAD_DIM) if hh == 0 else (lane >= SSM_HEAD_DIM)
            term = _mm(att, jnp.where(hm, xp, 0.0))
            yp = term if yp is None else yp + term
        y_pairs.append(yp)
    y_diag = jnp.concatenate(y_pairs, axis=1)

    y_offs = []
    for g in range(SSM_GROUPS):
        sl = slice(g * 256, (g + 1) * 256)
        s_in = state[g]
        y_offs.append(_mm(cg[g], s_in) * ecs_full[:, sl])
        new = _mm_tn(b128, xdtw[:, sl])
        state[g] = s_in * ecs_full[L - 1:L, sl] + new
    y = y_diag + jnp.concatenate(y_offs, axis=1) + xs * dfull_ref[...]
    y = y * _silu(z_ref[...])
    o_ref[...] = _rms_rows(y, nw_ref[...]).astype(BF16)


def _ssd_call(proj, p, bsz, seq):
    nt = seq // SSD_L
    row = lambda blk: pl.BlockSpec((SSD_L, 512), lambda b, t: (b * nt + t, blk))
    full = lambda shape: pl.BlockSpec(shape, lambda b, t: (0,) * len(shape))
    return pl.pallas_call(
        _ssd_kernel,
        grid=(bsz, nt),
        in_specs=[row(U512["ssm_z"]), row(U512["ssm_x"]),
                  pl.BlockSpec((SSD_L, 256), lambda b, t: (b * nt + t, U256_BC)),
                  pl.BlockSpec((SSD_L, 256), lambda b, t: (b * nt + t, U256_SMALL)),
                  full((CONV_W, 512)), full((1, 512)), full((CONV_W, 256)), full((1, 256)),
                  full((1, SMALL_W)), full((1, SMALL_W)), full((1, 512)), full((1, 512))],
        out_specs=pl.BlockSpec((SSD_L, 512), lambda b, t: (b * nt + t, 0)),
        out_shape=jax.ShapeDtypeStruct((bsz * seq, BRANCH), BF16),
        scratch_shapes=[pltpu.VMEM((SUBLANES + SSD_L, 512), F32),
                        pltpu.VMEM((SUBLANES + SSD_L, 256), F32),
                        pltpu.VMEM((SSM_GROUPS, LANES, 256), F32)],
        compiler_params=_params(("arbitrary", "arbitrary")),
    )(proj, proj, proj, proj, p["cwx"], p["cbx"], p["cwbc"], p["cbbc"], p["dtb"], p["alog"],
      p["dfull"], p["nw"])


GDN_L = 64


def _gdn_kernel(q_ref, k_ref, v_ref, z_ref, sm_ref, cwq_ref, cwk_ref, cwv_ref, dtb_ref, alog_ref,
                nw_ref, o_ref, ext, state, *, bsz):
    L = GDN_L
    H = GDN_HEADS
    t = pl.program_id(0)

    @pl.when(t == 0)
    def _():
        ext[:, :, 0:SUBLANES, :] = jnp.zeros((3, bsz, SUBLANES, BRANCH), F32)
        state[...] = jnp.zeros_like(state)

    ri = lax.broadcasted_iota(jnp.int32, (L, L), 0)
    ci = lax.broadcasted_iota(jnp.int32, (L, L), 1)
    incl = ri >= ci
    strict = ri > ci
    eye = (ri == ci).astype(F32)
    bx = (ri >> 3) ^ (ci >> 3)
    blk = (bx > 0).astype(jnp.int32) + (bx > 1).astype(jnp.int32) + (bx > 3).astype(jnp.int32)
    tri = incl.astype(F32)
    nw = nw_ref[...]
    chains = [(b, h) for b in range(bsz) for h in range(H)]

    qkv, beta_all, gc_all, gc_t = [], [], [], []
    for b in range(bsz):
        outs = []
        for i, (r, w) in enumerate(((q_ref, cwq_ref), (k_ref, cwk_ref), (v_ref, cwv_ref))):
            e = ext.at[i, b]
            e[SUBLANES:SUBLANES + L, :] = r[b]
            outs.append(_silu(_causal_conv(e, w, L)))
            e[0:SUBLANES, :] = e[L:L + SUBLANES, :]
        qkv.append(outs)
        sm = sm_ref[b]
        beta_all.append(_sigmoid(sm))
        g_all = -jnp.exp(alog_ref[...]) * _softplus(sm + dtb_ref[...])
        gc = _mm_hi(tri, g_all)
        gc_all.append(gc)
        gc_t.append(gc.T)

    qs, ks, kbs, rhss, decays, egcs, glasts, gcbs = [], [], [], [], [], [], [], []
    for b, h in chains:
        sl = slice(h * GDN_DIM, (h + 1) * GDN_DIM)
        qh, kh, vh = (a[:, sl] for a in qkv[b])
        qh = qh * lax.rsqrt(jnp.sum(qh * qh, axis=-1, keepdims=True) + EPS) * (GDN_DIM ** -0.5)
        kh = kh * lax.rsqrt(jnp.sum(kh * kh, axis=-1, keepdims=True) + EPS)
        beta = beta_all[b][:, SM_BETA + h:SM_BETA + h + 1]
        gcol = gc_all[b][:, SM_DECAY + h:SM_DECAY + h + 1]
        grow = gc_t[b][SM_DECAY + h:SM_DECAY + h + 1, :]
        decays.append(jnp.exp(jnp.where(incl, gcol - grow, NEG_INF)))
        gcb = jnp.broadcast_to(gcol, (L, GDN_DIM))
        egc = jnp.exp(gcb)
        kb = kh * beta
        qs.append(qh); ks.append(kh); kbs.append(kb); gcbs.append(gcb); egcs.append(egc)
        glasts.append(gcb[L - 1:L, :])
        rhss.append(jnp.concatenate([vh * beta, kb * egc], axis=1))

    n = len(chains)
    rng = range(n)
    kk = [_mm_nt(kbs[c], ks[c]) for c in rng]
    qk = [_mm_nt(qs[c], ks[c]) for c in rng]
    ms = [jnp.where(strict, kk[c] * decays[c], 0.0) for c in rng]
    mds = [jnp.where(blk == 0, ms[c], 0.0) for c in rng]
    p2 = [_mm(mds[c], mds[c]) for c in rng]
    base = [eye - mds[c] for c in rng]
    bp = [_mm(base[c], p2[c]) for c in rng]
    p4 = [_mm(p2[c], p2[c]) for c in rng]
    base = [base[c] + bp[c] for c in rng]
    bq = [_mm(base[c], p4[c]) for c in rng]
    inv = [base[c] + bq[c] for c in rng]
    for lvl in range(1, int(math.log2(L // SUBLANES)) + 1):
        oi = [_mm(jnp.where(blk == lvl, ms[c], 0.0), inv[c]) for c in rng]
        ioi = [_mm(inv[c], oi[c]) for c in rng]
        inv = [inv[c] - ioi[c] for c in rng]
    sol = [_mm(inv[c], rhss[c]) for c in rng]
    s_in = [state[c] for c in rng]
    ws = [_mm(sol[c][:, GDN_DIM:], s_in[c]) for c in rng]
    qgs = [_mm(qs[c] * egcs[c], s_in[c]) for c in rng]
    v_new = [sol[c][:, :GDN_DIM] - ws[c] for c in rng]
    av = [_mm(qk[c] * decays[c], v_new[c]) for c in rng]
    kv = [_mm_tn(ks[c] * jnp.exp(glasts[c] - gcbs[c]), v_new[c]) for c in rng]
    for c in rng:
        state[c] = s_in[c] * jnp.exp(glasts[c]) + kv[c]
    for b in range(bsz):
        z = z_ref[b]
        outs = []
        for h in range(H):
            c = b * H + h
            sl = slice(h * GDN_DIM, (h + 1) * GDN_DIM)
            outs.append(_rms_rows(qgs[c] + av[c], nw[:, sl]) * _silu(z[:, sl]))
        o_ref[b] = jnp.concatenate(outs, axis=1).astype(BF16)


def _gdn_call(proj3, p, bsz, seq):
    nt = seq // GDN_L
    row = lambda blk, w=512: pl.BlockSpec((bsz, GDN_L, w), lambda t: (0, t, blk))
    full = lambda shape: pl.BlockSpec(shape, lambda t: (0,) * len(shape))
    return pl.pallas_call(
        functools.partial(_gdn_kernel, bsz=bsz),
        grid=(nt,),
        in_specs=[row(U512["gdn_q"]), row(U512["gdn_k"]), row(U512["gdn_v"]), row(U512["gdn_z"]),
                  row(U256_SMALL, 256),
                  full((CONV_W, 512)), full((CONV_W, 512)), full((CONV_W, 512)),
                  full((1, SMALL_W)), full((1, SMALL_W)), full((1, 512))],
        out_specs=pl.BlockSpec((bsz, GDN_L, 512), lambda t: (0, t, 0)),
        out_shape=jax.ShapeDtypeStruct((bsz, seq, BRANCH), BF16),
        scratch_shapes=[pltpu.VMEM((3, bsz, SUBLANES + GDN_L, 512), F32),
                        pltpu.VMEM((bsz * GDN_HEADS, GDN_DIM, GDN_DIM), F32)],
        compiler_params=_params(("arbitrary",)),
    )(proj3, proj3, proj3, proj3, proj3, p["cwq"], p["cwk"], p["cwv"], p["dtb"], p["alog"], p["nw"])


GLA_T = 128
GLA_QK = GLA_HEADS * GLA_DK


def _gla_kernel(qk_ref, v_ref, r_ref, sm_ref, wg_ref, bg_ref, nw_ref, o_ref, state, pbuf, abuf):
    T = GLA_T
    C = GLA_CHUNK
    t = pl.program_id(1)

    @pl.when(t == 0)
    def _():
        state[...] = jnp.zeros_like(state)

    qk = qk_ref[...]
    q = qk[:, :GLA_QK] * (GLA_DK ** -0.5)
    k = qk[:, GLA_QK:]
    v = v_ref[...]
    pre = _mm(sm_ref[...], wg_ref[...]) + bg_ref[...]
    log_a = (jnp.minimum(pre, 0.0) - jnp.log1p(jnp.exp(-jnp.abs(pre)))) / GLA_TAU
    ri = lax.broadcasted_iota(jnp.int32, (T, T), 0)
    ci = lax.broadcasted_iota(jnp.int32, (T, T), 1)
    blocktri = ((ri >> 4) == (ci >> 4)) & (ri >= ci)
    G = _mm_hi(blocktri.astype(F32), log_a)

    rr = lax.broadcasted_iota(jnp.int32, (GLA_QK, BRANCH), 0)
    rc = lax.broadcasted_iota(jnp.int32, (GLA_QK, BRANCH), 1)
    red = ((rr >> 6) == (rc >> 7)).astype(BF16)
    rmod = lax.broadcasted_iota(jnp.int32, (T, GLA_QK), 0) & (C - 1)
    nc = T // C

    def chunk_row(x, jl):
        w = x.shape[1]
        x3 = x.reshape(nc, C, w)
        return jnp.broadcast_to(x3[:, jl:jl + 1, :], (nc, C, w)).reshape(T, w)

    for jl in range(C):
        ks = chunk_row(k, jl)
        gs = chunk_row(G, jl)
        qd = jnp.where(rmod >= jl, q, 0.0)
        pbuf[jl * T:(jl + 1) * T, :] = (qd * ks * jnp.exp(jnp.minimum(G - gs, 0.0))).astype(BF16)
    abuf[...] = jnp.dot(pbuf[...], red, preferred_element_type=F32)
    o_chunks = []
    for c in range(nc):
        oc = None
        for jl in range(C):
            a = abuf[jl * T + c * C:jl * T + (c + 1) * C, :]
            term = a * v_ref[c * C + jl:c * C + jl + 1, :]
            oc = term if oc is None else oc + term
        o_chunks.append(oc)
    o = jnp.concatenate(o_chunks, axis=0)

    head_of_lane = lax.broadcasted_iota(jnp.int32, (C, GLA_QK), 1) >> 6
    qgs, decs, upds = [], [], []
    for c in range(nc):
        rows = slice(c * C, (c + 1) * C)
        gc = G[rows]
        glast = gc[C - 1:C, :]
        qgs.append(q[rows] * jnp.exp(gc))
        decs.append(jnp.exp(glast))
        upds.append(_mm_tn(v[rows], k[rows] * jnp.exp(glast - gc)))
    st = state[...]
    st_in = []
    for c in range(nc):
        st_in.append(st)
        st = st * decs[c] + upds[c]
    state[...] = st
    inter = [jnp.concatenate(
        [_mm_nt(jnp.where(head_of_lane == h, qgs[c], 0.0), st_in[c][h * GLA_DV:(h + 1) * GLA_DV, :])
         for h in range(GLA_HEADS)], axis=1) for c in range(nc)]
    o = o + jnp.concatenate(inter, axis=0)
    r = r_ref[...]
    nw = nw_ref[...]
    outs = []
    for h in range(GLA_HEADS):
        sl = slice(h * GLA_DV, (h + 1) * GLA_DV)
        outs.append(_rms_rows(o[:, sl], nw[:, sl]) * _silu(r[:, sl]))
    o_ref[...] = jnp.concatenate(outs, axis=1).astype(BF16)


def _gla_call(proj, p, bsz, seq):
    nt = seq // GLA_T
    row = lambda blk: pl.BlockSpec((GLA_T, 512), lambda b, t: (b * nt + t, blk))
    full = lambda shape: pl.BlockSpec(shape, lambda b, t: (0,) * len(shape))
    return pl.pallas_call(
        _gla_kernel,
        grid=(bsz, nt),
        in_specs=[row(U512["gla_qk"]), row(U512["gla_v"]), row(U512["gla_r"]),
                  pl.BlockSpec((GLA_T, 256), lambda b, t: (b * nt + t, U256_SMALL)),
                  full((SMALL_W, GLA_QK)), full((1, GLA_QK)), full((1, 512))],
        out_specs=pl.BlockSpec((GLA_T, 512), lambda b, t: (b * nt + t, 0)),
        out_shape=jax.ShapeDtypeStruct((bsz * seq, BRANCH), BF16),
        scratch_shapes=[pltpu.VMEM((BRANCH, GLA_QK), F32),
                        pltpu.VMEM((GLA_CHUNK * GLA_T, GLA_QK), BF16),
                        pltpu.VMEM((GLA_CHUNK * GLA_T, BRANCH), F32)],
        compiler_params=_params(("arbitrary", "arbitrary")),
    )(proj, proj, proj, proj, p["wg"], p["bg"], p["nw"])


LRU_T = 256
LRU_PAD = LRU_T // 2


def _lru_kernel(x_ref, gate_ref, cw_ref, cb_ref, wa_ref, ba_ref, wx_ref, bx_ref, lam_ref, o_ref,
                ext, abuf, hbuf, carry):
    T = LRU_T
    P = LRU_PAD
    t = pl.program_id(1)

    @pl.when(t == 0)
    def _():
        ext[0:SUBLANES, :] = jnp.zeros((SUBLANES, BRANCH), F32)
        abuf[0:P, :] = jnp.ones((P, BRANCH), F32)
        hbuf[0:P, :] = jnp.zeros((P, BRANCH), F32)
        carry[...] = jnp.zeros_like(carry)

    ext[SUBLANES:SUBLANES + T, :] = x_ref[...]
    xc = _causal_conv(ext, cw_ref, T) + cb_ref[...]
    ext[0:SUBLANES, :] = ext[T:T + SUBLANES, :]
    gate_r = _sigmoid(_mm(xc, wa_ref[...]) + ba_ref[...])
    gate_i = _sigmoid(_mm(xc, wx_ref[...]) + bx_ref[...])
    log_a = -LRU_C * gate_r * _softplus(-lam_ref[...])
    abuf[P:P + T, :] = jnp.exp(log_a)
    th = jnp.tanh(log_a)
    hbuf[P:P + T, :] = jnp.sqrt(-2.0 * th / (1.0 - th)) * (gate_i * xc)
    s = 1
    while s < T:
        a_cur = abuf[P:P + T, :]
        h_cur = hbuf[P:P + T, :]
        a_sh = abuf[P - s:P - s + T, :]
        h_sh = hbuf[P - s:P - s + T, :]
        hbuf[P:P + T, :] = h_cur + a_cur * h_sh
        abuf[P:P + T, :] = a_cur * a_sh
        s *= 2
    h = hbuf[P:P + T, :] + abuf[P:P + T, :] * carry[0:1, :]
    carry[0:1, :] = h[T - 1:T, :]
    o_ref[...] = (h * _gelu(gate_ref[...])).astype(BF16)


def _lru_call(proj, p, bsz, seq):
    nt = seq // LRU_T
    row = lambda blk: pl.BlockSpec((LRU_T, 512), lambda b, t: (b * nt + t, blk))
    full = lambda shape: pl.BlockSpec(shape, lambda b, t: (0,) * len(shape))
    return pl.pallas_call(
        _lru_kernel,
        grid=(bsz, nt),
        in_specs=[row(U512["lru_x"]), row(U512["lru_gate"]),
                  full((CONV_W, 512)), full((1, 512)), full((512, 512)), full((1, 512)),
                  full((512, 512)), full((1, 512)), full((1, 512))],
        out_specs=pl.BlockSpec((LRU_T, 512), lambda b, t: (b * nt + t, 0)),
        out_shape=jax.ShapeDtypeStruct((bsz * seq, BRANCH), BF16),
        scratch_shapes=[pltpu.VMEM((SUBLANES + LRU_T, 512), F32),
                        pltpu.VMEM((LRU_PAD + LRU_T, 512), F32),
                        pltpu.VMEM((LRU_PAD + LRU_T, 512), F32),
                        pltpu.VMEM((SUBLANES, 512), F32)],
        compiler_params=_params(("arbitrary", "arbitrary")),
    )(proj, proj, p["cw"], p["cb"], p["wa"], p["ba"], p["wx"], p["bx"], p["lam"])


MRG_TM = 256


def _merge_kernel(x_ref, g_ref, lg_ref, y0_ref, y1_ref, y2_ref, y3_ref, wb_ref, wo_ref, o_ref):
    merged = None
    for i, y_ref in enumerate((y0_ref, y1_ref, y2_ref, y3_ref)):
        br = jnp.dot(y_ref[...], wb_ref[i], preferred_element_type=F32)
        term = _sigmoid(lg_ref[:, i * D_MODEL:(i + 1) * D_MODEL]) * br
        merged = term if merged is None else merged + term
    out = jnp.dot(merged.astype(BF16), wo_ref[...], preferred_element_type=F32)
    o_ref[...] = x_ref[...] + g_ref[0] * out


def _merge_call(x2, g1, proj, ys, wb, wo, seq):
    n_tok = x2.shape[0]
    per_b = seq // MRG_TM
    yspec = pl.BlockSpec((MRG_TM, BRANCH), lambda i: (i, 0))
    return pl.pallas_call(
        _merge_kernel,
        grid=(n_tok // MRG_TM,),
        in_specs=[pl.BlockSpec((MRG_TM, D_MODEL), lambda i: (i, 0)),
                  pl.BlockSpec((1, 1, D_MODEL), lambda i: (i // per_b, 0, 0)),
                  pl.BlockSpec((MRG_TM, N_BRANCH * D_MODEL), lambda i: (i, 0)),
                  yspec, yspec, yspec, yspec,
                  pl.BlockSpec((N_BRANCH, BRANCH, D_MODEL), lambda i: (0, 0, 0)),
                  pl.BlockSpec((D_MODEL, D_MODEL), lambda i: (0, 0))],
        out_specs=pl.BlockSpec((MRG_TM, D_MODEL), lambda i: (i, 0)),
        out_shape=jax.ShapeDtypeStruct((n_tok, D_MODEL), F32),
        compiler_params=_params(("arbitrary",)),
    )(x2, g1, proj, *ys, wb, wo)


PS_TB = 512
N_SCORE_ROWS = 2 * PEER_HEADS * PEER_KEYS


def _peer_score_kernel(x_ref, nw_ref, sc_ref, sh_ref, wqt_ref, keys_ref, h2t_ref, st_ref):
    h = _rms_rows(x_ref[...], nw_ref[...])
    h = h * (1.0 + sc_ref[0]) + sh_ref[0]
    ht = h.T.astype(BF16)
    h2t_ref[...] = ht
    qt = jnp.dot(wqt_ref[...], ht, preferred_element_type=F32).astype(BF16)
    for g in range(2 * PEER_HEADS):
        rows = slice(g * PEER_KEYS, (g + 1) * PEER_KEYS)
        st_ref[rows, :] = jnp.dot(keys_ref[g], qt[rows, :], preferred_element_type=F32)


def _peer_score_call(x2, nw, sc, sh, wqt, keys, seq):
    n_tok = x2.shape[0]
    per_b = seq // PS_TB
    return pl.pallas_call(
        _peer_score_kernel,
        grid=(n_tok // PS_TB,),
        in_specs=[pl.BlockSpec((PS_TB, D_MODEL), lambda i: (i, 0)),
                  pl.BlockSpec((1, D_MODEL), lambda i: (0, 0)),
                  pl.BlockSpec((1, 1, D_MODEL), lambda i: (i // per_b, 0, 0)),
                  pl.BlockSpec((1, 1, D_MODEL), lambda i: (i // per_b, 0, 0)),
                  pl.BlockSpec((N_SCORE_ROWS, D_MODEL), lambda i: (0, 0)),
                  pl.BlockSpec((2 * PEER_HEADS, PEER_KEYS, PEER_HALF), lambda i: (0, 0, 0))],
        out_specs=[pl.BlockSpec((D_MODEL, PS_TB), lambda i: (0, i)),
                   pl.BlockSpec((N_SCORE_ROWS, PS_TB), lambda i: (0, i))],
        out_shape=[jax.ShapeDtypeStruct((D_MODEL, n_tok), BF16),
                   jax.ShapeDtypeStruct((N_SCORE_ROWS, n_tok), F32)],
        compiler_params=_params(("arbitrary",)),
    )(x2, nw, sc, sh, wqt, keys)


PT_TL = 128
NOT_RANKED = 255.0
_CAND = [(i, j) for i in range(PEER_TOPK) for j in range(PEER_TOPK) if (i + 1) * (j + 1) <= PEER_TOPK]
N_CAND_ROWS = -(-len(_CAND) // SUBLANES) * SUBLANES


def _pop_max(x, iota):
    m = jnp.max(x, axis=0, keepdims=True)
    first = jnp.min(jnp.where(x == m, iota, float(x.shape[0])), axis=0, keepdims=True)
    return m, first, jnp.where(iota == first, NEG_INF, x)


def _gate_tables_head(st_ref, ce_ref, r2_ref, e2_ref, cand, h, exact_ties):
    TL = PT_TL
    iota_k = lax.broadcasted_iota(jnp.int32, (PEER_KEYS, TL), 0).astype(F32)
    iota_c = lax.broadcasted_iota(jnp.int32, (N_CAND_ROWS, TL), 0).astype(F32)
    rows1 = slice(h * PEER_KEYS, (h + 1) * PEER_KEYS)
    rows2 = slice((PEER_HEADS + h) * PEER_KEYS, (PEER_HEADS + h + 1) * PEER_KEYS)
    s1 = st_ref[rows1, :]
    s2 = st_ref[rows2, :]
    x = s1
    t1, pick1 = [], []
    for _ in range(PEER_TOPK):
        if exact_ties:
            m, f, x = _pop_max(x, iota_k)
            pick1.append(f)
        else:
            m = jnp.max(x, axis=0, keepdims=True)
            x = jnp.where(x == m, NEG_INF, x)
        t1.append(m)
    x = s2
    t2 = []
    rank2 = jnp.full((PEER_KEYS, TL), NOT_RANKED, F32)
    for r in range(PEER_TOPK):
        if exact_ties:
            m, f, x = _pop_max(x, iota_k)
            hit_rows = iota_k == f
        else:
            m = jnp.max(x, axis=0, keepdims=True)
            hit_rows = x == m
            x = jnp.where(hit_rows, NEG_INF, x)
        t2.append(m)
        rank2 = jnp.where(hit_rows, float(r), rank2)
    for n, (i, j) in enumerate(_CAND):
        cand[n:n + 1, :] = t1[i] + t2[j]
    c = cand[...]
    x = c
    tau = None
    for _ in range(PEER_TOPK):
        tau, _, x = _pop_max(x, iota_c)
    m1, m2 = t1[0], t2[0]
    zsum = jnp.sum(jnp.where(c >= tau, jnp.exp(c - (m1 + m2)), 0.0), axis=0, keepdims=True)
    count1 = jnp.zeros((PEER_KEYS, TL), F32)
    for i in range(PEER_TOPK):
        cnt = None
        for j in range(PEER_TOPK):
            if (i + 1) * (j + 1) <= PEER_TOPK:
                hit = jnp.where(t1[i] + t2[j] >= tau, 1.0, 0.0)
                cnt = hit if cnt is None else cnt + hit
        sel = (iota_k == pick1[i]) if exact_ties else (s1 == t1[i])
        count1 = jnp.where(sel, cnt, count1)
    ce_ref[rows1, :] = count1
    ce_ref[rows2, :] = jnp.exp(s1 - m1) * (0.5 / zsum)
    r2_ref[rows1, :] = rank2.astype(BF16)
    e2_ref[rows1, :] = jnp.exp(s2 - m2).astype(BF16)
    if exact_ties:
        return None
    n1 = jnp.sum(jnp.where(s1 >= t1[-1], 1.0, 0.0), axis=0, keepdims=True)
    n2 = jnp.sum(jnp.where(s2 >= t2[-1], 1.0, 0.0), axis=0, keepdims=True)
    return jnp.abs(n1 - PEER_TOPK) + jnp.abs(n2 - PEER_TOPK)


def _peer_gate_kernel(st_ref, ce_ref, r2_ref, e2_ref, cand):
    cand[...] = jnp.full((N_CAND_ROWS, PT_TL), NEG_INF, F32)
    tied = None
    for h in range(PEER_HEADS):
        t = _gate_tables_head(st_ref, ce_ref, r2_ref, e2_ref, cand, h, exact_ties=False)
        tied = t if tied is None else tied + t

    @pl.when(jnp.max(tied) > 0.0)
    def _():
        for h in range(PEER_HEADS):
            _gate_tables_head(st_ref, ce_ref, r2_ref, e2_ref, cand, h, exact_ties=True)


def _peer_gate_call(scores_t):
    n_tok = scores_t.shape[1]
    half = PEER_HEADS * PEER_KEYS
    return pl.pallas_call(
        _peer_gate_kernel,
        grid=(n_tok // PT_TL,),
        in_specs=[pl.BlockSpec((N_SCORE_ROWS, PT_TL), lambda i: (0, i))],
        out_specs=[pl.BlockSpec((N_SCORE_ROWS, PT_TL), lambda i: (0, i)),
                   pl.BlockSpec((half, PT_TL), lambda i: (0, i)),
                   pl.BlockSpec((half, PT_TL), lambda i: (0, i))],
        out_shape=[jax.ShapeDtypeStruct((N_SCORE_ROWS, n_tok), F32),
                   jax.ShapeDtypeStruct((half, n_tok), BF16),
                   jax.ShapeDtypeStruct((half, n_tok), BF16)],
        scratch_shapes=[pltpu.VMEM((N_CAND_ROWS, PT_TL), F32)],
        compiler_params=_params(("arbitrary",)),
    )(scores_t)


PE_TB = 512
PE_EB = 2048
HALF_ROWS = PEER_HEADS * PEER_KEYS


def _bcast_rows_bf16(row):
    r16 = jnp.broadcast_to(row, (2 * SUBLANES, LANES)).astype(BF16)
    return jnp.concatenate([r16] * (PEER_KEYS // (2 * SUBLANES)), axis=0)


def _peer_expert_kernel(h2t_ref, u_ref, vt_ref, ce_ref, r2_ref, e2_ref, x_ref, g_ref, fw_ref,
                        o_ref, acc, zt, *, final_norm):
    j = pl.program_id(1)
    H = PEER_HEADS

    @pl.when(j == 0)
    def _():
        acc[...] = jnp.zeros_like(acc)

    c0 = math.sqrt(2.0 / math.pi)
    zero = jnp.zeros((PEER_KEYS, LANES), BF16)
    st = jnp.dot(u_ref[0], h2t_ref[...], preferred_element_type=F32)
    for q in range(PE_EB // PEER_KEYS):
        rs = slice(q * PEER_KEYS, (q + 1) * PEER_KEYS)
        for lc in range(PE_TB // LANES):
            ls = slice(lc * LANES, (lc + 1) * LANES)
            w = None
            for h in range(H):
                rows = slice(h * PEER_KEYS, (h + 1) * PEER_KEYS)
                cnt = _bcast_rows_bf16(ce_ref[0, h, q:q + 1, ls])
                g1 = _bcast_rows_bf16(ce_ref[1, h, q:q + 1, ls])
                sel_g1 = jnp.minimum(jnp.maximum(cnt - r2_ref[rows, ls], zero), g1)
                term = sel_g1 * e2_ref[rows, ls]
                w = term if w is None else w + term
            x = st[rs, ls]
            th = jnp.tanh(x * (c0 + (c0 * 0.044715) * (x * x)))
            zt[rs, ls] = (x + x * th).astype(BF16) * w
    acc[...] += jnp.dot(vt_ref[0], zt[...], preferred_element_type=F32)

    @pl.when(j == pl.num_programs(1) - 1)
    def _():
        xn = x_ref[...] + g_ref[0] * acc[...].T
        if final_norm:
            xn = _rms_rows(xn, fw_ref[...])
        o_ref[...] = xn


def _peer_expert_call(h2, u_bf, vt_bf, layer, ce, r2, e2, x2, g2, fw, seq, final_norm):
    n_tok = x2.shape[0]
    per_b = seq // PE_TB
    kern = functools.partial(_peer_expert_kernel, final_norm=final_norm)
    ce_k = ce.reshape(2, PEER_HEADS, PEER_KEYS, n_tok)
    return pl.pallas_call(
        kern,
        grid=(n_tok // PE_TB, PEER_EXPERTS // PE_EB),
        in_specs=[pl.BlockSpec((D_MODEL, PE_TB), lambda i, j: (0, i)),
                  pl.BlockSpec((1, PE_EB, D_MODEL), lambda i, j: (layer, j, 0)),
                  pl.BlockSpec((1, D_MODEL, PE_EB), lambda i, j: (layer, 0, j)),
                  pl.BlockSpec((2, PEER_HEADS, PE_EB // PEER_KEYS, PE_TB), lambda i, j: (0, 0, j, i)),
                  pl.BlockSpec((HALF_ROWS, PE_TB), lambda i, j: (0, i)),
                  pl.BlockSpec((HALF_ROWS, PE_TB), lambda i, j: (0, i)),
                  pl.BlockSpec((PE_TB, D_MODEL), lambda i, j: (i, 0)),
                  pl.BlockSpec((1, 1, D_MODEL), lambda i, j: (i // per_b, 0, 0)),
                  pl.BlockSpec((1, D_MODEL), lambda i, j: (0, 0))],
        out_specs=pl.BlockSpec((PE_TB, D_MODEL), lambda i, j: (i, 0)),
        out_shape=jax.ShapeDtypeStruct((n_tok, D_MODEL), F32),
        scratch_shapes=[pltpu.VMEM((D_MODEL, PE_TB), F32),
                        pltpu.VMEM((PE_EB, PE_TB), BF16)],
        compiler_params=_params(("arbitrary", "arbitrary")),
    )(h2, u_bf, vt_bf, ce_k, r2, e2, x2, g2, fw)


def _pad_lanes(vec, start, width=SMALL_W):
    out = jnp.zeros((1, width), F32)
    return lax.dynamic_update_slice(out, vec.reshape(1, -1).astype(F32), (0, start))


def _block_diag(w):
    n, d, e = w.shape
    eye = jnp.eye(n, dtype=w.dtype)
    return (eye[:, None, :, None] * w[:, :, None, :]).reshape(n * d, n * e)


def kernel(x, c, w_ada, b_ada, norm_mix_w, norm_ffn_w, w_in, ssm_conv_w, ssm_conv_b, ssm_dt_bias,
           ssm_a_log, ssm_d, ssm_norm_w, gdn_conv_w, gdn_a_log, gdn_dt_bias, gdn_norm_w, gla_w_gate,
           gla_b_gate, gla_norm_w, lru_conv_w, lru_conv_b, lru_w_a, lru_b_a, lru_w_x, lru_b_x,
           lru_lambda, w_branch, w_out, peer_w_q, peer_sub_keys, peer_u, peer_v, final_norm_w):
    bsz, seq, d = x.shape
    n_layers = w_in.shape[0]
    n_tok = bsz * seq
    x2 = x.reshape(n_tok, d)

    c_pad = jnp.zeros((SUBLANES, d), F32).at[:bsz].set(c)
    mod = _ada_call(c_pad, w_ada, b_ada)

    row1 = lambda v: v.reshape(1, -1).astype(F32)
    w_perm = _permute_w_in(w_in)
    u_bf = peer_u.astype(BF16)
    vt_bf = jnp.swapaxes(peer_v, 1, 2).astype(BF16)

    for l in range(n_layers):
        m6 = mod[l, :bsz].reshape(bsz, 6, 1, d)
        sh1, sc1, g1, sh2, sc2, g2 = (m6[:, i] for i in range(6))
        proj = _inproj_call(x2, row1(norm_mix_w[l]), sc1, sh1, w_perm, l, seq)

        ssd_p = dict(cwx=ssm_conv_w[l][:, :512], cbx=row1(ssm_conv_b[l][:512]),
                     cwbc=ssm_conv_w[l][:, 512:], cbbc=row1(ssm_conv_b[l][512:]),
                     dtb=_pad_lanes(ssm_dt_bias[l], SM_DT), alog=_pad_lanes(ssm_a_log[l], SM_DT),
                     dfull=row1(jnp.repeat(ssm_d[l], SSM_HEAD_DIM)), nw=row1(ssm_norm_w[l]))
        y_ssd = _ssd_call(proj, ssd_p, bsz, seq)

        gdn_p = dict(cwq=gdn_conv_w[l][:, :512], cwk=gdn_conv_w[l][:, 512:1024],
                     cwv=gdn_conv_w[l][:, 1024:], dtb=_pad_lanes(gdn_dt_bias[l], SM_DECAY),
                     alog=_pad_lanes(gdn_a_log[l], SM_DECAY),
                     nw=row1(jnp.tile(gdn_norm_w[l], GDN_HEADS)))
        y_gdn = _gdn_call(proj.reshape(bsz, seq, N_COLS), gdn_p, bsz, seq).reshape(n_tok, BRANCH)

        wg = jnp.zeros((SMALL_W, GLA_QK), F32).at[SM_LOW:SM_LOW + GLA_RANK].set(gla_w_gate[l])
        gla_p = dict(wg=wg.astype(BF16), bg=row1(gla_b_gate[l]),
                     nw=row1(jnp.tile(gla_norm_w[l], GLA_HEADS)))
        y_gla = _gla_call(proj, gla_p, bsz, seq)

        lru_p = dict(cw=lru_conv_w[l], cb=row1(lru_conv_b[l]),
                     wa=_block_diag(lru_w_a[l]).astype(BF16), ba=row1(lru_b_a[l]),
                     wx=_block_diag(lru_w_x[l]).astype(BF16), bx=row1(lru_b_x[l]),
                     lam=row1(lru_lambda[l]))
        y_lru = _lru_call(proj, lru_p, bsz, seq)

        x2 = _merge_call(x2, g1, proj, (y_ssd, y_gdn, y_gla, y_lru),
                         w_branch[l].astype(BF16), w_out[l].astype(BF16), seq)

        wqt = peer_w_q[l].reshape(d, PEER_HEADS, 2, PEER_HALF).transpose(2, 1, 3, 0)
        wqt = wqt.reshape(N_SCORE_ROWS, d).astype(BF16)
        keys = peer_sub_keys[l].transpose(1, 0, 2, 3).reshape(2 * PEER_HEADS, PEER_KEYS, PEER_HALF)
        h2, scores_t = _peer_score_call(x2, row1(norm_ffn_w[l]), sc2, sh2, wqt, keys.astype(BF16), seq)
        ce, r2, e2 = _peer_gate_call(scores_t)
        x2 = _peer_expert_call(h2, u_bf, vt_bf, l, ce, r2, e2,
                               x2, g2, row1(final_norm_w), seq, final_norm=(l == n_layers - 1))
    return x2.reshape(bsz, seq, d)
```

```python
import functools
import math

import jax
import jax.numpy as jnp
from jax import lax
from jax.experimental import pallas as pl
from jax.experimental.pallas import tpu as pltpu

F32 = jnp.float32
BF16 = jnp.bfloat16
HIGHEST = lax.Precision.HIGHEST
NEG_INF = float("-inf")

D_MODEL = 1024
N_LAYERS = 2
EPS = 1e-6
CONV_W = 4
BRANCH = 512
N_BRANCH = 4
SSM_HEADS = 8
SSM_HEAD_DIM = 64
SSM_GROUPS = 2
SSM_STATE = 64
GDN_HEADS = 4
GDN_DIM = 128
GLA_HEADS = 4
GLA_DK = 64
GLA_DV = 128
GLA_RANK = 16
GLA_TAU = 16.0
GLA_CHUNK = 16
LRU_BLOCKS = 8
LRU_BLOCK_DIM = 64
LRU_C = 8.0
PEER_HEADS = 8
PEER_KEYS = 128
PEER_EXPERTS = PEER_KEYS * PEER_KEYS
PEER_HALF = 128
PEER_TOPK = 16

LANES = 128
SUBLANES = 8
VMEM_LIMIT = 48 * 1024 * 1024

_SRC = {}
_off = 0
for _name, _w in (("ssm_z", 512), ("ssm_x", 512), ("ssm_b", 128), ("ssm_c", 128), ("ssm_dt", 8),
                  ("gdn_q", 512), ("gdn_k", 512), ("gdn_v", 512), ("gdn_z", 512), ("gdn_beta", 4),
                  ("gdn_decay", 4), ("gla_q", 256), ("gla_k", 256), ("gla_v", 512), ("gla_r", 512),
                  ("gla_low", 16), ("lru_x", 512), ("lru_gate", 512), ("merge", 4096)):
    _SRC[_name] = (_off, _w)
    _off += _w
D_IN = _off
_DST_ORDER = ("merge", "ssm_z", "ssm_x", "gdn_q", "gdn_k", "gdn_v", "gdn_z", "gla_q", "gla_k",
              "gla_v", "gla_r", "lru_x", "lru_gate", "ssm_b", "ssm_c", "ssm_dt", "gdn_beta",
              "gdn_decay", "gla_low")
SMALL_W = 256
N_COLS = 4096 + 11 * 512 + 256 + SMALL_W
U512 = {"ssm_z": 8, "ssm_x": 9, "gdn_q": 10, "gdn_k": 11, "gdn_v": 12, "gdn_z": 13, "gla_qk": 14,
        "gla_v": 15, "gla_r": 16, "lru_x": 17, "lru_gate": 18}
U256_BC = 38
U256_SMALL = 39
SM_DT = 0
SM_BETA = 8
SM_DECAY = 12
SM_LOW = 16


def _permute_w_in(w):
    w = w.astype(BF16)
    parts = [w[..., _SRC[name][0]:_SRC[name][0] + _SRC[name][1]] for name in _DST_ORDER]
    used = sum(_SRC[name][1] for name in _DST_ORDER)
    parts.append(jnp.zeros(w.shape[:-1] + (N_COLS - used,), w.dtype))
    return jnp.concatenate(parts, axis=-1)


def _mm(a, b):
    return jnp.dot(a.astype(BF16), b.astype(BF16), preferred_element_type=F32)


def _mm_nt(a, b):
    return lax.dot_general(a.astype(BF16), b.astype(BF16), (((1,), (1,)), ((), ())),
                           preferred_element_type=F32)


def _mm_tn(a, b):
    return lax.dot_general(a.astype(BF16), b.astype(BF16), (((0,), (0,)), ((), ())),
                           preferred_element_type=F32)


def _mm_hi(a, b):
    return jnp.dot(a, b, precision=HIGHEST, preferred_element_type=F32)


def _sigmoid(x):
    return 1.0 / (1.0 + jnp.exp(-x))


def _silu(x):
    return x * _sigmoid(x)


def _softplus(x):
    return jnp.maximum(x, 0.0) + jnp.log1p(jnp.exp(-jnp.abs(x)))


def _gelu(x):
    c = math.sqrt(2.0 / math.pi)
    return 0.5 * x * (1.0 + jnp.tanh(c * (x + 0.044715 * (x * x * x))))


def _rms_rows(x, w):
    ms = jnp.mean(x * x, axis=-1, keepdims=True)
    return x * lax.rsqrt(ms + EPS) * w


def _causal_conv(ext_ref, w_ref, n_rows):
    acc = None
    for k in range(CONV_W):
        term = w_ref[k:k + 1, :] * ext_ref[SUBLANES - CONV_W + 1 + k:SUBLANES - CONV_W + 1 + k + n_rows, :]
        acc = term if acc is None else acc + term
    return acc


def _params(sem, flags=None):
    return pltpu.CompilerParams(dimension_semantics=sem, vmem_limit_bytes=VMEM_LIMIT, flags=flags)


ADA_TN = 1536


def _ada_kernel(c_ref, w_ref, b_ref, o_ref):
    c = c_ref[...]
    o_ref[0] = _mm(_silu(c), w_ref[0]) + b_ref[0]


def _ada_call(c_pad, w_ada, b_ada):
    n_l = w_ada.shape[0]
    n_out = w_ada.shape[2]
    return pl.pallas_call(
        _ada_kernel,
        grid=(n_l, n_out // ADA_TN),
        in_specs=[pl.BlockSpec((SUBLANES, D_MODEL), lambda l, j: (0, 0)),
                  pl.BlockSpec((1, D_MODEL, ADA_TN), lambda l, j: (l, 0, j)),
                  pl.BlockSpec((1, 1, ADA_TN), lambda l, j: (l, 0, j))],
        out_specs=pl.BlockSpec((1, SUBLANES, ADA_TN), lambda l, j: (l, 0, j)),
        out_shape=jax.ShapeDtypeStruct((n_l, SUBLANES, n_out), F32),
        compiler_params=_params(("arbitrary", "arbitrary")),
    )(c_pad, w_ada, b_ada.reshape(n_l, 1, n_out))


INP_TM = 1024
INP_TN = 2048


def _inproj_kernel(x_ref, nw_ref, sc_ref, sh_ref, w_ref, o_ref, h_scr):
    @pl.when(pl.program_id(1) == 0)
    def _():
        h = _rms_rows(x_ref[...], nw_ref[...])
        h = h * (1.0 + sc_ref[0]) + sh_ref[0]
        h_scr[...] = h.astype(BF16)

    o_ref[...] = jnp.dot(h_scr[...], w_ref[0], preferred_element_type=F32)


def _inproj_call(x2, nw, sc, sh, w_perm, layer, seq):
    n_tok = x2.shape[0]
    per_b = seq // INP_TM
    return pl.pallas_call(
        _inproj_kernel,
        grid=(n_tok // INP_TM, N_COLS // INP_TN),
        in_specs=[pl.BlockSpec((INP_TM, D_MODEL), lambda i, j: (i, 0)),
                  pl.BlockSpec((1, D_MODEL), lambda i, j: (0, 0)),
                  pl.BlockSpec((1, 1, D_MODEL), lambda i, j: (i // per_b, 0, 0)),
                  pl.BlockSpec((1, 1, D_MODEL), lambda i, j: (i // per_b, 0, 0)),
                  pl.BlockSpec((1, D_MODEL, INP_TN), lambda i, j: (layer, 0, j))],
        out_specs=pl.BlockSpec((INP_TM, INP_TN), lambda i, j: (i, j)),
        out_shape=jax.ShapeDtypeStruct((n_tok, N_COLS), F32),
        scratch_shapes=[pltpu.VMEM((INP_TM, D_MODEL), BF16)],
        compiler_params=_params(("arbitrary", "arbitrary")),
    )(x2, nw, sc, sh, w_perm)


SSD_L = 128


def _ssd_kernel(z_ref, xs_ref, bc_ref, sm_ref, cwx_ref, cbx_ref, cwbc_ref, cbbc_ref, dtb_ref,
                alog_ref, dfull_ref, nw_ref, o_ref, extx, extbc, state):
    L = SSD_L
    t = pl.program_id(1)

    @pl.when(t == 0)
    def _():
        extx[0:SUBLANES, :] = jnp.zeros((SUBLANES, BRANCH), F32)
        extbc[0:SUBLANES, :] = jnp.zeros((SUBLANES, 256), F32)
        state[...] = jnp.zeros_like(state)

    extx[SUBLANES:SUBLANES + L, :] = xs_ref[...]
    extbc[SUBLANES:SUBLANES + L, :] = bc_ref[...]
    xs = _silu(_causal_conv(extx, cwx_ref, L) + cbx_ref[...])
    bc = _silu(_causal_conv(extbc, cwbc_ref, L) + cbbc_ref[...])
    extx[0:SUBLANES, :] = extx[L:L + SUBLANES, :]
    extbc[0:SUBLANES, :] = extbc[L:L + SUBLANES, :]

    dt = _softplus(sm_ref[...] + dtb_ref[...])
    da = dt * (-jnp.exp(alog_ref[...]))
    ri = lax.broadcasted_iota(jnp.int32, (L, L), 0)
    ci = lax.broadcasted_iota(jnp.int32, (L, L), 1)
    tril = ri >= ci
    cs = _mm_hi(tril.astype(F32), da)
    er = lax.broadcasted_iota(jnp.int32, (SMALL_W, BRANCH), 0)
    ec = lax.broadcasted_iota(jnp.int32, (SMALL_W, BRANCH), 1)
    expand = (er == (ec >> 6)).astype(F32)
    cs_full = _mm_hi(cs, expand)
    dt_full = _mm_hi(dt, expand)
    ecs_full = jnp.exp(cs_full)
    cs_last = cs_full[L - 1:L, :]
    w_full = jnp.exp(cs_last - cs_full)
    xdt = xs * dt_full
    xdtw = xdt * w_full
    cs_t = cs.T

    b128 = bc[:, 0:LANES]
    c128 = bc[:, LANES:2 * LANES]
    lane = lax.broadcasted_iota(jnp.int32, (1, LANES), 1)
    cg = [jnp.where(lane < SSM_STATE, c128, 0.0), jnp.where(lane >= SSM_STATE, c128, 0.0)]
    cb = [_mm_nt(cg[g], b128) for g in range(SSM_GROUPS)]

    y_pairs = []
    for p in range(SSM_HEADS // 2):
        xp = xdt[:, p * LANES:(p + 1) * LANES]
        yp = None
        for hh in range(2):
            h = 2 * p + hh
            g = h // (SSM_HEADS // SSM_GROUPS)
            col = cs[:, h:h + 1]
            row = cs_t[h:h + 1, :]
            seg = jnp.exp(jnp.where(tril, col - row, NEG_INF))
            att = cb[g] * seg
            hm = (lane < SSM_HEAD_DIM) if hh == 0 else (lane >= SSM_HEAD_DIM)
            term = _mm(att, jnp.where(hm, xp, 0.0))
            yp = term if yp is None else yp + term
        y_pairs.append(yp)
    y_diag = jnp.concatenate(y_pairs, axis=1)

    y_offs = []
    for g in range(SSM_GROUPS):
        sl = slice(g * 256, (g + 1) * 256)
        s_in = state[g]
        y_offs.append(_mm(cg[g], s_in) * ecs_full[:, sl])
        new = _mm_tn(b128, xdtw[:, sl])
        state[g] = s_in * ecs_full[L - 1:L, sl] + new
    y = y_diag + jnp.concatenate(y_offs, axis=1) + xs * dfull_ref[...]
    y = y * _silu(z_ref[...])
    o_ref[...] = _rms_rows(y, nw_ref[...]).astype(BF16)


def _ssd_call(proj, p, bsz, seq):
    nt = seq // SSD_L
    row = lambda blk: pl.BlockSpec((SSD_L, 512), lambda b, t: (b * nt + t, blk))
    full = lambda shape: pl.BlockSpec(shape, lambda b, t: (0,) * len(shape))
    return pl.pallas_call(
        _ssd_kernel,
        grid=(bsz, nt),
        in_specs=[row(U512["ssm_z"]), row(U512["ssm_x"]),
                  pl.BlockSpec((SSD_L, 256), lambda b, t: (b * nt + t, U256_BC)),
                  pl.BlockSpec((SSD_L, 256), lambda b, t: (b * nt + t, U256_SMALL)),
                  full((CONV_W, 512)), full((1, 512)), full((CONV_W, 256)), full((1, 256)),
                  full((1, SMALL_W)), full((1, SMALL_W)), full((1, 512)), full((1, 512))],
        out_specs=pl.BlockSpec((SSD_L, 512), lambda b, t: (b * nt + t, 0)),
        out_shape=jax.ShapeDtypeStruct((bsz * seq, BRANCH), BF16),
        scratch_shapes=[pltpu.VMEM((SUBLANES + SSD_L, 512), F32),
                        pltpu.VMEM((SUBLANES + SSD_L, 256), F32),
                        pltpu.VMEM((SSM_GROUPS, LANES, 256), F32)],
        compiler_params=_params(("arbitrary", "arbitrary")),
    )(proj, proj, proj, proj, p["cwx"], p["cbx"], p["cwbc"], p["cbbc"], p["dtb"], p["alog"],
      p["dfull"], p["nw"])


GDN_L = 64
GDN_T = 256


def _gdn_kernel(q_ref, k_ref, v_ref, z_ref, sm_ref, cwq_ref, cwk_ref, cwv_ref, dtb_ref, alog_ref,
                nw_ref, o_ref, ext, state, *, bsz):
    L = GDN_L
    T = GDN_T
    H = GDN_HEADS
    t = pl.program_id(0)

    @pl.when(t == 0)
    def _():
        ext[:, :, 0:SUBLANES, :] = jnp.zeros((3, bsz, SUBLANES, BRANCH), F32)
        state[...] = jnp.zeros_like(state)

    ri = lax.broadcasted_iota(jnp.int32, (L, L), 0)
    ci = lax.broadcasted_iota(jnp.int32, (L, L), 1)
    incl = ri >= ci
    strict = ri > ci
    eye = (ri == ci).astype(F32)
    bx = (ri >> 3) ^ (ci >> 3)
    blk = (bx > 0).astype(jnp.int32) + (bx > 1).astype(jnp.int32) + (bx > 3).astype(jnp.int32)
    rt = lax.broadcasted_iota(jnp.int32, (T, T), 0)
    ct = lax.broadcasted_iota(jnp.int32, (T, T), 1)
    tri = ((rt >= ct) & ((rt >> 6) == (ct >> 6))).astype(F32)
    nw = nw_ref[...]
    n_ch = T // L
    chains = [(b, cidx, h) for b in range(bsz) for cidx in range(n_ch) for h in range(H)]

    qkv, beta_all, gc_all, gc_t = [], [], [], []
    for b in range(bsz):
        outs = []
        for i, (r, w) in enumerate(((q_ref, cwq_ref), (k_ref, cwk_ref), (v_ref, cwv_ref))):
            e = ext.at[i, b]
            e[SUBLANES:SUBLANES + T, :] = r[b]
            outs.append(_silu(_causal_conv(e, w, T)))
            e[0:SUBLANES, :] = e[T:T + SUBLANES, :]
        qkv.append(outs)
        sm = sm_ref[b]
        beta_all.append(_sigmoid(sm))
        g_all = -jnp.exp(alog_ref[...]) * _softplus(sm + dtb_ref[...])
        gc = _mm_hi(tri, g_all)
        gc_all.append(gc)
        gc_t.append(gc.T)

    qs, ks, kbs, rhss, decays, egcs, glasts, gcbs = [], [], [], [], [], [], [], []
    for b, cidx, h in chains:
        sl = slice(h * GDN_DIM, (h + 1) * GDN_DIM)
        rw = slice(cidx * L, (cidx + 1) * L)
        qh, kh, vh = (a[rw, sl] for a in qkv[b])
        qh = qh * lax.rsqrt(jnp.sum(qh * qh, axis=-1, keepdims=True) + EPS) * (GDN_DIM ** -0.5)
        kh = kh * lax.rsqrt(jnp.sum(kh * kh, axis=-1, keepdims=True) + EPS)
        beta = beta_all[b][rw, SM_BETA + h:SM_BETA + h + 1]
        gcol = gc_all[b][rw, SM_DECAY + h:SM_DECAY + h + 1]
        grow = gc_t[b][SM_DECAY + h:SM_DECAY + h + 1, rw]
        decays.append(jnp.exp(jnp.where(incl, gcol - grow, NEG_INF)))
        gcb = jnp.broadcast_to(gcol, (L, GDN_DIM))
        egc = jnp.exp(gcb)
        kb = kh * beta
        qs.append(qh); ks.append(kh); kbs.append(kb); gcbs.append(gcb); egcs.append(egc)
        glasts.append(gcb[L - 1:L, :])
        rhss.append(jnp.concatenate([vh * beta, kb * egc], axis=1))

    n = len(chains)
    rng = range(n)
    kk = [_mm_nt(kbs[c], ks[c]) for c in rng]
    qk = [_mm_nt(qs[c], ks[c]) for c in rng]
    ms = [jnp.where(strict, kk[c] * decays[c], 0.0) for c in rng]
    mds = [jnp.where(blk == 0, ms[c], 0.0) for c in rng]
    p2 = [_mm(mds[c], mds[c]) for c in rng]
    base = [eye - mds[c] for c in rng]
    bp = [_mm(base[c], p2[c]) for c in rng]
    p4 = [_mm(p2[c], p2[c]) for c in rng]
    base = [base[c] + bp[c] for c in rng]
    bq = [_mm(base[c], p4[c]) for c in rng]
    inv = [base[c] + bq[c] for c in rng]
    for lvl in range(1, int(math.log2(L // SUBLANES)) + 1):
        oi = [_mm(jnp.where(blk == lvl, ms[c], 0.0), inv[c]) for c in rng]
        ioi = [_mm(inv[c], oi[c]) for c in rng]
        inv = [inv[c] - ioi[c] for c in rng]
    sol = [_mm(inv[c], rhss[c]) for c in rng]
    attn = [qk[c] * decays[c] for c in rng]
    qg = [qs[c] * egcs[c] for c in rng]
    kd = [ks[c] * jnp.exp(glasts[c] - gcbs[c]) for c in rng]
    s_cur = [state[i] for i in range(bsz * H)]
    out_parts = {}
    for cidx in range(n_ch):
        ids = [((b * n_ch + cidx) * H + h, b * H + h) for b in range(bsz) for h in range(H)]
        ws = [_mm(sol[c][:, GDN_DIM:], s_cur[s]) for c, s in ids]
        qgs = [_mm(qg[c], s_cur[s]) for c, s in ids]
        v_new = [sol[c][:, :GDN_DIM] - ws[i] for i, (c, s) in enumerate(ids)]
        av = [_mm(attn[c], v_new[i]) for i, (c, s) in enumerate(ids)]
        kv = [_mm_tn(kd[c], v_new[i]) for i, (c, s) in enumerate(ids)]
        for i, (c, s) in enumerate(ids):
            s_cur[s] = s_cur[s] * jnp.exp(glasts[c]) + kv[i]
            out_parts[c] = qgs[i] + av[i]
    for i in range(bsz * H):
        state[i] = s_cur[i]
    for b in range(bsz):
        z = z_ref[b]
        for cidx in range(n_ch):
            rw = slice(cidx * L, (cidx + 1) * L)
            outs = []
            for h in range(H):
                c = (b * n_ch + cidx) * H + h
                sl = slice(h * GDN_DIM, (h + 1) * GDN_DIM)
                outs.append(_rms_rows(out_parts[c], nw[:, sl]) * _silu(z[rw, sl]))
            o_ref[b, rw, :] = jnp.concatenate(outs, axis=1).astype(BF16)


def _gdn_call(proj3, p, bsz, seq):
    nt = seq // GDN_T
    row = lambda blk, w=512: pl.BlockSpec((bsz, GDN_T, w), lambda t: (0, t, blk))
    full = lambda shape: pl.BlockSpec(shape, lambda t: (0,) * len(shape))
    return pl.pallas_call(
        functools.partial(_gdn_kernel, bsz=bsz),
        grid=(nt,),
        in_specs=[row(U512["gdn_q"]), row(U512["gdn_k"]), row(U512["gdn_v"]), row(U512["gdn_z"]),
                  row(U256_SMALL, 256),
                  full((CONV_W, 512)), full((CONV_W, 512)), full((CONV_W, 512)),
                  full((1, SMALL_W)), full((1, SMALL_W)), full((1, 512))],
        out_specs=pl.BlockSpec((bsz, GDN_T, 512), lambda t: (0, t, 0)),
        out_shape=jax.ShapeDtypeStruct((bsz, seq, BRANCH), BF16),
        scratch_shapes=[pltpu.VMEM((3, bsz, SUBLANES + GDN_T, 512), F32),
                        pltpu.VMEM((bsz * GDN_HEADS, GDN_DIM, GDN_DIM), F32)],
        compiler_params=_params(("arbitrary",)),
    )(proj3, proj3, proj3, proj3, proj3, p["cwq"], p["cwk"], p["cwv"], p["dtb"], p["alog"], p["nw"])


GLA_T = 128
GLA_QK = GLA_HEADS * GLA_DK


def _gla_kernel(qk_ref, v_ref, r_ref, sm_ref, wg_ref, bg_ref, nw_ref, o_ref, state, pbuf, abuf):
    T = GLA_T
    C = GLA_CHUNK
    t = pl.program_id(1)

    @pl.when(t == 0)
    def _():
        state[...] = jnp.zeros_like(state)

    qk = qk_ref[...]
    q = qk[:, :GLA_QK] * (GLA_DK ** -0.5)
    k = qk[:, GLA_QK:]
    v = v_ref[...]
    pre = _mm(sm_ref[...], wg_ref[...]) + bg_ref[...]
    log_a = (jnp.minimum(pre, 0.0) - jnp.log1p(jnp.exp(-jnp.abs(pre)))) / GLA_TAU
    ri = lax.broadcasted_iota(jnp.int32, (T, T), 0)
    ci = lax.broadcasted_iota(jnp.int32, (T, T), 1)
    blocktri = ((ri >> 4) == (ci >> 4)) & (ri >= ci)
    G = _mm_hi(blocktri.astype(F32), log_a)

    rr = lax.broadcasted_iota(jnp.int32, (GLA_QK, BRANCH), 0)
    rc = lax.broadcasted_iota(jnp.int32, (GLA_QK, BRANCH), 1)
    red = ((rr >> 6) == (rc >> 7)).astype(BF16)
    rmod = lax.broadcasted_iota(jnp.int32, (T, GLA_QK), 0) & (C - 1)
    nc = T // C

    def chunk_row(x, jl):
        w = x.shape[1]
        x3 = x.reshape(nc, C, w)
        return jnp.broadcast_to(x3[:, jl:jl + 1, :], (nc, C, w)).reshape(T, w)

    for jl in range(C):
        ks = chunk_row(k, jl)
        gs = chunk_row(G, jl)
        qd = jnp.where(rmod >= jl, q, 0.0)
        pbuf[jl * T:(jl + 1) * T, :] = (qd * ks * jnp.exp(jnp.minimum(G - gs, 0.0))).astype(BF16)
    abuf[...] = jnp.dot(pbuf[...], red, preferred_element_type=F32)
    o_chunks = []
    for c in range(nc):
        oc = None
        for jl in range(C):
            a = abuf[jl * T + c * C:jl * T + (c + 1) * C, :]
            term = a * v_ref[c * C + jl:c * C + jl + 1, :]
            oc = term if oc is None else oc + term
        o_chunks.append(oc)
    o = jnp.concatenate(o_chunks, axis=0)

    head_of_lane = lax.broadcasted_iota(jnp.int32, (C, GLA_QK), 1) >> 6
    qgs, decs, upds = [], [], []
    for c in range(nc):
        rows = slice(c * C, (c + 1) * C)
        gc = G[rows]
        glast = gc[C - 1:C, :]
        qgs.append(q[rows] * jnp.exp(gc))
        decs.append(jnp.exp(glast))
        upds.append(_mm_tn(v[rows], k[rows] * jnp.exp(glast - gc)))
    st = state[...]
    st_in = []
    for c in range(nc):
        st_in.append(st)
        st = st * decs[c] + upds[c]
    state[...] = st
    inter = [jnp.concatenate(
        [_mm_nt(jnp.where(head_of_lane == h, qgs[c], 0.0), st_in[c][h * GLA_DV:(h + 1) * GLA_DV, :])
         for h in range(GLA_HEADS)], axis=1) for c in range(nc)]
    o = o + jnp.concatenate(inter, axis=0)
    r = r_ref[...]
    nw = nw_ref[...]
    outs = []
    for h in range(GLA_HEADS):
        sl = slice(h * GLA_DV, (h + 1) * GLA_DV)
        outs.append(_rms_rows(o[:, sl], nw[:, sl]) * _silu(r[:, sl]))
    o_ref[...] = jnp.concatenate(outs, axis=1).astype(BF16)


def _gla_call(proj, p, bsz, seq):
    nt = seq // GLA_T
    row = lambda blk: pl.BlockSpec((GLA_T, 512), lambda b, t: (b * nt + t, blk))
    full = lambda shape: pl.BlockSpec(shape, lambda b, t: (0,) * len(shape))
    return pl.pallas_call(
        _gla_kernel,
        grid=(bsz, nt),
        in_specs=[row(U512["gla_qk"]), row(U512["gla_v"]), row(U512["gla_r"]),
                  pl.BlockSpec((GLA_T, 256), lambda b, t: (b * nt + t, U256_SMALL)),
                  full((SMALL_W, GLA_QK)), full((1, GLA_QK)), full((1, 512))],
        out_specs=pl.BlockSpec((GLA_T, 512), lambda b, t: (b * nt + t, 0)),
        out_shape=jax.ShapeDtypeStruct((bsz * seq, BRANCH), BF16),
        scratch_shapes=[pltpu.VMEM((BRANCH, GLA_QK), F32),
                        pltpu.VMEM((GLA_CHUNK * GLA_T, GLA_QK), BF16),
                        pltpu.VMEM((GLA_CHUNK * GLA_T, BRANCH), F32)],
        compiler_params=_params(("arbitrary", "arbitrary")),
    )(proj, proj, proj, proj, p["wg"], p["bg"], p["nw"])


LRU_T = 256
LRU_PAD = LRU_T // 2


def _lru_kernel(x_ref, gate_ref, cw_ref, cb_ref, wa_ref, ba_ref, wx_ref, bx_ref, lam_ref, o_ref,
                ext, abuf, hbuf, carry):
    T = LRU_T
    P = LRU_PAD
    t = pl.program_id(1)

    @pl.when(t == 0)
    def _():
        ext[0:SUBLANES, :] = jnp.zeros((SUBLANES, BRANCH), F32)
        abuf[0:P, :] = jnp.ones((P, BRANCH), F32)
        hbuf[0:P, :] = jnp.zeros((P, BRANCH), F32)
        carry[...] = jnp.zeros_like(carry)

    ext[SUBLANES:SUBLANES + T, :] = x_ref[...]
    xc = _causal_conv(ext, cw_ref, T) + cb_ref[...]
    ext[0:SUBLANES, :] = ext[T:T + SUBLANES, :]
    gate_r = _sigmoid(_mm(xc, wa_ref[...]) + ba_ref[...])
    gate_i = _sigmoid(_mm(xc, wx_ref[...]) + bx_ref[...])
    log_a = -LRU_C * gate_r * _softplus(-lam_ref[...])
    abuf[P:P + T, :] = jnp.exp(log_a)
    th = jnp.tanh(log_a)
    hbuf[P:P + T, :] = jnp.sqrt(-2.0 * th / (1.0 - th)) * (gate_i * xc)
    s = 1
    while s < T:
        a_cur = abuf[P:P + T, :]
        h_cur = hbuf[P:P + T, :]
        a_sh = abuf[P - s:P - s + T, :]
        h_sh = hbuf[P - s:P - s + T, :]
        hbuf[P:P + T, :] = h_cur + a_cur * h_sh
        abuf[P:P + T, :] = a_cur * a_sh
        s *= 2
    h = hbuf[P:P + T, :] + abuf[P:P + T, :] * carry[0:1, :]
    carry[0:1, :] = h[T - 1:T, :]
    o_ref[...] = (h * _gelu(gate_ref[...])).astype(BF16)


def _lru_call(proj, p, bsz, seq):
    nt = seq // LRU_T
    row = lambda blk: pl.BlockSpec((LRU_T, 512), lambda b, t: (b * nt + t, blk))
    full = lambda shape: pl.BlockSpec(shape, lambda b, t: (0,) * len(shape))
    return pl.pallas_call(
        _lru_kernel,
        grid=(bsz, nt),
        in_specs=[row(U512["lru_x"]), row(U512["lru_gate"]),
                  full((CONV_W, 512)), full((1, 512)), full((512, 512)), full((1, 512)),
                  full((512, 512)), full((1, 512)), full((1, 512))],
        out_specs=pl.BlockSpec((LRU_T, 512), lambda b, t: (b * nt + t, 0)),
        out_shape=jax.ShapeDtypeStruct((bsz * seq, BRANCH), BF16),
        scratch_shapes=[pltpu.VMEM((SUBLANES + LRU_T, 512), F32),
                        pltpu.VMEM((LRU_PAD + LRU_T, 512), F32),
                        pltpu.VMEM((LRU_PAD + LRU_T, 512), F32),
                        pltpu.VMEM((SUBLANES, 512), F32)],
        compiler_params=_params(("arbitrary", "arbitrary")),
    )(proj, proj, p["cw"], p["cb"], p["wa"], p["ba"], p["wx"], p["bx"], p["lam"])


MRG_TM = 256


def _merge_kernel(x_ref, g_ref, lg_ref, y0_ref, y1_ref, y2_ref, y3_ref, wb_ref, wo_ref, o_ref):
    merged = None
    for i, y_ref in enumerate((y0_ref, y1_ref, y2_ref, y3_ref)):
        br = jnp.dot(y_ref[...], wb_ref[i], preferred_element_type=F32)
        term = _sigmoid(lg_ref[:, i * D_MODEL:(i + 1) * D_MODEL]) * br
        merged = term if merged is None else merged + term
    out = jnp.dot(merged.astype(BF16), wo_ref[...], preferred_element_type=F32)
    o_ref[...] = x_ref[...] + g_ref[0] * out


def _merge_call(x2, g1, proj, ys, wb, wo, seq):
    n_tok = x2.shape[0]
    per_b = seq // MRG_TM
    yspec = pl.BlockSpec((MRG_TM, BRANCH), lambda i: (i, 0))
    return pl.pallas_call(
        _merge_kernel,
        grid=(n_tok // MRG_TM,),
        in_specs=[pl.BlockSpec((MRG_TM, D_MODEL), lambda i: (i, 0)),
                  pl.BlockSpec((1, 1, D_MODEL), lambda i: (i // per_b, 0, 0)),
                  pl.BlockSpec((MRG_TM, N_BRANCH * D_MODEL), lambda i: (i, 0)),
                  yspec, yspec, yspec, yspec,
                  pl.BlockSpec((N_BRANCH, BRANCH, D_MODEL), lambda i: (0, 0, 0)),
                  pl.BlockSpec((D_MODEL, D_MODEL), lambda i: (0, 0))],
        out_specs=pl.BlockSpec((MRG_TM, D_MODEL), lambda i: (i, 0)),
        out_shape=jax.ShapeDtypeStruct((n_tok, D_MODEL), F32),
        compiler_params=_params(("arbitrary",)),
    )(x2, g1, proj, *ys, wb, wo)


PS_TB = 512
N_SCORE_ROWS = 2 * PEER_HEADS * PEER_KEYS


def _peer_score_kernel(x_ref, nw_ref, sc_ref, sh_ref, wqt_ref, keys_ref, h2t_ref, st_ref):
    h = _rms_rows(x_ref[...], nw_ref[...])
    h = h * (1.0 + sc_ref[0]) + sh_ref[0]
    ht = h.T.astype(BF16)
    h2t_ref[...] = ht
    qt = jnp.dot(wqt_ref[...], ht, preferred_element_type=F32).astype(BF16)
    for g in range(2 * PEER_HEADS):
        rows = slice(g * PEER_KEYS, (g + 1) * PEER_KEYS)
        st_ref[rows, :] = jnp.dot(keys_ref[g], qt[rows, :], preferred_element_type=F32)


def _peer_score_call(x2, nw, sc, sh, wqt, keys, seq):
    n_tok = x2.shape[0]
    per_b = seq // PS_TB
    return pl.pallas_call(
        _peer_score_kernel,
        grid=(n_tok // PS_TB,),
        in_specs=[pl.BlockSpec((PS_TB, D_MODEL), lambda i: (i, 0)),
                  pl.BlockSpec((1, D_MODEL), lambda i: (0, 0)),
                  pl.BlockSpec((1, 1, D_MODEL), lambda i: (i // per_b, 0, 0)),
                  pl.BlockSpec((1, 1, D_MODEL), lambda i: (i // per_b, 0, 0)),
                  pl.BlockSpec((N_SCORE_ROWS, D_MODEL), lambda i: (0, 0)),
                  pl.BlockSpec((2 * PEER_HEADS, PEER_KEYS, PEER_HALF), lambda i: (0, 0, 0))],
        out_specs=[pl.BlockSpec((D_MODEL, PS_TB), lambda i: (0, i)),
                   pl.BlockSpec((N_SCORE_ROWS, PS_TB), lambda i: (0, i))],
        out_shape=[jax.ShapeDtypeStruct((D_MODEL, n_tok), BF16),
                   jax.ShapeDtypeStruct((N_SCORE_ROWS, n_tok), F32)],
        compiler_params=_params(("arbitrary",)),
    )(x2, nw, sc, sh, wqt, keys)


PT_TL = 128
NOT_RANKED = 255.0
_CAND = [(i, j) for i in range(PEER_TOPK) for j in range(PEER_TOPK) if (i + 1) * (j + 1) <= PEER_TOPK]
N_CAND_ROWS = -(-len(_CAND) // SUBLANES) * SUBLANES


def _pop_max(x, iota):
    m = jnp.max(x, axis=0, keepdims=True)
    first = jnp.min(jnp.where(x == m, iota, float(x.shape[0])), axis=0, keepdims=True)
    return m, first, jnp.where(iota == first, NEG_INF, x)


def _gate_tables_head(st_ref, ce_ref, r2_ref, e2_ref, cand, h, exact_ties):
    TL = PT_TL
    iota_k = lax.broadcasted_iota(jnp.int32, (PEER_KEYS, TL), 0).astype(F32)
    iota_c = lax.broadcasted_iota(jnp.int32, (N_CAND_ROWS, TL), 0).astype(F32)
    rows1 = slice(h * PEER_KEYS, (h + 1) * PEER_KEYS)
    rows2 = slice((PEER_HEADS + h) * PEER_KEYS, (PEER_HEADS + h + 1) * PEER_KEYS)
    s1 = st_ref[rows1, :]
    s2 = st_ref[rows2, :]
    x = s1
    t1, pick1 = [], []
    for _ in range(PEER_TOPK):
        if exact_ties:
            m, f, x = _pop_max(x, iota_k)
            pick1.append(f)
        else:
            m = jnp.max(x, axis=0, keepdims=True)
            x = jnp.where(x == m, NEG_INF, x)
        t1.append(m)
    x = s2
    t2 = []
    rank2 = jnp.full((PEER_KEYS, TL), NOT_RANKED, F32)
    for r in range(PEER_TOPK):
        if exact_ties:
            m, f, x = _pop_max(x, iota_k)
            hit_rows = iota_k == f
        else:
            m = jnp.max(x, axis=0, keepdims=True)
            hit_rows = x == m
            x = jnp.where(hit_rows, NEG_INF, x)
        t2.append(m)
        rank2 = jnp.where(hit_rows, float(r), rank2)
    for n, (i, j) in enumerate(_CAND):
        cand[n:n + 1, :] = t1[i] + t2[j]
    c = cand[...]
    x = c
    tau = None
    for _ in range(PEER_TOPK):
        tau, _, x = _pop_max(x, iota_c)
    m1, m2 = t1[0], t2[0]
    zsum = jnp.sum(jnp.where(c >= tau, jnp.exp(c - (m1 + m2)), 0.0), axis=0, keepdims=True)
    count1 = jnp.zeros((PEER_KEYS, TL), F32)
    for i in range(PEER_TOPK):
        cnt = None
        for j in range(PEER_TOPK):
            if (i + 1) * (j + 1) <= PEER_TOPK:
                hit = jnp.where(t1[i] + t2[j] >= tau, 1.0, 0.0)
                cnt = hit if cnt is None else cnt + hit
        sel = (iota_k == pick1[i]) if exact_ties else (s1 == t1[i])
        count1 = jnp.where(sel, cnt, count1)
    ce_ref[rows1, :] = count1
    ce_ref[rows2, :] = jnp.exp(s1 - m1) * (0.5 / zsum)
    r2_ref[rows1, :] = rank2.astype(BF16)
    e2_ref[rows1, :] = jnp.exp(s2 - m2).astype(BF16)
    if exact_ties:
        return None
    n1 = jnp.sum(jnp.where(s1 >= t1[-1], 1.0, 0.0), axis=0, keepdims=True)
    n2 = jnp.sum(jnp.where(s2 >= t2[-1], 1.0, 0.0), axis=0, keepdims=True)
    return jnp.abs(n1 - PEER_TOPK) + jnp.abs(n2 - PEER_TOPK)


def _peer_gate_kernel(st_ref, ce_ref, r2_ref, e2_ref, cand):
    cand[...] = jnp.full((N_CAND_ROWS, PT_TL), NEG_INF, F32)
    tied = None
    for h in range(PEER_HEADS):
        t = _gate_tables_head(st_ref, ce_ref, r2_ref, e2_ref, cand, h, exact_ties=False)
        tied = t if tied is None else tied + t

    @pl.when(jnp.max(tied) > 0.0)
    def _():
        for h in range(PEER_HEADS):
            _gate_tables_head(st_ref, ce_ref, r2_ref, e2_ref, cand, h, exact_ties=True)


def _peer_gate_call(scores_t):
    n_tok = scores_t.shape[1]
    half = PEER_HEADS * PEER_KEYS
    return pl.pallas_call(
        _peer_gate_kernel,
        grid=(n_tok // PT_TL,),
        in_specs=[pl.BlockSpec((N_SCORE_ROWS, PT_TL), lambda i: (0, i))],
        out_specs=[pl.BlockSpec((N_SCORE_ROWS, PT_TL), lambda i: (0, i)),
                   pl.BlockSpec((half, PT_TL), lambda i: (0, i)),
                   pl.BlockSpec((half, PT_TL), lambda i: (0, i))],
        out_shape=[jax.ShapeDtypeStruct((N_SCORE_ROWS, n_tok), F32),
                   jax.ShapeDtypeStruct((half, n_tok), BF16),
                   jax.ShapeDtypeStruct((half, n_tok), BF16)],
        scratch_shapes=[pltpu.VMEM((N_CAND_ROWS, PT_TL), F32)],
        compiler_params=_params(("arbitrary",)),
    )(scores_t)


PE_TB = 512
PE_EB = 2048
HALF_ROWS = PEER_HEADS * PEER_KEYS


def _bcast_rows_bf16(row):
    r16 = jnp.broadcast_to(row, (2 * SUBLANES, LANES)).astype(BF16)
    return jnp.concatenate([r16] * (PEER_KEYS // (2 * SUBLANES)), axis=0)


def _peer_expert_kernel(h2t_ref, u_ref, vt_ref, ce_ref, r2_ref, e2_ref, x_ref, g_ref, fw_ref,
                        o_ref, acc, zt, *, final_norm):
    j = pl.program_id(1)
    H = PEER_HEADS

    @pl.when(j == 0)
    def _():
        acc[...] = jnp.zeros_like(acc)

    c0 = math.sqrt(2.0 / math.pi)
    zero = jnp.zeros((PEER_KEYS, LANES), BF16)
    st = jnp.dot(u_ref[0], h2t_ref[...], preferred_element_type=F32)
    for q in range(PE_EB // PEER_KEYS):
        rs = slice(q * PEER_KEYS, (q + 1) * PEER_KEYS)
        for lc in range(PE_TB // LANES):
            ls = slice(lc * LANES, (lc + 1) * LANES)
            w = None
            for h in range(H):
                rows = slice(h * PEER_KEYS, (h + 1) * PEER_KEYS)
                cnt = _bcast_rows_bf16(ce_ref[0, h, q:q + 1, ls])
                g1 = _bcast_rows_bf16(ce_ref[1, h, q:q + 1, ls])
                sel_g1 = jnp.minimum(jnp.maximum(cnt - r2_ref[rows, ls], zero), g1)
                term = sel_g1 * e2_ref[rows, ls]
                w = term if w is None else w + term
            x = st[rs, ls]
            th = jnp.tanh(x * (c0 + (c0 * 0.044715) * (x * x)))
            zt[rs, ls] = (x + x * th).astype(BF16) * w
    acc[...] += jnp.dot(vt_ref[0], zt[...], preferred_element_type=F32)

    @pl.when(j == pl.num_programs(1) - 1)
    def _():
        xn = x_ref[...] + g_ref[0] * acc[...].T
        if final_norm:
            xn = _rms_rows(xn, fw_ref[...])
        o_ref[...] = xn


def _peer_expert_call(h2, u_bf, vt_bf, layer, ce, r2, e2, x2, g2, fw, seq, final_norm):
    n_tok = x2.shape[0]
    per_b = seq // PE_TB
    kern = functools.partial(_peer_expert_kernel, final_norm=final_norm)
    ce_k = ce.reshape(2, PEER_HEADS, PEER_KEYS, n_tok)
    return pl.pallas_call(
        kern,
        grid=(n_tok // PE_TB, PEER_EXPERTS // PE_EB),
        in_specs=[pl.BlockSpec((D_MODEL, PE_TB), lambda i, j: (0, i)),
                  pl.BlockSpec((1, PE_EB, D_MODEL), lambda i, j: (layer, j, 0)),
                  pl.BlockSpec((1, D_MODEL, PE_EB), lambda i, j: (layer, 0, j)),
                  pl.BlockSpec((2, PEER_HEADS, PE_EB // PEER_KEYS, PE_TB), lambda i, j: (0, 0, j, i)),
                  pl.BlockSpec((HALF_ROWS, PE_TB), lambda i, j: (0, i)),
                  pl.BlockSpec((HALF_ROWS, PE_TB), lambda i, j: (0, i)),
                  pl.BlockSpec((PE_TB, D_MODEL), lambda i, j: (i, 0)),
                  pl.BlockSpec((1, 1, D_MODEL), lambda i, j: (i // per_b, 0, 0)),
                  pl.BlockSpec((1, D_MODEL), lambda i, j: (0, 0))],
        out_specs=pl.BlockSpec((PE_TB, D_MODEL), lambda i, j: (i, 0)),
        out_shape=jax.ShapeDtypeStruct((n_tok, D_MODEL), F32),
        scratch_shapes=[pltpu.VMEM((D_MODEL, PE_TB), F32),
                        pltpu.VMEM((PE_EB, PE_TB), BF16)],
        compiler_params=_params(("arbitrary", "arbitrary")),
    )(h2, u_bf, vt_bf, ce_k, r2, e2, x2, g2, fw)


def _pad_lanes(vec, start, width=SMALL_W):
    out = jnp.zeros((1, width), F32)
    return lax.dynamic_update_slice(out, vec.reshape(1, -1).astype(F32), (0, start))


def _block_diag(w):
    n, d, e = w.shape
    eye = jnp.eye(n, dtype=w.dtype)
    return (eye[:, None, :, None] * w[:, :, None, :]).reshape(n * d, n * e)


def kernel(x, c, w_ada, b_ada, norm_mix_w, norm_ffn_w, w_in, ssm_conv_w, ssm_conv_b, ssm_dt_bias,
           ssm_a_log, ssm_d, ssm_norm_w, gdn_conv_w, gdn_a_log, gdn_dt_bias, gdn_norm_w, gla_w_gate,
           gla_b_gate, gla_norm_w, lru_conv_w, lru_conv_b, lru_w_a, lru_b_a, lru_w_x, lru_b_x,
           lru_lambda, w_branch, w_out, peer_w_q, peer_sub_keys, peer_u, peer_v, final_norm_w):
    bsz, seq, d = x.shape
    n_layers = w_in.shape[0]
    n_tok = bsz * seq
    x2 = x.reshape(n_tok, d)

    c_pad = jnp.zeros((SUBLANES, d), F32).at[:bsz].set(c)
    mod = _ada_call(c_pad, w_ada, b_ada)

    row1 = lambda v: v.reshape(1, -1).astype(F32)
    w_perm = _permute_w_in(w_in)
    u_bf = peer_u.astype(BF16)
    vt_bf = jnp.swapaxes(peer_v, 1, 2).astype(BF16)

    for l in range(n_layers):
        m6 = mod[l, :bsz].reshape(bsz, 6, 1, d)
        sh1, sc1, g1, sh2, sc2, g2 = (m6[:, i] for i in range(6))
        proj = _inproj_call(x2, row1(norm_mix_w[l]), sc1, sh1, w_perm, l, seq)

        ssd_p = dict(cwx=ssm_conv_w[l][:, :512], cbx=row1(ssm_conv_b[l][:512]),
                     cwbc=ssm_conv_w[l][:, 512:], cbbc=row1(ssm_conv_b[l][512:]),
                     dtb=_pad_lanes(ssm_dt_bias[l], SM_DT), alog=_pad_lanes(ssm_a_log[l], SM_DT),
                     dfull=row1(jnp.repeat(ssm_d[l], SSM_HEAD_DIM)), nw=row1(ssm_norm_w[l]))
        y_ssd = _ssd_call(proj, ssd_p, bsz, seq)

        gdn_p = dict(cwq=gdn_conv_w[l][:, :512], cwk=gdn_conv_w[l][:, 512:1024],
                     cwv=gdn_conv_w[l][:, 1024:], dtb=_pad_lanes(gdn_dt_bias[l], SM_DECAY),
                     alog=_pad_lanes(gdn_a_log[l], SM_DECAY),
                     nw=row1(jnp.tile(gdn_norm_w[l], GDN_HEADS)))
        y_gdn = _gdn_call(proj.reshape(bsz, seq, N_COLS), gdn_p, bsz, seq).reshape(n_tok, BRANCH)

        wg = jnp.zeros((SMALL_W, GLA_QK), F32).at[SM_LOW:SM_LOW + GLA_RANK].set(gla_w_gate[l])
        gla_p = dict(wg=wg.astype(BF16), bg=row1(gla_b_gate[l]),
                     nw=row1(jnp.tile(gla_norm_w[l], GLA_HEADS)))
        y_gla = _gla_call(proj, gla_p, bsz, seq)

        lru_p = dict(cw=lru_conv_w[l], cb=row1(lru_conv_b[l]),
                     wa=_block_diag(lru_w_a[l]).astype(BF16), ba=row1(lru_b_a[l]),
                     wx=_block_diag(lru_w_x[l]).astype(BF16), bx=row1(lru_b_x[l]),
                     lam=row1(lru_lambda[l]))
        y_lru = _lru_call(proj, lru_p, bsz, seq)

        x2 = _merge_call(x2, g1, proj, (y_ssd, y_gdn, y_gla, y_lru),
                         w_branch[l].astype(BF16), w_out[l].astype(BF16), seq)

        wqt = peer_w_q[l].reshape(d, PEER_HEADS, 2, PEER_HALF).transpose(2, 1, 3, 0)
        wqt = wqt.reshape(N_SCORE_ROWS, d).astype(BF16)
        keys = peer_sub_keys[l].transpose(1, 0, 2, 3).reshape(2 * PEER_HEADS, PEER_KEYS, PEER_HALF)
        h2, scores_t = _peer_score_call(x2, row1(norm_ffn_w[l]), sc2, sh2, wqt, keys.astype(BF16), seq)
        ce, r2, e2 = _peer_gate_call(scores_t)
        x2 = _peer_expert_call(h2, u_bf, vt_bf, l, ce, r2, e2,
                               x2, g2, row1(final_norm_w), seq, final_norm=(l == n_layers - 1))
    return x2.reshape(bsz, seq, d)
```

```python
import functools
import math

import jax
import jax.numpy as jnp
from jax import lax
from jax.experimental import pallas as pl
from jax.experimental.pallas import tpu as pltpu

F32 = jnp.float32
BF16 = jnp.bfloat16
NEG_INF = float("-inf")

D_MODEL = 1024
N_LAYERS = 2
EPS = 1e-6
CONV_W = 4
BRANCH = 512
N_BRANCH = 4
SSM_HEADS = 8
SSM_HEAD_DIM = 64
SSM_GROUPS = 2
SSM_STATE = 64
GDN_HEADS = 4
GDN_DIM = 128
GLA_HEADS = 4
GLA_DK = 64
GLA_DV = 128
GLA_RANK = 16
GLA_TAU = 16.0
GLA_CHUNK = 16
LRU_BLOCKS = 8
LRU_BLOCK_DIM = 64
LRU_C = 8.0
PEER_HEADS = 8
PEER_KEYS = 128
PEER_EXPERTS = PEER_KEYS * PEER_KEYS
PEER_HALF = 128
PEER_TOPK = 16

LANES = 128
SUBLANES = 8
VMEM_LIMIT = 48 * 1024 * 1024

_SRC = {}
_off = 0
for _name, _w in (("ssm_z", 512), ("ssm_x", 512), ("ssm_b", 128), ("ssm_c", 128), ("ssm_dt", 8),
                  ("gdn_q", 512), ("gdn_k", 512), ("gdn_v", 512), ("gdn_z", 512), ("gdn_beta", 4),
                  ("gdn_decay", 4), ("gla_q", 256), ("gla_k", 256), ("gla_v", 512), ("gla_r", 512),
                  ("gla_low", 16), ("lru_x", 512), ("lru_gate", 512), ("merge", 4096)):
    _SRC[_name] = (_off, _w)
    _off += _w
D_IN = _off
_DST_ORDER = ("merge", "ssm_z", "ssm_x", "gdn_q", "gdn_k", "gdn_v", "gdn_z", "gla_q", "gla_k",
              "gla_v", "gla_r", "lru_x", "lru_gate", "ssm_b", "ssm_c", "ssm_dt", "gdn_beta",
              "gdn_decay", "gla_low")
SMALL_W = 256
N_COLS = 4096 + 11 * 512 + 256 + SMALL_W
U512 = {"ssm_z": 8, "ssm_x": 9, "gdn_q": 10, "gdn_k": 11, "gdn_v": 12, "gdn_z": 13, "gla_qk": 14,
        "gla_v": 15, "gla_r": 16, "lru_x": 17, "lru_gate": 18}
U256_BC = 38
U256_SMALL = 39
SM_DT = 0
SM_BETA = 8
SM_DECAY = 12
SM_LOW = 16


def _permute_w_in(w):
    w = w.astype(BF16)
    parts = [w[..., _SRC[name][0]:_SRC[name][0] + _SRC[name][1]] for name in _DST_ORDER]
    used = sum(_SRC[name][1] for name in _DST_ORDER)
    parts.append(jnp.zeros(w.shape[:-1] + (N_COLS - used,), w.dtype))
    return jnp.concatenate(parts, axis=-1)


def _mm(a, b):
    return jnp.dot(a.astype(BF16), b.astype(BF16), preferred_element_type=F32)


def _mm_nt(a, b):
    return lax.dot_general(a.astype(BF16), b.astype(BF16), (((1,), (1,)), ((), ())),
                           preferred_element_type=F32)


def _mm_tn(a, b):
    return lax.dot_general(a.astype(BF16), b.astype(BF16), (((0,), (0,)), ((), ())),
                           preferred_element_type=F32)


def _split3(x):
    hi = x.astype(BF16)
    r = x - hi.astype(F32)
    mid = r.astype(BF16)
    lo = (r - mid.astype(F32)).astype(BF16)
    return hi, mid, lo


def _sel_mm(sel, x):
    s = sel.astype(BF16)
    hi, mid, lo = _split3(x)
    return (jnp.dot(s, lo, preferred_element_type=F32) + jnp.dot(s, mid, preferred_element_type=F32)
            + jnp.dot(s, hi, preferred_element_type=F32))


def _mm_sel(x, sel):
    s = sel.astype(BF16)
    hi, mid, lo = _split3(x)
    return (jnp.dot(lo, s, preferred_element_type=F32) + jnp.dot(mid, s, preferred_element_type=F32)
            + jnp.dot(hi, s, preferred_element_type=F32))


def _sigmoid(x):
    return 1.0 / (1.0 + jnp.exp(-x))


def _silu(x):
    return x * _sigmoid(x)


def _softplus(x):
    return jnp.maximum(x, 0.0) + jnp.log1p(jnp.exp(-jnp.abs(x)))


def _gelu(x):
    c = math.sqrt(2.0 / math.pi)
    return 0.5 * x * (1.0 + jnp.tanh(c * (x + 0.044715 * (x * x * x))))


def _rms_rows(x, w):
    ms = jnp.mean(x * x, axis=-1, keepdims=True)
    return x * lax.rsqrt(ms + EPS) * w


def _causal_conv(ext_ref, w_ref, n_rows):
    acc = None
    for k in range(CONV_W):
        term = w_ref[k:k + 1, :] * ext_ref[SUBLANES - CONV_W + 1 + k:SUBLANES - CONV_W + 1 + k + n_rows, :]
        acc = term if acc is None else acc + term
    return acc


def _params(sem, flags=None):
    return pltpu.CompilerParams(dimension_semantics=sem, vmem_limit_bytes=VMEM_LIMIT, flags=flags)


ADA_TN = 1536


def _ada_kernel(c_ref, w_ref, b_ref, o_ref):
    c = c_ref[...]
    o_ref[0] = _mm(_silu(c), w_ref[0]) + b_ref[0]


def _ada_call(c_pad, w_ada, b_ada):
    n_l = w_ada.shape[0]
    n_out = w_ada.shape[2]
    return pl.pallas_call(
        _ada_kernel,
        grid=(n_l, n_out // ADA_TN),
        in_specs=[pl.BlockSpec((SUBLANES, D_MODEL), lambda l, j: (0, 0)),
                  pl.BlockSpec((1, D_MODEL, ADA_TN), lambda l, j: (l, 0, j)),
                  pl.BlockSpec((1, 1, ADA_TN), lambda l, j: (l, 0, j))],
        out_specs=pl.BlockSpec((1, SUBLANES, ADA_TN), lambda l, j: (l, 0, j)),
        out_shape=jax.ShapeDtypeStruct((n_l, SUBLANES, n_out), F32),
        compiler_params=_params(("arbitrary", "arbitrary")),
    )(c_pad, w_ada, b_ada.reshape(n_l, 1, n_out))


INP_TM = 1024
INP_TN = 2048


def _inproj_kernel(x_ref, nw_ref, sc_ref, sh_ref, w_ref, o_ref, h_scr):
    @pl.when(pl.program_id(1) == 0)
    def _():
        h = _rms_rows(x_ref[...], nw_ref[...])
        h = h * (1.0 + sc_ref[0]) + sh_ref[0]
        h_scr[...] = h.astype(BF16)

    o_ref[...] = jnp.dot(h_scr[...], w_ref[0], preferred_element_type=F32)


def _inproj_call(x2, nw, sc, sh, w_perm, layer, seq):
    n_tok = x2.shape[0]
    per_b = seq // INP_TM
    return pl.pallas_call(
        _inproj_kernel,
        grid=(n_tok // INP_TM, N_COLS // INP_TN),
        in_specs=[pl.BlockSpec((INP_TM, D_MODEL), lambda i, j: (i, 0)),
                  pl.BlockSpec((1, D_MODEL), lambda i, j: (0, 0)),
                  pl.BlockSpec((1, 1, D_MODEL), lambda i, j: (i // per_b, 0, 0)),
                  pl.BlockSpec((1, 1, D_MODEL), lambda i, j: (i // per_b, 0, 0)),
                  pl.BlockSpec((1, D_MODEL, INP_TN), lambda i, j: (layer, 0, j))],
        out_specs=pl.BlockSpec((INP_TM, INP_TN), lambda i, j: (i, j)),
        out_shape=jax.ShapeDtypeStruct((n_tok, N_COLS), F32),
        scratch_shapes=[pltpu.VMEM((INP_TM, D_MODEL), BF16)],
        compiler_params=_params(("arbitrary", "arbitrary")),
    )(x2, nw, sc, sh, w_perm)


SSD_L = 128


def _ssd_kernel(z_ref, xs_ref, bc_ref, sm_ref, cwx_ref, cbx_ref, cwbc_ref, cbbc_ref, dtb_ref,
                alog_ref, dfull_ref, nw_ref, o_ref, extx, extbc, state):
    L = SSD_L
    t = pl.program_id(1)

    @pl.when(t == 0)
    def _():
        extx[0:SUBLANES, :] = jnp.zeros((SUBLANES, BRANCH), F32)
        extbc[0:SUBLANES, :] = jnp.zeros((SUBLANES, 256), F32)
        state[...] = jnp.zeros_like(state)

    extx[SUBLANES:SUBLANES + L, :] = xs_ref[...]
    extbc[SUBLANES:SUBLANES + L, :] = bc_ref[...]
    xs = _silu(_causal_conv(extx, cwx_ref, L) + cbx_ref[...])
    bc = _silu(_causal_conv(extbc, cwbc_ref, L) + cbbc_ref[...])
    extx[0:SUBLANES, :] = extx[L:L + SUBLANES, :]
    extbc[0:SUBLANES, :] = extbc[L:L + SUBLANES, :]

    dt = _softplus(sm_ref[...] + dtb_ref[...])
    da = dt * (-jnp.exp(alog_ref[...]))
    ri = lax.broadcasted_iota(jnp.int32, (L, L), 0)
    ci = lax.broadcasted_iota(jnp.int32, (L, L), 1)
    tril = ri >= ci
    cs = _sel_mm(tril, da)
    er = lax.broadcasted_iota(jnp.int32, (SMALL_W, BRANCH), 0)
    ec = lax.broadcasted_iota(jnp.int32, (SMALL_W, BRANCH), 1)
    expand = (er == (ec >> 6)).astype(F32)
    cs_full = _mm_sel(cs, expand)
    dt_full = _mm_sel(dt, expand)
    ecs_full = jnp.exp(cs_full)
    cs_last = cs_full[L - 1:L, :]
    w_full = jnp.exp(cs_last - cs_full)
    xdt = xs * dt_full
    xdtw = xdt * w_full
    cs_t = cs.T

    b128 = bc[:, 0:LANES]
    c128 = bc[:, LANES:2 * LANES]
    lane = lax.broadcasted_iota(jnp.int32, (1, LANES), 1)
    cg = [jnp.where(lane < SSM_STATE, c128, 0.0), jnp.where(lane >= SSM_STATE, c128, 0.0)]
    cb = [_mm_nt(cg[g], b128) for g in range(SSM_GROUPS)]

    y_pairs = []
    for p in range(SSM_HEADS // 2):
        xp = xdt[:, p * LANES:(p + 1) * LANES]
        yp = None
        for hh in range(2):
            h = 2 * p + hh
            g = h // (SSM_HEADS // SSM_GROUPS)
            col = cs[:, h:h + 1]
            row = cs_t[h:h + 1, :]
            seg = jnp.exp(jnp.where(tril, col - row, NEG_INF))
            att = cb[g] * seg
            hm = (lane < SSM_HEAD_DIM) if hh == 0 else (lane >= SSM_HEAD_DIM)
            term = _mm(att, jnp.where(hm, xp, 0.0))
            yp = term if yp is None else yp + term
        y_pairs.append(yp)
    y_diag = jnp.concatenate(y_pairs, axis=1)

    y_offs = []
    for g in range(SSM_GROUPS):
        sl = slice(g * 256, (g + 1) * 256)
        s_in = state[g]
        y_offs.append(_mm(cg[g], s_in) * ecs_full[:, sl])
        new = _mm_tn(b128, xdtw[:, sl])
        state[g] = s_in * ecs_full[L - 1:L, sl] + new
    y = y_diag + jnp.concatenate(y_offs, axis=1) + xs * dfull_ref[...]
    y = y * _silu(z_ref[...])
    o_ref[...] = _rms_rows(y, nw_ref[...]).astype(BF16)


def _ssd_call(proj, p, bsz, seq):
    nt = seq // SSD_L
    row = lambda blk: pl.BlockSpec((SSD_L, 512), lambda b, t: (b * nt + t, blk))
    full = lambda shape: pl.BlockSpec(shape, lambda b, t: (0,) * len(shape))
    return pl.pallas_call(
        _ssd_kernel,
        grid=(bsz, nt),
        in_specs=[row(U512["ssm_z"]), row(U512["ssm_x"]),
                  pl.BlockSpec((SSD_L, 256), lambda b, t: (b * nt + t, U256_BC)),
                  pl.BlockSpec((SSD_L, 256), lambda b, t: (b * nt + t, U256_SMALL)),
                  full((CONV_W, 512)), full((1, 512)), full((CONV_W, 256)), full((1, 256)),
                  full((1, SMALL_W)), full((1, SMALL_W)), full((1, 512)), full((1, 512))],
        out_specs=pl.BlockSpec((SSD_L, 512), lambda b, t: (b * nt + t, 0)),
        out_shape=jax.ShapeDtypeStruct((bsz * seq, BRANCH), BF16),
        scratch_shapes=[pltpu.VMEM((SUBLANES + SSD_L, 512), F32),
                        pltpu.VMEM((SUBLANES + SSD_L, 256), F32),
                        pltpu.VMEM((SSM_GROUPS, LANES, 256), F32)],
        compiler_params=_params(("arbitrary", "arbitrary")),
    )(proj, proj, proj, proj, p["cwx"], p["cbx"], p["cwbc"], p["cbbc"], p["dtb"], p["alog"],
      p["dfull"], p["nw"])


GDN_L = 64
GDN_T = 256


def _gdn_kernel(q_ref, k_ref, v_ref, z_ref, sm_ref, cwq_ref, cwk_ref, cwv_ref, dtb_ref, alog_ref,
                nw_ref, o_ref, ext, state, *, bsz):
    L = GDN_L
    T = GDN_T
    H = GDN_HEADS
    t = pl.program_id(0)

    @pl.when(t == 0)
    def _():
        ext[:, :, 0:SUBLANES, :] = jnp.zeros((3, bsz, SUBLANES, BRANCH), F32)
        state[...] = jnp.zeros_like(state)

    ri = lax.broadcasted_iota(jnp.int32, (L, L), 0)
    ci = lax.broadcasted_iota(jnp.int32, (L, L), 1)
    incl = ri >= ci
    strict = ri > ci
    eye = (ri == ci).astype(F32)
    bx = (ri >> 3) ^ (ci >> 3)
    blk = (bx > 0).astype(jnp.int32) + (bx > 1).astype(jnp.int32) + (bx > 3).astype(jnp.int32)
    rt = lax.broadcasted_iota(jnp.int32, (T, T), 0)
    ct = lax.broadcasted_iota(jnp.int32, (T, T), 1)
    tri = ((rt >= ct) & ((rt >> 6) == (ct >> 6))).astype(F32)
    nw = nw_ref[...]
    n_ch = T // L
    chains = [(b, cidx, h) for b in range(bsz) for cidx in range(n_ch) for h in range(H)]

    qkv, beta_all, gc_all, gc_t = [], [], [], []
    for b in range(bsz):
        outs = []
        for i, (r, w) in enumerate(((q_ref, cwq_ref), (k_ref, cwk_ref), (v_ref, cwv_ref))):
            e = ext.at[i, b]
            e[SUBLANES:SUBLANES + T, :] = r[b]
            outs.append(_silu(_causal_conv(e, w, T)))
            e[0:SUBLANES, :] = e[T:T + SUBLANES, :]
        qkv.append(outs)
        sm = sm_ref[b]
        beta_all.append(_sigmoid(sm))
        g_all = -jnp.exp(alog_ref[...]) * _softplus(sm + dtb_ref[...])
        gc = _sel_mm(tri, g_all)
        gc_all.append(gc)
        gc_t.append(gc.T)

    qs, ks, kbs, rhss, decays, egcs, glasts, gcbs = [], [], [], [], [], [], [], []
    for b, cidx, h in chains:
        sl = slice(h * GDN_DIM, (h + 1) * GDN_DIM)
        rw = slice(cidx * L, (cidx + 1) * L)
        qh, kh, vh = (a[rw, sl] for a in qkv[b])
        qh = qh * lax.rsqrt(jnp.sum(qh * qh, axis=-1, keepdims=True) + EPS) * (GDN_DIM ** -0.5)
        kh = kh * lax.rsqrt(jnp.sum(kh * kh, axis=-1, keepdims=True) + EPS)
        beta = beta_all[b][rw, SM_BETA + h:SM_BETA + h + 1]
        gcol = gc_all[b][rw, SM_DECAY + h:SM_DECAY + h + 1]
        grow = gc_t[b][SM_DECAY + h:SM_DECAY + h + 1, rw]
        decays.append(jnp.exp(jnp.where(incl, gcol - grow, NEG_INF)))
        gcb = jnp.broadcast_to(gcol, (L, GDN_DIM))
        egc = jnp.exp(gcb)
        kb = kh * beta
        qs.append(qh); ks.append(kh); kbs.append(kb); gcbs.append(gcb); egcs.append(egc)
        glasts.append(gcb[L - 1:L, :])
        rhss.append(jnp.concatenate([vh * beta, kb * egc], axis=1))

    n = len(chains)
    rng = range(n)
    kk = [_mm_nt(kbs[c], ks[c]) for c in rng]
    qk = [_mm_nt(qs[c], ks[c]) for c in rng]
    ms = [jnp.where(strict, kk[c] * decays[c], 0.0) for c in rng]
    mds = [jnp.where(blk == 0, ms[c], 0.0) for c in rng]
    p2 = [_mm(mds[c], mds[c]) for c in rng]
    base = [eye - mds[c] for c in rng]
    bp = [_mm(base[c], p2[c]) for c in rng]
    p4 = [_mm(p2[c], p2[c]) for c in rng]
    base = [base[c] + bp[c] for c in rng]
    bq = [_mm(base[c], p4[c]) for c in rng]
    inv = [base[c] + bq[c] for c in rng]
    for lvl in range(1, int(math.log2(L // SUBLANES)) + 1):
        oi = [_mm(jnp.where(blk == lvl, ms[c], 0.0), inv[c]) for c in rng]
        ioi = [_mm(inv[c], oi[c]) for c in rng]
        inv = [inv[c] - ioi[c] for c in rng]
    sol = [_mm(inv[c], rhss[c]) for c in rng]
    attn = [qk[c] * decays[c] for c in rng]
    qg = [qs[c] * egcs[c] for c in rng]
    kd = [ks[c] * jnp.exp(glasts[c] - gcbs[c]) for c in rng]
    s_cur = [state[i] for i in range(bsz * H)]
    out_parts = {}
    for cidx in range(n_ch):
        ids = [((b * n_ch + cidx) * H + h, b * H + h) for b in range(bsz) for h in range(H)]
        ws = [_mm(sol[c][:, GDN_DIM:], s_cur[s]) for c, s in ids]
        qgs = [_mm(qg[c], s_cur[s]) for c, s in ids]
        v_new = [sol[c][:, :GDN_DIM] - ws[i] for i, (c, s) in enumerate(ids)]
        av = [_mm(attn[c], v_new[i]) for i, (c, s) in enumerate(ids)]
        kv = [_mm_tn(kd[c], v_new[i]) for i, (c, s) in enumerate(ids)]
        for i, (c, s) in enumerate(ids):
            s_cur[s] = s_cur[s] * jnp.exp(glasts[c]) + kv[i]
            out_parts[c] = qgs[i] + av[i]
    for i in range(bsz * H):
        state[i] = s_cur[i]
    for b in range(bsz):
        z = z_ref[b]
        for cidx in range(n_ch):
            rw = slice(cidx * L, (cidx + 1) * L)
            outs = []
            for h in range(H):
                c = (b * n_ch + cidx) * H + h
                sl = slice(h * GDN_DIM, (h + 1) * GDN_DIM)
                outs.append(_rms_rows(out_parts[c], nw[:, sl]) * _silu(z[rw, sl]))
            o_ref[b, rw, :] = jnp.concatenate(outs, axis=1).astype(BF16)


def _gdn_call(proj3, p, bsz, seq):
    nt = seq // GDN_T
    row = lambda blk, w=512: pl.BlockSpec((bsz, GDN_T, w), lambda t: (0, t, blk))
    full = lambda shape: pl.BlockSpec(shape, lambda t: (0,) * len(shape))
    return pl.pallas_call(
        functools.partial(_gdn_kernel, bsz=bsz),
        grid=(nt,),
        in_specs=[row(U512["gdn_q"]), row(U512["gdn_k"]), row(U512["gdn_v"]), row(U512["gdn_z"]),
                  row(U256_SMALL, 256),
                  full((CONV_W, 512)), full((CONV_W, 512)), full((CONV_W, 512)),
                  full((1, SMALL_W)), full((1, SMALL_W)), full((1, 512))],
        out_specs=pl.BlockSpec((bsz, GDN_T, 512), lambda t: (0, t, 0)),
        out_shape=jax.ShapeDtypeStruct((bsz, seq, BRANCH), BF16),
        scratch_shapes=[pltpu.VMEM((3, bsz, SUBLANES + GDN_T, 512), F32),
                        pltpu.VMEM((bsz * GDN_HEADS, GDN_DIM, GDN_DIM), F32)],
        compiler_params=_params(("arbitrary",)),
    )(proj3, proj3, proj3, proj3, proj3, p["cwq"], p["cwk"], p["cwv"], p["dtb"], p["alog"], p["nw"])


GLA_T = 128
GLA_QK = GLA_HEADS * GLA_DK


def _gla_kernel(qk_ref, v_ref, r_ref, sm_ref, wg_ref, bg_ref, nw_ref, o_ref, state, pbuf, abuf):
    T = GLA_T
    C = GLA_CHUNK
    t = pl.program_id(1)

    @pl.when(t == 0)
    def _():
        state[...] = jnp.zeros_like(state)

    qk = qk_ref[...]
    q = qk[:, :GLA_QK] * (GLA_DK ** -0.5)
    k = qk[:, GLA_QK:]
    v = v_ref[...]
    pre = _mm(sm_ref[...], wg_ref[...]) + bg_ref[...]
    log_a = (jnp.minimum(pre, 0.0) - jnp.log1p(jnp.exp(-jnp.abs(pre)))) / GLA_TAU
    ri = lax.broadcasted_iota(jnp.int32, (T, T), 0)
    ci = lax.broadcasted_iota(jnp.int32, (T, T), 1)
    blocktri = ((ri >> 4) == (ci >> 4)) & (ri >= ci)
    G = _sel_mm(blocktri, log_a)

    rr = lax.broadcasted_iota(jnp.int32, (GLA_QK, BRANCH), 0)
    rc = lax.broadcasted_iota(jnp.int32, (GLA_QK, BRANCH), 1)
    red = ((rr >> 6) == (rc >> 7)).astype(BF16)
    rmod = lax.broadcasted_iota(jnp.int32, (T, GLA_QK), 0) & (C - 1)
    nc = T // C

    def chunk_row(x, jl):
        w = x.shape[1]
        x3 = x.reshape(nc, C, w)
        return jnp.broadcast_to(x3[:, jl:jl + 1, :], (nc, C, w)).reshape(T, w)

    for jl in range(C):
        ks = chunk_row(k, jl)
        gs = chunk_row(G, jl)
        qd = jnp.where(rmod >= jl, q, 0.0)
        pbuf[jl * T:(jl + 1) * T, :] = (qd * ks * jnp.exp(jnp.minimum(G - gs, 0.0))).astype(BF16)
    abuf[...] = jnp.dot(pbuf[...], red, preferred_element_type=F32)
    o_chunks = []
    for c in range(nc):
        oc = None
        for jl in range(C):
            a = abuf[jl * T + c * C:jl * T + (c + 1) * C, :]
            term = a * v_ref[c * C + jl:c * C + jl + 1, :]
            oc = term if oc is None else oc + term
        o_chunks.append(oc)
    o = jnp.concatenate(o_chunks, axis=0)

    head_of_lane = lax.broadcasted_iota(jnp.int32, (C, GLA_QK), 1) >> 6
    qgs, decs, upds = [], [], []
    for c in range(nc):
        rows = slice(c * C, (c + 1) * C)
        gc = G[rows]
        glast = gc[C - 1:C, :]
        qgs.append(q[rows] * jnp.exp(gc))
        decs.append(jnp.exp(glast))
        upds.append(_mm_tn(v[rows], k[rows] * jnp.exp(glast - gc)))
    st = state[...]
    st_in = []
    for c in range(nc):
        st_in.append(st)
        st = st * decs[c] + upds[c]
    state[...] = st
    inter = [jnp.concatenate(
        [_mm_nt(jnp.where(head_of_lane == h, qgs[c], 0.0), st_in[c][h * GLA_DV:(h + 1) * GLA_DV, :])
         for h in range(GLA_HEADS)], axis=1) for c in range(nc)]
    o = o + jnp.concatenate(inter, axis=0)
    r = r_ref[...]
    nw = nw_ref[...]
    outs = []
    for h in range(GLA_HEADS):
        sl = slice(h * GLA_DV, (h + 1) * GLA_DV)
        outs.append(_rms_rows(o[:, sl], nw[:, sl]) * _silu(r[:, sl]))
    o_ref[...] = jnp.concatenate(outs, axis=1).astype(BF16)


def _gla_call(proj, p, bsz, seq):
    nt = seq // GLA_T
    row = lambda blk: pl.BlockSpec((GLA_T, 512), lambda b, t: (b * nt + t, blk))
    full = lambda shape: pl.BlockSpec(shape, lambda b, t: (0,) * len(shape))
    return pl.pallas_call(
        _gla_kernel,
        grid=(bsz, nt),
        in_specs=[row(U512["gla_qk"]), row(U512["gla_v"]), row(U512["gla_r"]),
                  pl.BlockSpec((GLA_T, 256), lambda b, t: (b * nt + t, U256_SMALL)),
                  full((SMALL_W, GLA_QK)), full((1, GLA_QK)), full((1, 512))],
        out_specs=pl.BlockSpec((GLA_T, 512), lambda b, t: (b * nt + t, 0)),
        out_shape=jax.ShapeDtypeStruct((bsz * seq, BRANCH), BF16),
        scratch_shapes=[pltpu.VMEM((BRANCH, GLA_QK), F32),
                        pltpu.VMEM((GLA_CHUNK * GLA_T, GLA_QK), BF16),
                        pltpu.VMEM((GLA_CHUNK * GLA_T, BRANCH), F32)],
        compiler_params=_params(("arbitrary", "arbitrary")),
    )(proj, proj, proj, proj, p["wg"], p["bg"], p["nw"])


LRU_T = 256
LRU_PAD = LRU_T // 2


def _lru_kernel(x_ref, gate_ref, cw_ref, cb_ref, wa_ref, ba_ref, wx_ref, bx_ref, lam_ref, o_ref,
                ext, abuf, hbuf, carry):
    T = LRU_T
    P = LRU_PAD
    t = pl.program_id(1)

    @pl.when(t == 0)
    def _():
        ext[0:SUBLANES, :] = jnp.zeros((SUBLANES, BRANCH), F32)
        abuf[0:P, :] = jnp.ones((P, BRANCH), F32)
        hbuf[0:P, :] = jnp.zeros((P, BRANCH), F32)
        carry[...] = jnp.zeros_like(carry)

    ext[SUBLANES:SUBLANES + T, :] = x_ref[...]
    xc = _causal_conv(ext, cw_ref, T) + cb_ref[...]
    ext[0:SUBLANES, :] = ext[T:T + SUBLANES, :]
    gate_r = _sigmoid(_mm(xc, wa_ref[...]) + ba_ref[...])
    gate_i = _sigmoid(_mm(xc, wx_ref[...]) + bx_ref[...])
    log_a = -LRU_C * gate_r * _softplus(-lam_ref[...])
    abuf[P:P + T, :] = jnp.exp(log_a)
    th = jnp.tanh(log_a)
    hbuf[P:P + T, :] = jnp.sqrt(-2.0 * th / (1.0 - th)) * (gate_i * xc)
    s = 1
    while s < T:
        a_cur = abuf[P:P + T, :]
        h_cur = hbuf[P:P + T, :]
        a_sh = abuf[P - s:P - s + T, :]
        h_sh = hbuf[P - s:P - s + T, :]
        hbuf[P:P + T, :] = h_cur + a_cur * h_sh
        abuf[P:P + T, :] = a_cur * a_sh
        s *= 2
    h = hbuf[P:P + T, :] + abuf[P:P + T, :] * carry[0:1, :]
    carry[0:1, :] = h[T - 1:T, :]
    o_ref[...] = (h * _gelu(gate_ref[...])).astype(BF16)


def _lru_call(proj, p, bsz, seq):
    nt = seq // LRU_T
    row = lambda blk: pl.BlockSpec((LRU_T, 512), lambda b, t: (b * nt + t, blk))
    full = lambda shape: pl.BlockSpec(shape, lambda b, t: (0,) * len(shape))
    return pl.pallas_call(
        _lru_kernel,
        grid=(bsz, nt),
        in_specs=[row(U512["lru_x"]), row(U512["lru_gate"]),
                  full((CONV_W, 512)), full((1, 512)), full((512, 512)), full((1, 512)),
                  full((512, 512)), full((1, 512)), full((1, 512))],
        out_specs=pl.BlockSpec((LRU_T, 512), lambda b, t: (b * nt + t, 0)),
        out_shape=jax.ShapeDtypeStruct((bsz * seq, BRANCH), BF16),
        scratch_shapes=[pltpu.VMEM((SUBLANES + LRU_T, 512), F32),
                        pltpu.VMEM((LRU_PAD + LRU_T, 512), F32),
                        pltpu.VMEM((LRU_PAD + LRU_T, 512), F32),
                        pltpu.VMEM((SUBLANES, 512), F32)],
        compiler_params=_params(("arbitrary", "arbitrary")),
    )(proj, proj, p["cw"], p["cb"], p["wa"], p["ba"], p["wx"], p["bx"], p["lam"])


MRG_TM = 256


def _merge_kernel(x_ref, g_ref, lg_ref, y0_ref, y1_ref, y2_ref, y3_ref, wb_ref, wo_ref, o_ref):
    merged = None
    for i, y_ref in enumerate((y0_ref, y1_ref, y2_ref, y3_ref)):
        br = jnp.dot(y_ref[...], wb_ref[i], preferred_element_type=F32)
        term = _sigmoid(lg_ref[:, i * D_MODEL:(i + 1) * D_MODEL]) * br
        merged = term if merged is None else merged + term
    out = jnp.dot(merged.astype(BF16), wo_ref[...], preferred_element_type=F32)
    o_ref[...] = x_ref[...] + g_ref[0] * out


def _merge_call(x2, g1, proj, ys, wb, wo, seq):
    n_tok = x2.shape[0]
    per_b = seq // MRG_TM
    yspec = pl.BlockSpec((MRG_TM, BRANCH), lambda i: (i, 0))
    return pl.pallas_call(
        _merge_kernel,
        grid=(n_tok // MRG_TM,),
        in_specs=[pl.BlockSpec((MRG_TM, D_MODEL), lambda i: (i, 0)),
                  pl.BlockSpec((1, 1, D_MODEL), lambda i: (i // per_b, 0, 0)),
                  pl.BlockSpec((MRG_TM, N_BRANCH * D_MODEL), lambda i: (i, 0)),
                  yspec, yspec, yspec, yspec,
                  pl.BlockSpec((N_BRANCH, BRANCH, D_MODEL), lambda i: (0, 0, 0)),
                  pl.BlockSpec((D_MODEL, D_MODEL), lambda i: (0, 0))],
        out_specs=pl.BlockSpec((MRG_TM, D_MODEL), lambda i: (i, 0)),
        out_shape=jax.ShapeDtypeStruct((n_tok, D_MODEL), F32),
        compiler_params=_params(("arbitrary",)),
    )(x2, g1, proj, *ys, wb, wo)


PS_TB = 512
N_SCORE_ROWS = 2 * PEER_HEADS * PEER_KEYS


def _peer_score_kernel(x_ref, nw_ref, sc_ref, sh_ref, wqt_ref, keys_ref, h2t_ref, st_ref):
    h = _rms_rows(x_ref[...], nw_ref[...])
    h = h * (1.0 + sc_ref[0]) + sh_ref[0]
    ht = h.T.astype(BF16)
    h2t_ref[...] = ht
    qt = jnp.dot(wqt_ref[...], ht, preferred_element_type=F32).astype(BF16)
    for g in range(2 * PEER_HEADS):
        rows = slice(g * PEER_KEYS, (g + 1) * PEER_KEYS)
        st_ref[rows, :] = jnp.dot(keys_ref[g], qt[rows, :], preferred_element_type=F32)


def _peer_score_call(x2, nw, sc, sh, wqt, keys, seq):
    n_tok = x2.shape[0]
    per_b = seq // PS_TB
    return pl.pallas_call(
        _peer_score_kernel,
        grid=(n_tok // PS_TB,),
        in_specs=[pl.BlockSpec((PS_TB, D_MODEL), lambda i: (i, 0)),
                  pl.BlockSpec((1, D_MODEL), lambda i: (0, 0)),
                  pl.BlockSpec((1, 1, D_MODEL), lambda i: (i // per_b, 0, 0)),
                  pl.BlockSpec((1, 1, D_MODEL), lambda i: (i // per_b, 0, 0)),
                  pl.BlockSpec((N_SCORE_ROWS, D_MODEL), lambda i: (0, 0)),
                  pl.BlockSpec((2 * PEER_HEADS, PEER_KEYS, PEER_HALF), lambda i: (0, 0, 0))],
        out_specs=[pl.BlockSpec((D_MODEL, PS_TB), lambda i: (0, i)),
                   pl.BlockSpec((N_SCORE_ROWS, PS_TB), lambda i: (0, i))],
        out_shape=[jax.ShapeDtypeStruct((D_MODEL, n_tok), BF16),
                   jax.ShapeDtypeStruct((N_SCORE_ROWS, n_tok), F32)],
        compiler_params=_params(("arbitrary",)),
    )(x2, nw, sc, sh, wqt, keys)


PT_TL = 128
NOT_RANKED = 255.0
_CAND = [(i, j) for i in range(PEER_TOPK) for j in range(PEER_TOPK) if (i + 1) * (j + 1) <= PEER_TOPK]
N_CAND_ROWS = -(-len(_CAND) // SUBLANES) * SUBLANES


def _pop_max(x, iota):
    m = jnp.max(x, axis=0, keepdims=True)
    first = jnp.min(jnp.where(x == m, iota, float(x.shape[0])), axis=0, keepdims=True)
    return m, first, jnp.where(iota == first, NEG_INF, x)


def _gate_tables_head(st_ref, ce_ref, r2_ref, e2_ref, cand, h, exact_ties):
    TL = PT_TL
    iota_k = lax.broadcasted_iota(jnp.int32, (PEER_KEYS, TL), 0).astype(F32)
    iota_c = lax.broadcasted_iota(jnp.int32, (N_CAND_ROWS, TL), 0).astype(F32)
    rows1 = slice(h * PEER_KEYS, (h + 1) * PEER_KEYS)
    rows2 = slice((PEER_HEADS + h) * PEER_KEYS, (PEER_HEADS + h + 1) * PEER_KEYS)
    s1 = st_ref[rows1, :]
    s2 = st_ref[rows2, :]
    x = s1
    t1, pick1 = [], []
    for _ in range(PEER_TOPK):
        if exact_ties:
            m, f, x = _pop_max(x, iota_k)
            pick1.append(f)
        else:
            m = jnp.max(x, axis=0, keepdims=True)
            x = jnp.where(x == m, NEG_INF, x)
        t1.append(m)
    x = s2
    t2 = []
    rank2 = jnp.full((PEER_KEYS, TL), NOT_RANKED, F32)
    for r in range(PEER_TOPK):
        if exact_ties:
            m, f, x = _pop_max(x, iota_k)
            hit_rows = iota_k == f
        else:
            m = jnp.max(x, axis=0, keepdims=True)
            hit_rows = x == m
            x = jnp.where(hit_rows, NEG_INF, x)
        t2.append(m)
        rank2 = jnp.where(hit_rows, float(r), rank2)
    for n, (i, j) in enumerate(_CAND):
        cand[n:n + 1, :] = t1[i] + t2[j]
    c = cand[...]
    x = c
    tau = None
    for _ in range(PEER_TOPK):
        tau, _, x = _pop_max(x, iota_c)
    m1, m2 = t1[0], t2[0]
    zsum = jnp.sum(jnp.where(c >= tau, jnp.exp(c - (m1 + m2)), 0.0), axis=0, keepdims=True)
    count1 = jnp.zeros((PEER_KEYS, TL), F32)
    for i in range(PEER_TOPK):
        cnt = None
        for j in range(PEER_TOPK):
            if (i + 1) * (j + 1) <= PEER_TOPK:
                hit = jnp.where(t1[i] + t2[j] >= tau, 1.0, 0.0)
                cnt = hit if cnt is None else cnt + hit
        sel = (iota_k == pick1[i]) if exact_ties else (s1 == t1[i])
        count1 = jnp.where(sel, cnt, count1)
    ce_ref[rows1, :] = count1
    ce_ref[rows2, :] = jnp.exp(s1 - m1) * (0.5 / zsum)
    r2_ref[rows1, :] = rank2.astype(BF16)
    e2_ref[rows1, :] = jnp.exp(s2 - m2).astype(BF16)
    if exact_ties:
        return None
    n1 = jnp.sum(jnp.where(s1 >= t1[-1], 1.0, 0.0), axis=0, keepdims=True)
    n2 = jnp.sum(jnp.where(s2 >= t2[-1], 1.0, 0.0), axis=0, keepdims=True)
    return jnp.abs(n1 - PEER_TOPK) + jnp.abs(n2 - PEER_TOPK)


def _peer_gate_kernel(st_ref, ce_ref, r2_ref, e2_ref, cand):
    cand[...] = jnp.full((N_CAND_ROWS, PT_TL), NEG_INF, F32)
    tied = None
    for h in range(PEER_HEADS):
        t = _gate_tables_head(st_ref, ce_ref, r2_ref, e2_ref, cand, h, exact_ties=False)
        tied = t if tied is None else tied + t

    @pl.when(jnp.max(tied) > 0.0)
    def _():
        for h in range(PEER_HEADS):
            _gate_tables_head(st_ref, ce_ref, r2_ref, e2_ref, cand, h, exact_ties=True)


def _peer_gate_call(scores_t):
    n_tok = scores_t.shape[1]
    half = PEER_HEADS * PEER_KEYS
    return pl.pallas_call(
        _peer_gate_kernel,
        grid=(n_tok // PT_TL,),
        in_specs=[pl.BlockSpec((N_SCORE_ROWS, PT_TL), lambda i: (0, i))],
        out_specs=[pl.BlockSpec((N_SCORE_ROWS, PT_TL), lambda i: (0, i)),
                   pl.BlockSpec((half, PT_TL), lambda i: (0, i)),
                   pl.BlockSpec((half, PT_TL), lambda i: (0, i))],
        out_shape=[jax.ShapeDtypeStruct((N_SCORE_ROWS, n_tok), F32),
                   jax.ShapeDtypeStruct((half, n_tok), BF16),
                   jax.ShapeDtypeStruct((half, n_tok), BF16)],
        scratch_shapes=[pltpu.VMEM((N_CAND_ROWS, PT_TL), F32)],
        compiler_params=_params(("arbitrary",)),
    )(scores_t)


PE_TB = 512
PE_EB = 2048
HALF_ROWS = PEER_HEADS * PEER_KEYS


def _bcast_rows_bf16(row):
    r16 = jnp.broadcast_to(row, (2 * SUBLANES, LANES)).astype(BF16)
    return jnp.concatenate([r16] * (PEER_KEYS // (2 * SUBLANES)), axis=0)


def _peer_expert_kernel(h2t_ref, u_ref, vt_ref, ce_ref, r2_ref, e2_ref, x_ref, g_ref, fw_ref,
                        o_ref, acc, zt, *, final_norm):
    j = pl.program_id(1)
    H = PEER_HEADS

    @pl.when(j == 0)
    def _():
        acc[...] = jnp.zeros_like(acc)

    c0 = math.sqrt(2.0 / math.pi)
    zero = jnp.zeros((PEER_KEYS, LANES), BF16)
    st = jnp.dot(u_ref[0], h2t_ref[...], preferred_element_type=F32)
    for q in range(PE_EB // PEER_KEYS):
        rs = slice(q * PEER_KEYS, (q + 1) * PEER_KEYS)
        for lc in range(PE_TB // LANES):
            ls = slice(lc * LANES, (lc + 1) * LANES)
            w = None
            for h in range(H):
                rows = slice(h * PEER_KEYS, (h + 1) * PEER_KEYS)
                cnt = _bcast_rows_bf16(ce_ref[0, h, q:q + 1, ls])
                g1 = _bcast_rows_bf16(ce_ref[1, h, q:q + 1, ls])
                sel_g1 = jnp.minimum(jnp.maximum(cnt - r2_ref[rows, ls], zero), g1)
                term = sel_g1 * e2_ref[rows, ls]
                w = term if w is None else w + term
            x = st[rs, ls]
            th = jnp.tanh(x * (c0 + (c0 * 0.044715) * (x * x)))
            zt[rs, ls] = (x + x * th).astype(BF16) * w
    acc[...] += jnp.dot(vt_ref[0], zt[...], preferred_element_type=F32)

    @pl.when(j == pl.num_programs(1) - 1)
    def _():
        xn = x_ref[...] + g_ref[0] * acc[...].T
        if final_norm:
            xn = _rms_rows(xn, fw_ref[...])
        o_ref[...] = xn


def _peer_expert_call(h2, u_bf, vt_bf, layer, ce, r2, e2, x2, g2, fw, seq, final_norm):
    n_tok = x2.shape[0]
    per_b = seq // PE_TB
    kern = functools.partial(_peer_expert_kernel, final_norm=final_norm)
    ce_k = ce.reshape(2, PEER_HEADS, PEER_KEYS, n_tok)
    return pl.pallas_call(
        kern,
        grid=(n_tok // PE_TB, PEER_EXPERTS // PE_EB),
        in_specs=[pl.BlockSpec((D_MODEL, PE_TB), lambda i, j: (0, i)),
                  pl.BlockSpec((1, PE_EB, D_MODEL), lambda i, j: (layer, j, 0)),
                  pl.BlockSpec((1, D_MODEL, PE_EB), lambda i, j: (layer, 0, j)),
                  pl.BlockSpec((2, PEER_HEADS, PE_EB // PEER_KEYS, PE_TB), lambda i, j: (0, 0, j, i)),
                  pl.BlockSpec((HALF_ROWS, PE_TB), lambda i, j: (0, i)),
                  pl.BlockSpec((HALF_ROWS, PE_TB), lambda i, j: (0, i)),
                  pl.BlockSpec((PE_TB, D_MODEL), lambda i, j: (i, 0)),
                  pl.BlockSpec((1, 1, D_MODEL), lambda i, j: (i // per_b, 0, 0)),
                  pl.BlockSpec((1, D_MODEL), lambda i, j: (0, 0))],
        out_specs=pl.BlockSpec((PE_TB, D_MODEL), lambda i, j: (i, 0)),
        out_shape=jax.ShapeDtypeStruct((n_tok, D_MODEL), F32),
        scratch_shapes=[pltpu.VMEM((D_MODEL, PE_TB), F32),
                        pltpu.VMEM((PE_EB, PE_TB), BF16)],
        compiler_params=_params(("arbitrary", "arbitrary")),
    )(h2, u_bf, vt_bf, ce_k, r2, e2, x2, g2, fw)


def _pad_lanes(vec, start, width=SMALL_W):
    out = jnp.zeros((1, width), F32)
    return lax.dynamic_update_slice(out, vec.reshape(1, -1).astype(F32), (0, start))


def _block_diag(w):
    n, d, e = w.shape
    eye = jnp.eye(n, dtype=w.dtype)
    return (eye[:, None, :, None] * w[:, :, None, :]).reshape(n * d, n * e)


def kernel(x, c, w_ada, b_ada, norm_mix_w, norm_ffn_w, w_in, ssm_conv_w, ssm_conv_b, ssm_dt_bias,
           ssm_a_log, ssm_d, ssm_norm_w, gdn_conv_w, gdn_a_log, gdn_dt_bias, gdn_norm_w, gla_w_gate,
           gla_b_gate, gla_norm_w, lru_conv_w, lru_conv_b, lru_w_a, lru_b_a, lru_w_x, lru_b_x,
           lru_lambda, w_branch, w_out, peer_w_q, peer_sub_keys, peer_u, peer_v, final_norm_w):
    bsz, seq, d = x.shape
    n_layers = w_in.shape[0]
    n_tok = bsz * seq
    x2 = x.reshape(n_tok, d)

    c_pad = jnp.zeros((SUBLANES, d), F32).at[:bsz].set(c)
    mod = _ada_call(c_pad, w_ada, b_ada)

    row1 = lambda v: v.reshape(1, -1).astype(F32)
    w_perm = _permute_w_in(w_in)
    u_bf = peer_u.astype(BF16)
    vt_bf = jnp.swapaxes(peer_v, 1, 2).astype(BF16)

    for l in range(n_layers):
        m6 = mod[l, :bsz].reshape(bsz, 6, 1, d)
        sh1, sc1, g1, sh2, sc2, g2 = (m6[:, i] for i in range(6))
        proj = _inproj_call(x2, row1(norm_mix_w[l]), sc1, sh1, w_perm, l, seq)

        ssd_p = dict(cwx=ssm_conv_w[l][:, :512], cbx=row1(ssm_conv_b[l][:512]),
                     cwbc=ssm_conv_w[l][:, 512:], cbbc=row1(ssm_conv_b[l][512:]),
                     dtb=_pad_lanes(ssm_dt_bias[l], SM_DT), alog=_pad_lanes(ssm_a_log[l], SM_DT),
                     dfull=row1(jnp.repeat(ssm_d[l], SSM_HEAD_DIM)), nw=row1(ssm_norm_w[l]))
        y_ssd = _ssd_call(proj, ssd_p, bsz, seq)

        gdn_p = dict(cwq=gdn_conv_w[l][:, :512], cwk=gdn_conv_w[l][:, 512:1024],
                     cwv=gdn_conv_w[l][:, 1024:], dtb=_pad_lanes(gdn_dt_bias[l], SM_DECAY),
                     alog=_pad_lanes(gdn_a_log[l], SM_DECAY),
                     nw=row1(jnp.tile(gdn_norm_w[l], GDN_HEADS)))
        y_gdn = _gdn_call(proj.reshape(bsz, seq, N_COLS), gdn_p, bsz, seq).reshape(n_tok, BRANCH)

        wg = jnp.zeros((SMALL_W, GLA_QK), F32).at[SM_LOW:SM_LOW + GLA_RANK].set(gla_w_gate[l])
        gla_p = dict(wg=wg.astype(BF16), bg=row1(gla_b_gate[l]),
                     nw=row1(jnp.tile(gla_norm_w[l], GLA_HEADS)))
        y_gla = _gla_call(proj, gla_p, bsz, seq)

        lru_p = dict(cw=lru_conv_w[l], cb=row1(lru_conv_b[l]),
                     wa=_block_diag(lru_w_a[l]).astype(BF16), ba=row1(lru_b_a[l]),
                     wx=_block_diag(lru_w_x[l]).astype(BF16), bx=row1(lru_b_x[l]),
                     lam=row1(lru_lambda[l]))
        y_lru = _lru_call(proj, lru_p, bsz, seq)

        x2 = _merge_call(x2, g1, proj, (y_ssd, y_gdn, y_gla, y_lru),
                         w_branch[l].astype(BF16), w_out[l].astype(BF16), seq)

        wqt = peer_w_q[l].reshape(d, PEER_HEADS, 2, PEER_HALF).transpose(2, 1, 3, 0)
        wqt = wqt.reshape(N_SCORE_ROWS, d).astype(BF16)
        keys = peer_sub_keys[l].transpose(1, 0, 2, 3).reshape(2 * PEER_HEADS, PEER_KEYS, PEER_HALF)
        h2, scores_t = _peer_score_call(x2, row1(norm_ffn_w[l]), sc2, sh2, wqt, keys.astype(BF16), seq)
        ce, r2, e2 = _peer_gate_call(scores_t)
        x2 = _peer_expert_call(h2, u_bf, vt_bf, l, ce, r2, e2,
                               x2, g2, row1(final_norm_w), seq, final_norm=(l == n_layers - 1))
    return x2.reshape(bsz, seq, d)
```

```python
import functools
import math

import jax
import jax.numpy as jnp
from jax import lax
from jax.experimental import pallas as pl
from jax.experimental.pallas import tpu as pltpu

F32 = jnp.float32
BF16 = jnp.bfloat16
NEG_INF = float("-inf")

D_MODEL = 1024
N_LAYERS = 2
EPS = 1e-6
CONV_W = 4
BRANCH = 512
N_BRANCH = 4
SSM_HEADS = 8
SSM_HEAD_DIM = 64
SSM_GROUPS = 2
SSM_STATE = 64
GDN_HEADS = 4
GDN_DIM = 128
GLA_HEADS = 4
GLA_DK = 64
GLA_DV = 128
GLA_RANK = 16
GLA_TAU = 16.0
GLA_CHUNK = 16
LRU_BLOCKS = 8
LRU_BLOCK_DIM = 64
LRU_C = 8.0
PEER_HEADS = 8
PEER_KEYS = 128
PEER_EXPERTS = PEER_KEYS * PEER_KEYS
PEER_HALF = 128
PEER_TOPK = 16

LANES = 128
SUBLANES = 8
VMEM_LIMIT = 48 * 1024 * 1024

_SRC = {}
_off = 0
for _name, _w in (("ssm_z", 512), ("ssm_x", 512), ("ssm_b", 128), ("ssm_c", 128), ("ssm_dt", 8),
                  ("gdn_q", 512), ("gdn_k", 512), ("gdn_v", 512), ("gdn_z", 512), ("gdn_beta", 4),
                  ("gdn_decay", 4), ("gla_q", 256), ("gla_k", 256), ("gla_v", 512), ("gla_r", 512),
                  ("gla_low", 16), ("lru_x", 512), ("lru_gate", 512), ("merge", 4096)):
    _SRC[_name] = (_off, _w)
    _off += _w
D_IN = _off
_DST_ORDER = ("merge", "ssm_z", "ssm_x", "gdn_q", "gdn_k", "gdn_v", "gdn_z", "gla_q", "gla_k",
              "gla_v", "gla_r", "lru_x", "lru_gate", "ssm_b", "ssm_c", "ssm_dt", "gdn_beta",
              "gdn_decay", "gla_low")
SMALL_W = 256
N_COLS = 4096 + 11 * 512 + 256 + SMALL_W
U512 = {"ssm_z": 8, "ssm_x": 9, "gdn_q": 10, "gdn_k": 11, "gdn_v": 12, "gdn_z": 13, "gla_qk": 14,
        "gla_v": 15, "gla_r": 16, "lru_x": 17, "lru_gate": 18}
U256_BC = 38
U256_SMALL = 39
SM_DT = 0
SM_BETA = 8
SM_DECAY = 12
SM_LOW = 16


def _permute_w_in(w):
    w = w.astype(BF16)
    parts = [w[..., _SRC[name][0]:_SRC[name][0] + _SRC[name][1]] for name in _DST_ORDER]
    used = sum(_SRC[name][1] for name in _DST_ORDER)
    parts.append(jnp.zeros(w.shape[:-1] + (N_COLS - used,), w.dtype))
    return jnp.concatenate(parts, axis=-1)


def _mm(a, b):
    return jnp.dot(a.astype(BF16), b.astype(BF16), preferred_element_type=F32)


def _mm_nt(a, b):
    return lax.dot_general(a.astype(BF16), b.astype(BF16), (((1,), (1,)), ((), ())),
                           preferred_element_type=F32)


def _mm_tn(a, b):
    return lax.dot_general(a.astype(BF16), b.astype(BF16), (((0,), (0,)), ((), ())),
                           preferred_element_type=F32)


def _split3(x):
    hi = x.astype(BF16)
    r = x - hi.astype(F32)
    mid = r.astype(BF16)
    lo = (r - mid.astype(F32)).astype(BF16)
    return hi, mid, lo


def _sel_mm(sel, x):
    s = sel.astype(BF16)
    hi, mid, lo = _split3(x)
    return (jnp.dot(s, lo, preferred_element_type=F32) + jnp.dot(s, mid, preferred_element_type=F32)
            + jnp.dot(s, hi, preferred_element_type=F32))


def _mm_sel(x, sel):
    s = sel.astype(BF16)
    hi, mid, lo = _split3(x)
    return (jnp.dot(lo, s, preferred_element_type=F32) + jnp.dot(mid, s, preferred_element_type=F32)
            + jnp.dot(hi, s, preferred_element_type=F32))


def _sigmoid(x):
    return 1.0 / (1.0 + jnp.exp(-x))


def _silu(x):
    return x * _sigmoid(x)


def _softplus(x):
    return jnp.maximum(x, 0.0) + jnp.log1p(jnp.exp(-jnp.abs(x)))


def _gelu(x):
    c = math.sqrt(2.0 / math.pi)
    return 0.5 * x * (1.0 + jnp.tanh(c * (x + 0.044715 * (x * x * x))))


def _rms_rows(x, w):
    ms = jnp.mean(x * x, axis=-1, keepdims=True)
    return x * lax.rsqrt(ms + EPS) * w


def _causal_conv(ext_ref, w_ref, n_rows):
    acc = None
    for k in range(CONV_W):
        term = w_ref[k:k + 1, :] * ext_ref[SUBLANES - CONV_W + 1 + k:SUBLANES - CONV_W + 1 + k + n_rows, :]
        acc = term if acc is None else acc + term
    return acc


def _params(sem, flags=None):
    return pltpu.CompilerParams(dimension_semantics=sem, vmem_limit_bytes=VMEM_LIMIT, flags=flags)


ADA_TN = 1536


def _ada_kernel(c_ref, w_ref, b_ref, o_ref):
    c = c_ref[...]
    o_ref[0] = _mm(_silu(c), w_ref[0]) + b_ref[0]


def _ada_call(c_pad, w_ada, b_ada):
    n_l = w_ada.shape[0]
    n_out = w_ada.shape[2]
    return pl.pallas_call(
        _ada_kernel,
        grid=(n_l, n_out // ADA_TN),
        in_specs=[pl.BlockSpec((SUBLANES, D_MODEL), lambda l, j: (0, 0)),
                  pl.BlockSpec((1, D_MODEL, ADA_TN), lambda l, j: (l, 0, j)),
                  pl.BlockSpec((1, 1, ADA_TN), lambda l, j: (l, 0, j))],
        out_specs=pl.BlockSpec((1, SUBLANES, ADA_TN), lambda l, j: (l, 0, j)),
        out_shape=jax.ShapeDtypeStruct((n_l, SUBLANES, n_out), F32),
        compiler_params=_params(("arbitrary", "arbitrary")),
    )(c_pad, w_ada, b_ada.reshape(n_l, 1, n_out))


INP_TM = 1024
INP_TN = 2048


def _inproj_kernel(x_ref, nw_ref, sc_ref, sh_ref, w_ref, o_ref, h_scr):
    @pl.when(pl.program_id(1) == 0)
    def _():
        h = _rms_rows(x_ref[...], nw_ref[...])
        h = h * (1.0 + sc_ref[0]) + sh_ref[0]
        h_scr[...] = h.astype(BF16)

    o_ref[...] = jnp.dot(h_scr[...], w_ref[0], preferred_element_type=F32)


def _inproj_call(x2, nw, sc, sh, w_perm, layer, seq):
    n_tok = x2.shape[0]
    per_b = seq // INP_TM
    return pl.pallas_call(
        _inproj_kernel,
        grid=(n_tok // INP_TM, N_COLS // INP_TN),
        in_specs=[pl.BlockSpec((INP_TM, D_MODEL), lambda i, j: (i, 0)),
                  pl.BlockSpec((1, D_MODEL), lambda i, j: (0, 0)),
                  pl.BlockSpec((1, 1, D_MODEL), lambda i, j: (i // per_b, 0, 0)),
                  pl.BlockSpec((1, 1, D_MODEL), lambda i, j: (i // per_b, 0, 0)),
                  pl.BlockSpec((1, D_MODEL, INP_TN), lambda i, j: (layer, 0, j))],
        out_specs=pl.BlockSpec((INP_TM, INP_TN), lambda i, j: (i, j)),
        out_shape=jax.ShapeDtypeStruct((n_tok, N_COLS), F32),
        scratch_shapes=[pltpu.VMEM((INP_TM, D_MODEL), BF16)],
        compiler_params=_params(("arbitrary", "arbitrary")),
    )(x2, nw, sc, sh, w_perm)


SSD_L = 128


def _ssd_kernel(z_ref, xs_ref, bc_ref, sm_ref, cwx_ref, cbx_ref, cwbc_ref, cbbc_ref, dtb_ref,
                alog_ref, dfull_ref, nw_ref, o_ref, extx, extbc, state):
    L = SSD_L
    t = pl.program_id(1)

    @pl.when(t == 0)
    def _():
        extx[0:SUBLANES, :] = jnp.zeros((SUBLANES, BRANCH), F32)
        extbc[0:SUBLANES, :] = jnp.zeros((SUBLANES, 256), F32)
        state[...] = jnp.zeros_like(state)

    extx[SUBLANES:SUBLANES + L, :] = xs_ref[...]
    extbc[SUBLANES:SUBLANES + L, :] = bc_ref[...]
    xs = _silu(_causal_conv(extx, cwx_ref, L) + cbx_ref[...])
    bc = _silu(_causal_conv(extbc, cwbc_ref, L) + cbbc_ref[...])
    extx[0:SUBLANES, :] = extx[L:L + SUBLANES, :]
    extbc[0:SUBLANES, :] = extbc[L:L + SUBLANES, :]

    dt = _softplus(sm_ref[...] + dtb_ref[...])
    da = dt * (-jnp.exp(alog_ref[...]))
    ri = lax.broadcasted_iota(jnp.int32, (L, L), 0)
    ci = lax.broadcasted_iota(jnp.int32, (L, L), 1)
    tril = ri >= ci
    cs = _sel_mm(tril, da)
    er = lax.broadcasted_iota(jnp.int32, (SMALL_W, BRANCH), 0)
    ec = lax.broadcasted_iota(jnp.int32, (SMALL_W, BRANCH), 1)
    expand = (er == (ec >> 6)).astype(F32)
    cs_full = _mm_sel(cs, expand)
    dt_full = _mm_sel(dt, expand)
    ecs_full = jnp.exp(cs_full)
    cs_last = cs_full[L - 1:L, :]
    w_full = jnp.exp(cs_last - cs_full)
    xdt = xs * dt_full
    xdtw = xdt * w_full
    cs_t = cs.T

    b128 = bc[:, 0:LANES]
    c128 = bc[:, LANES:2 * LANES]
    lane = lax.broadcasted_iota(jnp.int32, (1, LANES), 1)
    cg = [jnp.where(lane < SSM_STATE, c128, 0.0), jnp.where(lane >= SSM_STATE, c128, 0.0)]
    cb = [_mm_nt(cg[g], b128) for g in range(SSM_GROUPS)]

    y_pairs = []
    for p in range(SSM_HEADS // 2):
        xp = xdt[:, p * LANES:(p + 1) * LANES]
        yp = None
        for hh in range(2):
            h = 2 * p + hh
            g = h // (SSM_HEADS // SSM_GROUPS)
            col = cs[:, h:h + 1]
            row = cs_t[h:h + 1, :]
            seg = jnp.exp(jnp.where(tril, col - row, NEG_INF))
            att = cb[g] * seg
            hm = (lane < SSM_HEAD_DIM) if hh == 0 else (lane >= SSM_HEAD_DIM)
            term = _mm(att, jnp.where(hm, xp, 0.0))
            yp = term if yp is None else yp + term
        y_pairs.append(yp)
    y_diag = jnp.concatenate(y_pairs, axis=1)

    y_offs = []
    for g in range(SSM_GROUPS):
        sl = slice(g * 256, (g + 1) * 256)
        s_in = state[g]
        y_offs.append(_mm(cg[g], s_in) * ecs_full[:, sl])
        new = _mm_tn(b128, xdtw[:, sl])
        state[g] = s_in * ecs_full[L - 1:L, sl] + new
    y = y_diag + jnp.concatenate(y_offs, axis=1) + xs * dfull_ref[...]
    y = y * _silu(z_ref[...])
    o_ref[...] = _rms_rows(y, nw_ref[...]).astype(BF16)


def _ssd_call(proj, p, bsz, seq):
    nt = seq // SSD_L
    row = lambda blk: pl.BlockSpec((SSD_L, 512), lambda b, t: (b * nt + t, blk))
    full = lambda shape: pl.BlockSpec(shape, lambda b, t: (0,) * len(shape))
    return pl.pallas_call(
        _ssd_kernel,
        grid=(bsz, nt),
        in_specs=[row(U512["ssm_z"]), row(U512["ssm_x"]),
                  pl.BlockSpec((SSD_L, 256), lambda b, t: (b * nt + t, U256_BC)),
                  pl.BlockSpec((SSD_L, 256), lambda b, t: (b * nt + t, U256_SMALL)),
                  full((CONV_W, 512)), full((1, 512)), full((CONV_W, 256)), full((1, 256)),
                  full((1, SMALL_W)), full((1, SMALL_W)), full((1, 512)), full((1, 512))],
        out_specs=pl.BlockSpec((SSD_L, 512), lambda b, t: (b * nt + t, 0)),
        out_shape=jax.ShapeDtypeStruct((bsz * seq, BRANCH), BF16),
        scratch_shapes=[pltpu.VMEM((SUBLANES + SSD_L, 512), F32),
                        pltpu.VMEM((SUBLANES + SSD_L, 256), F32),
                        pltpu.VMEM((SSM_GROUPS, LANES, 256), F32)],
        compiler_params=_params(("arbitrary", "arbitrary")),
    )(proj, proj, proj, proj, p["cwx"], p["cbx"], p["cwbc"], p["cbbc"], p["dtb"], p["alog"],
      p["dfull"], p["nw"])


GDN_L = 64
GDN_T = 256


def _gdn_kernel(q_ref, k_ref, v_ref, z_ref, sm_ref, cwq_ref, cwk_ref, cwv_ref, dtb_ref, alog_ref,
                nw_ref, o_ref, ext, state, *, bsz):
    L = GDN_L
    T = GDN_T
    H = GDN_HEADS
    t = pl.program_id(0)

    @pl.when(t == 0)
    def _():
        ext[:, :, 0:SUBLANES, :] = jnp.zeros((3, bsz, SUBLANES, BRANCH), F32)
        state[...] = jnp.zeros_like(state)

    ri = lax.broadcasted_iota(jnp.int32, (L, L), 0)
    ci = lax.broadcasted_iota(jnp.int32, (L, L), 1)
    incl = ri >= ci
    strict = ri > ci
    eye = (ri == ci).astype(F32)
    bx = (ri >> 3) ^ (ci >> 3)
    blk = (bx > 0).astype(jnp.int32) + (bx > 1).astype(jnp.int32) + (bx > 3).astype(jnp.int32)
    rt = lax.broadcasted_iota(jnp.int32, (T, T), 0)
    ct = lax.broadcasted_iota(jnp.int32, (T, T), 1)
    tri = ((rt >= ct) & ((rt >> 6) == (ct >> 6))).astype(F32)
    nw = nw_ref[...]
    n_ch = T // L
    chains = [(b, cidx, h) for b in range(bsz) for cidx in range(n_ch) for h in range(H)]

    qkv, beta_all, gc_all, gc_t = [], [], [], []
    for b in range(bsz):
        outs = []
        for i, (r, w) in enumerate(((q_ref, cwq_ref), (k_ref, cwk_ref), (v_ref, cwv_ref))):
            e = ext.at[i, b]
            e[SUBLANES:SUBLANES + T, :] = r[b]
            outs.append(_silu(_causal_conv(e, w, T)))
            e[0:SUBLANES, :] = e[T:T + SUBLANES, :]
        qkv.append(outs)
        sm = sm_ref[b]
        beta_all.append(_sigmoid(sm))
        g_all = -jnp.exp(alog_ref[...]) * _softplus(sm + dtb_ref[...])
        gc = _sel_mm(tri, g_all)
        gc_all.append(gc)
        gc_t.append(gc.T)

    qs, ks, kbs, rhss, decays, egcs, glasts, gcbs = [], [], [], [], [], [], [], []
    for b, cidx, h in chains:
        sl = slice(h * GDN_DIM, (h + 1) * GDN_DIM)
        rw = slice(cidx * L, (cidx + 1) * L)
        qh, kh, vh = (a[rw, sl] for a in qkv[b])
        qh = qh * lax.rsqrt(jnp.sum(qh * qh, axis=-1, keepdims=True) + EPS) * (GDN_DIM ** -0.5)
        kh = kh * lax.rsqrt(jnp.sum(kh * kh, axis=-1, keepdims=True) + EPS)
        beta = beta_all[b][rw, SM_BETA + h:SM_BETA + h + 1]
        gcol = gc_all[b][rw, SM_DECAY + h:SM_DECAY + h + 1]
        grow = gc_t[b][SM_DECAY + h:SM_DECAY + h + 1, rw]
        decays.append(jnp.exp(jnp.where(incl, gcol - grow, NEG_INF)))
        gcb = jnp.broadcast_to(gcol, (L, GDN_DIM))
        egc = jnp.exp(gcb)
        kb = kh * beta
        qs.append(qh); ks.append(kh); kbs.append(kb); gcbs.append(gcb); egcs.append(egc)
        glasts.append(gcb[L - 1:L, :])
        rhss.append(jnp.concatenate([vh * beta, kb * egc], axis=1))

    n = len(chains)
    rng = range(n)
    kk = [_mm_nt(kbs[c], ks[c]) for c in rng]
    qk = [_mm_nt(qs[c], ks[c]) for c in rng]
    ms = [jnp.where(strict, kk[c] * decays[c], 0.0) for c in rng]
    mds = [jnp.where(blk == 0, ms[c], 0.0) for c in rng]
    p2 = [_mm(mds[c], mds[c]) for c in rng]
    base = [eye - mds[c] for c in rng]
    bp = [_mm(base[c], p2[c]) for c in rng]
    p4 = [_mm(p2[c], p2[c]) for c in rng]
    base = [base[c] + bp[c] for c in rng]
    bq = [_mm(base[c], p4[c]) for c in rng]
    inv = [base[c] + bq[c] for c in rng]
    for lvl in range(1, int(math.log2(L // SUBLANES)) + 1):
        oi = [_mm(jnp.where(blk == lvl, ms[c], 0.0), inv[c]) for c in rng]
        ioi = [_mm(inv[c], oi[c]) for c in rng]
        inv = [inv[c] - ioi[c] for c in rng]
    sol = [_mm(inv[c], rhss[c]) for c in rng]
    attn = [qk[c] * decays[c] for c in rng]
    qg = [qs[c] * egcs[c] for c in rng]
    kd = [ks[c] * jnp.exp(glasts[c] - gcbs[c]) for c in rng]
    s_cur = [state[i] for i in range(bsz * H)]
    out_parts = {}
    for cidx in range(n_ch):
        ids = [((b * n_ch + cidx) * H + h, b * H + h) for b in range(bsz) for h in range(H)]
        ws = [_mm(sol[c][:, GDN_DIM:], s_cur[s]) for c, s in ids]
        qgs = [_mm(qg[c], s_cur[s]) for c, s in ids]
        v_new = [sol[c][:, :GDN_DIM] - ws[i] for i, (c, s) in enumerate(ids)]
        av = [_mm(attn[c], v_new[i]) for i, (c, s) in enumerate(ids)]
        kv = [_mm_tn(kd[c], v_new[i]) for i, (c, s) in enumerate(ids)]
        for i, (c, s) in enumerate(ids):
            s_cur[s] = s_cur[s] * jnp.exp(glasts[c]) + kv[i]
            out_parts[c] = qgs[i] + av[i]
    for i in range(bsz * H):
        state[i] = s_cur[i]
    for b in range(bsz):
        z = z_ref[b]
        for cidx in range(n_ch):
            rw = slice(cidx * L, (cidx + 1) * L)
            outs = []
            for h in range(H):
                c = (b * n_ch + cidx) * H + h
                sl = slice(h * GDN_DIM, (h + 1) * GDN_DIM)
                outs.append(_rms_rows(out_parts[c], nw[:, sl]) * _silu(z[rw, sl]))
            o_ref[b, rw, :] = jnp.concatenate(outs, axis=1).astype(BF16)


def _gdn_call(proj3, p, bsz, seq):
    nt = seq // GDN_T
    row = lambda blk, w=512: pl.BlockSpec((bsz, GDN_T, w), lambda t: (0, t, blk))
    full = lambda shape: pl.BlockSpec(shape, lambda t: (0,) * len(shape))
    return pl.pallas_call(
        functools.partial(_gdn_kernel, bsz=bsz),
        grid=(nt,),
        in_specs=[row(U512["gdn_q"]), row(U512["gdn_k"]), row(U512["gdn_v"]), row(U512["gdn_z"]),
                  row(U256_SMALL, 256),
                  full((CONV_W, 512)), full((CONV_W, 512)), full((CONV_W, 512)),
                  full((1, SMALL_W)), full((1, SMALL_W)), full((1, 512))],
        out_specs=pl.BlockSpec((bsz, GDN_T, 512), lambda t: (0, t, 0)),
        out_shape=jax.ShapeDtypeStruct((bsz, seq, BRANCH), BF16),
        scratch_shapes=[pltpu.VMEM((3, bsz, SUBLANES + GDN_T, 512), F32),
                        pltpu.VMEM((bsz * GDN_HEADS, GDN_DIM, GDN_DIM), F32)],
        compiler_params=_params(("arbitrary",)),
    )(proj3, proj3, proj3, proj3, proj3, p["cwq"], p["cwk"], p["cwv"], p["dtb"], p["alog"], p["nw"])


GLA_T = 256
GLA_QK = GLA_HEADS * GLA_DK


def _gla_kernel(qk_ref, v_ref, r_ref, sm_ref, wg_ref, bg_ref, nw_ref, o_ref, state, pbuf, abuf):
    T = GLA_T
    C = GLA_CHUNK
    t = pl.program_id(1)

    @pl.when(t == 0)
    def _():
        state[...] = jnp.zeros_like(state)

    qk = qk_ref[...]
    q = qk[:, :GLA_QK] * (GLA_DK ** -0.5)
    k = qk[:, GLA_QK:]
    v = v_ref[...]
    pre = _mm(sm_ref[...], wg_ref[...]) + bg_ref[...]
    log_a = (jnp.minimum(pre, 0.0) - jnp.log1p(jnp.exp(-jnp.abs(pre)))) / GLA_TAU
    ri = lax.broadcasted_iota(jnp.int32, (T, T), 0)
    ci = lax.broadcasted_iota(jnp.int32, (T, T), 1)
    blocktri = ((ri >> 4) == (ci >> 4)) & (ri >= ci)
    G = _sel_mm(blocktri, log_a)

    rr = lax.broadcasted_iota(jnp.int32, (GLA_QK, BRANCH), 0)
    rc = lax.broadcasted_iota(jnp.int32, (GLA_QK, BRANCH), 1)
    red = ((rr >> 6) == (rc >> 7)).astype(BF16)
    rmod = lax.broadcasted_iota(jnp.int32, (T, GLA_QK), 0) & (C - 1)
    nc = T // C

    def chunk_row(x, jl):
        w = x.shape[1]
        x3 = x.reshape(nc, C, w)
        return jnp.broadcast_to(x3[:, jl:jl + 1, :], (nc, C, w)).reshape(T, w)

    for jl in range(C):
        ks = chunk_row(k, jl)
        gs = chunk_row(G, jl)
        qd = jnp.where(rmod >= jl, q, 0.0)
        pbuf[jl * T:(jl + 1) * T, :] = (qd * ks * jnp.exp(jnp.minimum(G - gs, 0.0))).astype(BF16)
    abuf[...] = jnp.dot(pbuf[...], red, preferred_element_type=F32)
    o_chunks = []
    for c in range(nc):
        oc = None
        for jl in range(C):
            a = abuf[jl * T + c * C:jl * T + (c + 1) * C, :]
            term = a * v_ref[c * C + jl:c * C + jl + 1, :]
            oc = term if oc is None else oc + term
        o_chunks.append(oc)
    o = jnp.concatenate(o_chunks, axis=0)

    head_of_lane = lax.broadcasted_iota(jnp.int32, (C, GLA_QK), 1) >> 6
    qgs, decs, upds = [], [], []
    for c in range(nc):
        rows = slice(c * C, (c + 1) * C)
        gc = G[rows]
        glast = gc[C - 1:C, :]
        qgs.append(q[rows] * jnp.exp(gc))
        decs.append(jnp.exp(glast))
        upds.append(_mm_tn(v[rows], k[rows] * jnp.exp(glast - gc)))
    st = state[...]
    st_in = []
    for c in range(nc):
        st_in.append(st)
        st = st * decs[c] + upds[c]
    state[...] = st
    inter = [jnp.concatenate(
        [_mm_nt(jnp.where(head_of_lane == h, qgs[c], 0.0), st_in[c][h * GLA_DV:(h + 1) * GLA_DV, :])
         for h in range(GLA_HEADS)], axis=1) for c in range(nc)]
    o = o + jnp.concatenate(inter, axis=0)
    r = r_ref[...]
    nw = nw_ref[...]
    outs = []
    for h in range(GLA_HEADS):
        sl = slice(h * GLA_DV, (h + 1) * GLA_DV)
        outs.append(_rms_rows(o[:, sl], nw[:, sl]) * _silu(r[:, sl]))
    o_ref[...] = jnp.concatenate(outs, axis=1).astype(BF16)


def _gla_call(proj, p, bsz, seq):
    nt = seq // GLA_T
    row = lambda blk: pl.BlockSpec((GLA_T, 512), lambda b, t: (b * nt + t, blk))
    full = lambda shape: pl.BlockSpec(shape, lambda b, t: (0,) * len(shape))
    return pl.pallas_call(
        _gla_kernel,
        grid=(bsz, nt),
        in_specs=[row(U512["gla_qk"]), row(U512["gla_v"]), row(U512["gla_r"]),
                  pl.BlockSpec((GLA_T, 256), lambda b, t: (b * nt + t, U256_SMALL)),
                  full((SMALL_W, GLA_QK)), full((1, GLA_QK)), full((1, 512))],
        out_specs=pl.BlockSpec((GLA_T, 512), lambda b, t: (b * nt + t, 0)),
        out_shape=jax.ShapeDtypeStruct((bsz * seq, BRANCH), BF16),
        scratch_shapes=[pltpu.VMEM((BRANCH, GLA_QK), F32),
                        pltpu.VMEM((GLA_CHUNK * GLA_T, GLA_QK), BF16),
                        pltpu.VMEM((GLA_CHUNK * GLA_T, BRANCH), F32)],
        compiler_params=_params(("arbitrary", "arbitrary")),
    )(proj, proj, proj, proj, p["wg"], p["bg"], p["nw"])


LRU_T = 256
LRU_PAD = LRU_T // 2


def _lru_kernel(x_ref, gate_ref, cw_ref, cb_ref, wa_ref, ba_ref, wx_ref, bx_ref, lam_ref, o_ref,
                ext, abuf, hbuf, carry):
    T = LRU_T
    P = LRU_PAD
    t = pl.program_id(1)

    @pl.when(t == 0)
    def _():
        ext[0:SUBLANES, :] = jnp.zeros((SUBLANES, BRANCH), F32)
        abuf[0:P, :] = jnp.ones((P, BRANCH), F32)
        hbuf[0:P, :] = jnp.zeros((P, BRANCH), F32)
        carry[...] = jnp.zeros_like(carry)

    ext[SUBLANES:SUBLANES + T, :] = x_ref[...]
    xc = _causal_conv(ext, cw_ref, T) + cb_ref[...]
    ext[0:SUBLANES, :] = ext[T:T + SUBLANES, :]
    gate_r = _sigmoid(_mm(xc, wa_ref[...]) + ba_ref[...])
    gate_i = _sigmoid(_mm(xc, wx_ref[...]) + bx_ref[...])
    log_a = -LRU_C * gate_r * _softplus(-lam_ref[...])
    abuf[P:P + T, :] = jnp.exp(log_a)
    th = jnp.tanh(log_a)
    hbuf[P:P + T, :] = jnp.sqrt(-2.0 * th / (1.0 - th)) * (gate_i * xc)
    s = 1
    while s < T:
        a_cur = abuf[P:P + T, :]
        h_cur = hbuf[P:P + T, :]
        a_sh = abuf[P - s:P - s + T, :]
        h_sh = hbuf[P - s:P - s + T, :]
        hbuf[P:P + T, :] = h_cur + a_cur * h_sh
        abuf[P:P + T, :] = a_cur * a_sh
        s *= 2
    h = hbuf[P:P + T, :] + abuf[P:P + T, :] * carry[0:1, :]
    carry[0:1, :] = h[T - 1:T, :]
    o_ref[...] = (h * _gelu(gate_ref[...])).astype(BF16)


def _lru_call(proj, p, bsz, seq):
    nt = seq // LRU_T
    row = lambda blk: pl.BlockSpec((LRU_T, 512), lambda b, t: (b * nt + t, blk))
    full = lambda shape: pl.BlockSpec(shape, lambda b, t: (0,) * len(shape))
    return pl.pallas_call(
        _lru_kernel,
        grid=(bsz, nt),
        in_specs=[row(U512["lru_x"]), row(U512["lru_gate"]),
                  full((CONV_W, 512)), full((1, 512)), full((512, 512)), full((1, 512)),
                  full((512, 512)), full((1, 512)), full((1, 512))],
        out_specs=pl.BlockSpec((LRU_T, 512), lambda b, t: (b * nt + t, 0)),
        out_shape=jax.ShapeDtypeStruct((bsz * seq, BRANCH), BF16),
        scratch_shapes=[pltpu.VMEM((SUBLANES + LRU_T, 512), F32),
                        pltpu.VMEM((LRU_PAD + LRU_T, 512), F32),
                        pltpu.VMEM((LRU_PAD + LRU_T, 512), F32),
                        pltpu.VMEM((SUBLANES, 512), F32)],
        compiler_params=_params(("arbitrary", "arbitrary")),
    )(proj, proj, p["cw"], p["cb"], p["wa"], p["ba"], p["wx"], p["bx"], p["lam"])


MRG_TM = 512


def _merge_kernel(x_ref, g_ref, lg_ref, y0_ref, y1_ref, y2_ref, y3_ref, wb_ref, wo_ref, o_ref):
    merged = None
    for i, y_ref in enumerate((y0_ref, y1_ref, y2_ref, y3_ref)):
        br = jnp.dot(y_ref[...], wb_ref[i], preferred_element_type=F32)
        term = _sigmoid(lg_ref[:, i * D_MODEL:(i + 1) * D_MODEL]) * br
        merged = term if merged is None else merged + term
    out = jnp.dot(merged.astype(BF16), wo_ref[...], preferred_element_type=F32)
    o_ref[...] = x_ref[...] + g_ref[0] * out


def _merge_call(x2, g1, proj, ys, wb, wo, seq):
    n_tok = x2.shape[0]
    per_b = seq // MRG_TM
    yspec = pl.BlockSpec((MRG_TM, BRANCH), lambda i: (i, 0))
    return pl.pallas_call(
        _merge_kernel,
        grid=(n_tok // MRG_TM,),
        in_specs=[pl.BlockSpec((MRG_TM, D_MODEL), lambda i: (i, 0)),
                  pl.BlockSpec((1, 1, D_MODEL), lambda i: (i // per_b, 0, 0)),
                  pl.BlockSpec((MRG_TM, N_BRANCH * D_MODEL), lambda i: (i, 0)),
                  yspec, yspec, yspec, yspec,
                  pl.BlockSpec((N_BRANCH, BRANCH, D_MODEL), lambda i: (0, 0, 0)),
                  pl.BlockSpec((D_MODEL, D_MODEL), lambda i: (0, 0))],
        out_specs=pl.BlockSpec((MRG_TM, D_MODEL), lambda i: (i, 0)),
        out_shape=jax.ShapeDtypeStruct((n_tok, D_MODEL), F32),
        compiler_params=_params(("arbitrary",)),
    )(x2, g1, proj, *ys, wb, wo)


PS_TB = 512
N_SCORE_ROWS = 2 * PEER_HEADS * PEER_KEYS


def _peer_score_kernel(x_ref, nw_ref, sc_ref, sh_ref, wqt_ref, keys_ref, h2t_ref, st_ref):
    h = _rms_rows(x_ref[...], nw_ref[...])
    h = h * (1.0 + sc_ref[0]) + sh_ref[0]
    ht = h.T.astype(BF16)
    h2t_ref[...] = ht
    qt = jnp.dot(wqt_ref[...], ht, preferred_element_type=F32).astype(BF16)
    for g in range(2 * PEER_HEADS):
        rows = slice(g * PEER_KEYS, (g + 1) * PEER_KEYS)
        st_ref[rows, :] = jnp.dot(keys_ref[g], qt[rows, :], preferred_element_type=F32)


def _peer_score_call(x2, nw, sc, sh, wqt, keys, seq):
    n_tok = x2.shape[0]
    per_b = seq // PS_TB
    return pl.pallas_call(
        _peer_score_kernel,
        grid=(n_tok // PS_TB,),
        in_specs=[pl.BlockSpec((PS_TB, D_MODEL), lambda i: (i, 0)),
                  pl.BlockSpec((1, D_MODEL), lambda i: (0, 0)),
                  pl.BlockSpec((1, 1, D_MODEL), lambda i: (i // per_b, 0, 0)),
                  pl.BlockSpec((1, 1, D_MODEL), lambda i: (i // per_b, 0, 0)),
                  pl.BlockSpec((N_SCORE_ROWS, D_MODEL), lambda i: (0, 0)),
                  pl.BlockSpec((2 * PEER_HEADS, PEER_KEYS, PEER_HALF), lambda i: (0, 0, 0))],
        out_specs=[pl.BlockSpec((D_MODEL, PS_TB), lambda i: (0, i)),
                   pl.BlockSpec((N_SCORE_ROWS, PS_TB), lambda i: (0, i))],
        out_shape=[jax.ShapeDtypeStruct((D_MODEL, n_tok), BF16),
                   jax.ShapeDtypeStruct((N_SCORE_ROWS, n_tok), F32)],
        compiler_params=_params(("arbitrary",)),
    )(x2, nw, sc, sh, wqt, keys)


PT_TL = 128
NOT_RANKED = 255.0
_CAND = [(i, j) for i in range(PEER_TOPK) for j in range(PEER_TOPK) if (i + 1) * (j + 1) <= PEER_TOPK]
N_CAND_ROWS = -(-len(_CAND) // SUBLANES) * SUBLANES


def _pop_max(x, iota):
    m = jnp.max(x, axis=0, keepdims=True)
    first = jnp.min(jnp.where(x == m, iota, float(x.shape[0])), axis=0, keepdims=True)
    return m, first, jnp.where(iota == first, NEG_INF, x)


def _gate_tables_head(st_ref, ce_ref, r2_ref, e2_ref, cand, h, exact_ties):
    TL = PT_TL
    iota_k = lax.broadcasted_iota(jnp.int32, (PEER_KEYS, TL), 0).astype(F32)
    iota_c = lax.broadcasted_iota(jnp.int32, (N_CAND_ROWS, TL), 0).astype(F32)
    rows1 = slice(h * PEER_KEYS, (h + 1) * PEER_KEYS)
    rows2 = slice((PEER_HEADS + h) * PEER_KEYS, (PEER_HEADS + h + 1) * PEER_KEYS)
    s1 = st_ref[rows1, :]
    s2 = st_ref[rows2, :]
    x = s1
    t1, pick1 = [], []
    for _ in range(PEER_TOPK):
        if exact_ties:
            m, f, x = _pop_max(x, iota_k)
            pick1.append(f)
        else:
            m = jnp.max(x, axis=0, keepdims=True)
            x = jnp.where(x == m, NEG_INF, x)
        t1.append(m)
    x = s2
    t2 = []
    rank2 = jnp.full((PEER_KEYS, TL), NOT_RANKED, F32)
    for r in range(PEER_TOPK):
        if exact_ties:
            m, f, x = _pop_max(x, iota_k)
            hit_rows = iota_k == f
        else:
            m = jnp.max(x, axis=0, keepdims=True)
            hit_rows = x == m
            x = jnp.where(hit_rows, NEG_INF, x)
        t2.append(m)
        rank2 = jnp.where(hit_rows, float(r), rank2)
    for n, (i, j) in enumerate(_CAND):
        cand[n:n + 1, :] = t1[i] + t2[j]
    c = cand[...]
    x = c
    tau = None
    for _ in range(PEER_TOPK):
        tau, _, x = _pop_max(x, iota_c)
    m1, m2 = t1[0], t2[0]
    zsum = jnp.sum(jnp.where(c >= tau, jnp.exp(c - (m1 + m2)), 0.0), axis=0, keepdims=True)
    count1 = jnp.zeros((PEER_KEYS, TL), F32)
    for i in range(PEER_TOPK):
        cnt = None
        for j in range(PEER_TOPK):
            if (i + 1) * (j + 1) <= PEER_TOPK:
                hit = jnp.where(t1[i] + t2[j] >= tau, 1.0, 0.0)
                cnt = hit if cnt is None else cnt + hit
        sel = (iota_k == pick1[i]) if exact_ties else (s1 == t1[i])
        count1 = jnp.where(sel, cnt, count1)
    ce_ref[rows1, :] = count1
    ce_ref[rows2, :] = jnp.exp(s1 - m1) * (0.5 / zsum)
    r2_ref[rows1, :] = rank2.astype(BF16)
    e2_ref[rows1, :] = jnp.exp(s2 - m2).astype(BF16)
    if exact_ties:
        return None
    n1 = jnp.sum(jnp.where(s1 >= t1[-1], 1.0, 0.0), axis=0, keepdims=True)
    n2 = jnp.sum(jnp.where(s2 >= t2[-1], 1.0, 0.0), axis=0, keepdims=True)
    return jnp.abs(n1 - PEER_TOPK) + jnp.abs(n2 - PEER_TOPK)


def _peer_gate_kernel(st_ref, ce_ref, r2_ref, e2_ref, cand):
    cand[...] = jnp.full((N_CAND_ROWS, PT_TL), NEG_INF, F32)
    tied = None
    for h in range(PEER_HEADS):
        t = _gate_tables_head(st_ref, ce_ref, r2_ref, e2_ref, cand, h, exact_ties=False)
        tied = t if tied is None else tied + t

    @pl.when(jnp.max(tied) > 0.0)
    def _():
        for h in range(PEER_HEADS):
            _gate_tables_head(st_ref, ce_ref, r2_ref, e2_ref, cand, h, exact_ties=True)


def _peer_gate_call(scores_t):
    n_tok = scores_t.shape[1]
    half = PEER_HEADS * PEER_KEYS
    return pl.pallas_call(
        _peer_gate_kernel,
        grid=(n_tok // PT_TL,),
        in_specs=[pl.BlockSpec((N_SCORE_ROWS, PT_TL), lambda i: (0, i))],
        out_specs=[pl.BlockSpec((N_SCORE_ROWS, PT_TL), lambda i: (0, i)),
                   pl.BlockSpec((half, PT_TL), lambda i: (0, i)),
                   pl.BlockSpec((half, PT_TL), lambda i: (0, i))],
        out_shape=[jax.ShapeDtypeStruct((N_SCORE_ROWS, n_tok), F32),
                   jax.ShapeDtypeStruct((half, n_tok), BF16),
                   jax.ShapeDtypeStruct((half, n_tok), BF16)],
        scratch_shapes=[pltpu.VMEM((N_CAND_ROWS, PT_TL), F32)],
        compiler_params=_params(("arbitrary",)),
    )(scores_t)


PE_TB = 512
PE_EB = 2048
HALF_ROWS = PEER_HEADS * PEER_KEYS


def _bcast_rows_bf16(row):
    r16 = jnp.broadcast_to(row, (2 * SUBLANES, LANES)).astype(BF16)
    return jnp.concatenate([r16] * (PEER_KEYS // (2 * SUBLANES)), axis=0)


def _peer_expert_kernel(h2t_ref, u_ref, vt_ref, ce_ref, r2_ref, e2_ref, x_ref, g_ref, fw_ref,
                        o_ref, acc, zt, *, final_norm):
    j = pl.program_id(1)
    H = PEER_HEADS

    @pl.when(j == 0)
    def _():
        acc[...] = jnp.zeros_like(acc)

    c0 = math.sqrt(2.0 / math.pi)
    zero = jnp.zeros((PEER_KEYS, LANES), BF16)
    st = jnp.dot(u_ref[0], h2t_ref[...], preferred_element_type=F32)
    for q in range(PE_EB // PEER_KEYS):
        rs = slice(q * PEER_KEYS, (q + 1) * PEER_KEYS)
        for lc in range(PE_TB // LANES):
            ls = slice(lc * LANES, (lc + 1) * LANES)
            w = None
            for h in range(H):
                rows = slice(h * PEER_KEYS, (h + 1) * PEER_KEYS)
                cnt = _bcast_rows_bf16(ce_ref[0, h, q:q + 1, ls])
                g1 = _bcast_rows_bf16(ce_ref[1, h, q:q + 1, ls])
                sel_g1 = jnp.minimum(jnp.maximum(cnt - r2_ref[rows, ls], zero), g1)
                term = sel_g1 * e2_ref[rows, ls]
                w = term if w is None else w + term
            x = st[rs, ls]
            th = jnp.tanh(x * (c0 + (c0 * 0.044715) * (x * x)))
            zt[rs, ls] = (x + x * th).astype(BF16) * w
    acc[...] += jnp.dot(vt_ref[0], zt[...], preferred_element_type=F32)

    @pl.when(j == pl.num_programs(1) - 1)
    def _():
        xn = x_ref[...] + g_ref[0] * acc[...].T
        if final_norm:
            xn = _rms_rows(xn, fw_ref[...])
        o_ref[...] = xn


def _peer_expert_call(h2, u_bf, vt_bf, layer, ce, r2, e2, x2, g2, fw, seq, final_norm):
    n_tok = x2.shape[0]
    per_b = seq // PE_TB
    kern = functools.partial(_peer_expert_kernel, final_norm=final_norm)
    ce_k = ce.reshape(2, PEER_HEADS, PEER_KEYS, n_tok)
    return pl.pallas_call(
        kern,
        grid=(n_tok // PE_TB, PEER_EXPERTS // PE_EB),
        in_specs=[pl.BlockSpec((D_MODEL, PE_TB), lambda i, j: (0, i)),
                  pl.BlockSpec((1, PE_EB, D_MODEL), lambda i, j: (layer, j, 0)),
                  pl.BlockSpec((1, D_MODEL, PE_EB), lambda i, j: (layer, 0, j)),
                  pl.BlockSpec((2, PEER_HEADS, PE_EB // PEER_KEYS, PE_TB), lambda i, j: (0, 0, j, i)),
                  pl.BlockSpec((HALF_ROWS, PE_TB), lambda i, j: (0, i)),
                  pl.BlockSpec((HALF_ROWS, PE_TB), lambda i, j: (0, i)),
                  pl.BlockSpec((PE_TB, D_MODEL), lambda i, j: (i, 0)),
                  pl.BlockSpec((1, 1, D_MODEL), lambda i, j: (i // per_b, 0, 0)),
                  pl.BlockSpec((1, D_MODEL), lambda i, j: (0, 0))],
        out_specs=pl.BlockSpec((PE_TB, D_MODEL), lambda i, j: (i, 0)),
        out_shape=jax.ShapeDtypeStruct((n_tok, D_MODEL), F32),
        scratch_shapes=[pltpu.VMEM((D_MODEL, PE_TB), F32),
                        pltpu.VMEM((PE_EB, PE_TB), BF16)],
        compiler_params=_params(("arbitrary", "arbitrary")),
    )(h2, u_bf, vt_bf, ce_k, r2, e2, x2, g2, fw)


def _pad_lanes(vec, start, width=SMALL_W):
    out = jnp.zeros((1, width), F32)
    return lax.dynamic_update_slice(out, vec.reshape(1, -1).astype(F32), (0, start))


def _block_diag(w):
    n, d, e = w.shape
    eye = jnp.eye(n, dtype=w.dtype)
    return (eye[:, None, :, None] * w[:, :, None, :]).reshape(n * d, n * e)


def kernel(x, c, w_ada, b_ada, norm_mix_w, norm_ffn_w, w_in, ssm_conv_w, ssm_conv_b, ssm_dt_bias,
           ssm_a_log, ssm_d, ssm_norm_w, gdn_conv_w, gdn_a_log, gdn_dt_bias, gdn_norm_w, gla_w_gate,
           gla_b_gate, gla_norm_w, lru_conv_w, lru_conv_b, lru_w_a, lru_b_a, lru_w_x, lru_b_x,
           lru_lambda, w_branch, w_out, peer_w_q, peer_sub_keys, peer_u, peer_v, final_norm_w):
    bsz, seq, d = x.shape
    n_layers = w_in.shape[0]
    n_tok = bsz * seq
    x2 = x.reshape(n_tok, d)

    c_pad = jnp.zeros((SUBLANES, d), F32).at[:bsz].set(c)
    mod = _ada_call(c_pad, w_ada, b_ada)

    row1 = lambda v: v.reshape(1, -1).astype(F32)
    w_perm = _permute_w_in(w_in)
    u_bf = peer_u.astype(BF16)
    vt_bf = jnp.swapaxes(peer_v, 1, 2).astype(BF16)

    for l in range(n_layers):
        m6 = mod[l, :bsz].reshape(bsz, 6, 1, d)
        sh1, sc1, g1, sh2, sc2, g2 = (m6[:, i] for i in range(6))
        proj = _inproj_call(x2, row1(norm_mix_w[l]), sc1, sh1, w_perm, l, seq)

        ssd_p = dict(cwx=ssm_conv_w[l][:, :512], cbx=row1(ssm_conv_b[l][:512]),
                     cwbc=ssm_conv_w[l][:, 512:], cbbc=row1(ssm_conv_b[l][512:]),
                     dtb=_pad_lanes(ssm_dt_bias[l], SM_DT), alog=_pad_lanes(ssm_a_log[l], SM_DT),
                     dfull=row1(jnp.repeat(ssm_d[l], SSM_HEAD_DIM)), nw=row1(ssm_norm_w[l]))
        y_ssd = _ssd_call(proj, ssd_p, bsz, seq)

        gdn_p = dict(cwq=gdn_conv_w[l][:, :512], cwk=gdn_conv_w[l][:, 512:1024],
                     cwv=gdn_conv_w[l][:, 1024:], dtb=_pad_lanes(gdn_dt_bias[l], SM_DECAY),
                     alog=_pad_lanes(gdn_a_log[l], SM_DECAY),
                     nw=row1(jnp.tile(gdn_norm_w[l], GDN_HEADS)))
        y_gdn = _gdn_call(proj.reshape(bsz, seq, N_COLS), gdn_p, bsz, seq).reshape(n_tok, BRANCH)

        wg = jnp.zeros((SMALL_W, GLA_QK), F32).at[SM_LOW:SM_LOW + GLA_RANK].set(gla_w_gate[l])
        gla_p = dict(wg=wg.astype(BF16), bg=row1(gla_b_gate[l]),
                     nw=row1(jnp.tile(gla_norm_w[l], GLA_HEADS)))
        y_gla = _gla_call(proj, gla_p, bsz, seq)

        lru_p = dict(cw=lru_conv_w[l], cb=row1(lru_conv_b[l]),
                     wa=_block_diag(lru_w_a[l]).astype(BF16), ba=row1(lru_b_a[l]),
                     wx=_block_diag(lru_w_x[l]).astype(BF16), bx=row1(lru_b_x[l]),
                     lam=row1(lru_lambda[l]))
        y_lru = _lru_call(proj, lru_p, bsz, seq)

        x2 = _merge_call(x2, g1, proj, (y_ssd, y_gdn, y_gla, y_lru),
                         w_branch[l].astype(BF16), w_out[l].astype(BF16), seq)

        wqt = peer_w_q[l].reshape(d, PEER_HEADS, 2, PEER_HALF).transpose(2, 1, 3, 0)
        wqt = wqt.reshape(N_SCORE_ROWS, d).astype(BF16)
        keys = peer_sub_keys[l].transpose(1, 0, 2, 3).reshape(2 * PEER_HEADS, PEER_KEYS, PEER_HALF)
        h2, scores_t = _peer_score_call(x2, row1(norm_ffn_w[l]), sc2, sh2, wqt, keys.astype(BF16), seq)
        ce, r2, e2 = _peer_gate_call(scores_t)
        x2 = _peer_expert_call(h2, u_bf, vt_bf, l, ce, r2, e2,
                               x2, g2, row1(final_norm_w), seq, final_norm=(l == n_layers - 1))
    return x2.reshape(bsz, seq, d)
```

```python
import functools
import math

import jax
import jax.numpy as jnp
from jax import lax
from jax.experimental import pallas as pl
from jax.experimental.pallas import tpu as pltpu

F32 = jnp.float32
BF16 = jnp.bfloat16
NEG_INF = float("-inf")

D_MODEL = 1024
N_LAYERS = 2
EPS = 1e-6
CONV_W = 4
BRANCH = 512
N_BRANCH = 4
SSM_HEADS = 8
SSM_HEAD_DIM = 64
SSM_GROUPS = 2
SSM_STATE = 64
GDN_HEADS = 4
GDN_DIM = 128
GLA_HEADS = 4
GLA_DK = 64
GLA_DV = 128
GLA_RANK = 16
GLA_TAU = 16.0
GLA_CHUNK = 16
LRU_BLOCKS = 8
LRU_BLOCK_DIM = 64
LRU_C = 8.0
PEER_HEADS = 8
PEER_KEYS = 128
PEER_EXPERTS = PEER_KEYS * PEER_KEYS
PEER_HALF = 128
PEER_TOPK = 16

LANES = 128
SUBLANES = 8
VMEM_LIMIT = 48 * 1024 * 1024

_SRC = {}
_off = 0
for _name, _w in (("ssm_z", 512), ("ssm_x", 512), ("ssm_b", 128), ("ssm_c", 128), ("ssm_dt", 8),
                  ("gdn_q", 512), ("gdn_k", 512), ("gdn_v", 512), ("gdn_z", 512), ("gdn_beta", 4),
                  ("gdn_decay", 4), ("gla_q", 256), ("gla_k", 256), ("gla_v", 512), ("gla_r", 512),
                  ("gla_low", 16), ("lru_x", 512), ("lru_gate", 512), ("merge", 4096)):
    _SRC[_name] = (_off, _w)
    _off += _w
D_IN = _off
_DST_ORDER = ("merge", "ssm_z", "ssm_x", "gdn_q", "gdn_k", "gdn_v", "gdn_z", "gla_q", "gla_k",
              "gla_v", "gla_r", "lru_x", "lru_gate", "ssm_b", "ssm_c", "ssm_dt", "gdn_beta",
              "gdn_decay", "gla_low")
SMALL_W = 256
N_COLS = 4096 + 11 * 512 + 256 + SMALL_W
U512 = {"ssm_z": 8, "ssm_x": 9, "gdn_q": 10, "gdn_k": 11, "gdn_v": 12, "gdn_z": 13, "gla_qk": 14,
        "gla_v": 15, "gla_r": 16, "lru_x": 17, "lru_gate": 18}
U256_BC = 38
U256_SMALL = 39
SM_DT = 0
SM_BETA = 8
SM_DECAY = 12
SM_LOW = 16


def _permute_w_in(w):
    w = w.astype(BF16)
    parts = [w[..., _SRC[name][0]:_SRC[name][0] + _SRC[name][1]] for name in _DST_ORDER]
    used = sum(_SRC[name][1] for name in _DST_ORDER)
    parts.append(jnp.zeros(w.shape[:-1] + (N_COLS - used,), w.dtype))
    return jnp.concatenate(parts, axis=-1)


def _mm(a, b):
    return jnp.dot(a.astype(BF16), b.astype(BF16), preferred_element_type=F32)


def _mm_nt(a, b):
    return lax.dot_general(a.astype(BF16), b.astype(BF16), (((1,), (1,)), ((), ())),
                           preferred_element_type=F32)


def _mm_tn(a, b):
    return lax.dot_general(a.astype(BF16), b.astype(BF16), (((0,), (0,)), ((), ())),
                           preferred_element_type=F32)


def _split3(x):
    hi = x.astype(BF16)
    r = x - hi.astype(F32)
    mid = r.astype(BF16)
    lo = (r - mid.astype(F32)).astype(BF16)
    return hi, mid, lo


def _sel_mm(sel, x):
    s = sel.astype(BF16)
    hi, mid, lo = _split3(x)
    return (jnp.dot(s, lo, preferred_element_type=F32) + jnp.dot(s, mid, preferred_element_type=F32)
            + jnp.dot(s, hi, preferred_element_type=F32))


def _mm_sel(x, sel):
    s = sel.astype(BF16)
    hi, mid, lo = _split3(x)
    return (jnp.dot(lo, s, preferred_element_type=F32) + jnp.dot(mid, s, preferred_element_type=F32)
            + jnp.dot(hi, s, preferred_element_type=F32))


def _sigmoid(x):
    return 1.0 / (1.0 + jnp.exp(-x))


def _silu(x):
    return x * _sigmoid(x)


def _softplus(x):
    return jnp.maximum(x, 0.0) + jnp.log1p(jnp.exp(-jnp.abs(x)))


def _gelu(x):
    c = math.sqrt(2.0 / math.pi)
    return 0.5 * x * (1.0 + jnp.tanh(c * (x + 0.044715 * (x * x * x))))


def _rms_rows(x, w):
    ms = jnp.mean(x * x, axis=-1, keepdims=True)
    return x * lax.rsqrt(ms + EPS) * w


def _causal_conv(ext_ref, w_ref, n_rows):
    acc = None
    for k in range(CONV_W):
        term = w_ref[k:k + 1, :] * ext_ref[SUBLANES - CONV_W + 1 + k:SUBLANES - CONV_W + 1 + k + n_rows, :]
        acc = term if acc is None else acc + term
    return acc


def _params(sem, flags=None):
    return pltpu.CompilerParams(dimension_semantics=sem, vmem_limit_bytes=VMEM_LIMIT, flags=flags)


ADA_TN = 1536


def _ada_kernel(c_ref, w_ref, b_ref, o_ref):
    c = c_ref[...]
    o_ref[0] = _mm(_silu(c), w_ref[0]) + b_ref[0]


def _ada_call(c_pad, w_ada, b_ada):
    n_l = w_ada.shape[0]
    n_out = w_ada.shape[2]
    return pl.pallas_call(
        _ada_kernel,
        grid=(n_l, n_out // ADA_TN),
        in_specs=[pl.BlockSpec((SUBLANES, D_MODEL), lambda l, j: (0, 0)),
                  pl.BlockSpec((1, D_MODEL, ADA_TN), lambda l, j: (l, 0, j)),
                  pl.BlockSpec((1, 1, ADA_TN), lambda l, j: (l, 0, j))],
        out_specs=pl.BlockSpec((1, SUBLANES, ADA_TN), lambda l, j: (l, 0, j)),
        out_shape=jax.ShapeDtypeStruct((n_l, SUBLANES, n_out), F32),
        compiler_params=_params(("arbitrary", "arbitrary")),
    )(c_pad, w_ada, b_ada.reshape(n_l, 1, n_out))


INP_TM = 1024
INP_TN = 2048


def _inproj_kernel(x_ref, nw_ref, sc_ref, sh_ref, w_ref, o_ref, h_scr):
    @pl.when(pl.program_id(1) == 0)
    def _():
        h = _rms_rows(x_ref[...], nw_ref[...])
        h = h * (1.0 + sc_ref[0]) + sh_ref[0]
        h_scr[...] = h.astype(BF16)

    o_ref[...] = jnp.dot(h_scr[...], w_ref[0], preferred_element_type=F32)


def _inproj_call(x2, nw, sc, sh, w_perm, layer, seq):
    n_tok = x2.shape[0]
    per_b = seq // INP_TM
    return pl.pallas_call(
        _inproj_kernel,
        grid=(n_tok // INP_TM, N_COLS // INP_TN),
        in_specs=[pl.BlockSpec((INP_TM, D_MODEL), lambda i, j: (i, 0)),
                  pl.BlockSpec((1, D_MODEL), lambda i, j: (0, 0)),
                  pl.BlockSpec((1, 1, D_MODEL), lambda i, j: (i // per_b, 0, 0)),
                  pl.BlockSpec((1, 1, D_MODEL), lambda i, j: (i // per_b, 0, 0)),
                  pl.BlockSpec((1, D_MODEL, INP_TN), lambda i, j: (layer, 0, j))],
        out_specs=pl.BlockSpec((INP_TM, INP_TN), lambda i, j: (i, j)),
        out_shape=jax.ShapeDtypeStruct((n_tok, N_COLS), F32),
        scratch_shapes=[pltpu.VMEM((INP_TM, D_MODEL), BF16)],
        compiler_params=_params(("parallel", "arbitrary")),
    )(x2, nw, sc, sh, w_perm)


SSD_L = 128


def _ssd_kernel(z_ref, xs_ref, bc_ref, sm_ref, cwx_ref, cbx_ref, cwbc_ref, cbbc_ref, dtb_ref,
                alog_ref, dfull_ref, nw_ref, o_ref, extx, extbc, state):
    L = SSD_L
    t = pl.program_id(1)

    @pl.when(t == 0)
    def _():
        extx[0:SUBLANES, :] = jnp.zeros((SUBLANES, BRANCH), F32)
        extbc[0:SUBLANES, :] = jnp.zeros((SUBLANES, 256), F32)
        state[...] = jnp.zeros_like(state)

    extx[SUBLANES:SUBLANES + L, :] = xs_ref[...]
    extbc[SUBLANES:SUBLANES + L, :] = bc_ref[...]
    xs = _silu(_causal_conv(extx, cwx_ref, L) + cbx_ref[...])
    bc = _silu(_causal_conv(extbc, cwbc_ref, L) + cbbc_ref[...])
    extx[0:SUBLANES, :] = extx[L:L + SUBLANES, :]
    extbc[0:SUBLANES, :] = extbc[L:L + SUBLANES, :]

    dt = _softplus(sm_ref[...] + dtb_ref[...])
    da = dt * (-jnp.exp(alog_ref[...]))
    ri = lax.broadcasted_iota(jnp.int32, (L, L), 0)
    ci = lax.broadcasted_iota(jnp.int32, (L, L), 1)
    tril = ri >= ci
    cs = _sel_mm(tril, da)
    er = lax.broadcasted_iota(jnp.int32, (SMALL_W, BRANCH), 0)
    ec = lax.broadcasted_iota(jnp.int32, (SMALL_W, BRANCH), 1)
    expand = (er == (ec >> 6)).astype(F32)
    cs_full = _mm_sel(cs, expand)
    dt_full = _mm_sel(dt, expand)
    ecs_full = jnp.exp(cs_full)
    cs_last = cs_full[L - 1:L, :]
    w_full = jnp.exp(cs_last - cs_full)
    xdt = xs * dt_full
    xdtw = xdt * w_full
    cs_t = cs.T

    b128 = bc[:, 0:LANES]
    c128 = bc[:, LANES:2 * LANES]
    lane = lax.broadcasted_iota(jnp.int32, (1, LANES), 1)
    cg = [jnp.where(lane < SSM_STATE, c128, 0.0), jnp.where(lane >= SSM_STATE, c128, 0.0)]
    cb = [_mm_nt(cg[g], b128) for g in range(SSM_GROUPS)]

    y_pairs = []
    for p in range(SSM_HEADS // 2):
        xp = xdt[:, p * LANES:(p + 1) * LANES]
        yp = None
        for hh in range(2):
            h = 2 * p + hh
            g = h // (SSM_HEADS // SSM_GROUPS)
            col = cs[:, h:h + 1]
            row = cs_t[h:h + 1, :]
            seg = jnp.exp(jnp.where(tril, col - row, NEG_INF))
            att = cb[g] * seg
            hm = (lane < SSM_HEAD_DIM) if hh == 0 else (lane >= SSM_HEAD_DIM)
            term = _mm(att, jnp.where(hm, xp, 0.0))
            yp = term if yp is None else yp + term
        y_pairs.append(yp)
    y_diag = jnp.concatenate(y_pairs, axis=1)

    y_offs = []
    for g in range(SSM_GROUPS):
        sl = slice(g * 256, (g + 1) * 256)
        s_in = state[g]
        y_offs.append(_mm(cg[g], s_in) * ecs_full[:, sl])
        new = _mm_tn(b128, xdtw[:, sl])
        state[g] = s_in * ecs_full[L - 1:L, sl] + new
    y = y_diag + jnp.concatenate(y_offs, axis=1) + xs * dfull_ref[...]
    y = y * _silu(z_ref[...])
    o_ref[...] = _rms_rows(y, nw_ref[...]).astype(BF16)


def _ssd_call(proj, p, bsz, seq):
    nt = seq // SSD_L
    row = lambda blk: pl.BlockSpec((SSD_L, 512), lambda b, t: (b * nt + t, blk))
    full = lambda shape: pl.BlockSpec(shape, lambda b, t: (0,) * len(shape))
    return pl.pallas_call(
        _ssd_kernel,
        grid=(bsz, nt),
        in_specs=[row(U512["ssm_z"]), row(U512["ssm_x"]),
                  pl.BlockSpec((SSD_L, 256), lambda b, t: (b * nt + t, U256_BC)),
                  pl.BlockSpec((SSD_L, 256), lambda b, t: (b * nt + t, U256_SMALL)),
                  full((CONV_W, 512)), full((1, 512)), full((CONV_W, 256)), full((1, 256)),
                  full((1, SMALL_W)), full((1, SMALL_W)), full((1, 512)), full((1, 512))],
        out_specs=pl.BlockSpec((SSD_L, 512), lambda b, t: (b * nt + t, 0)),
        out_shape=jax.ShapeDtypeStruct((bsz * seq, BRANCH), BF16),
        scratch_shapes=[pltpu.VMEM((SUBLANES + SSD_L, 512), F32),
                        pltpu.VMEM((SUBLANES + SSD_L, 256), F32),
                        pltpu.VMEM((SSM_GROUPS, LANES, 256), F32)],
        compiler_params=_params(("arbitrary", "arbitrary")),
    )(proj, proj, proj, proj, p["cwx"], p["cbx"], p["cwbc"], p["cbbc"], p["dtb"], p["alog"],
      p["dfull"], p["nw"])


GDN_L = 64
GDN_T = 256


def _gdn_kernel(q_ref, k_ref, v_ref, z_ref, sm_ref, cwq_ref, cwk_ref, cwv_ref, dtb_ref, alog_ref,
                nw_ref, o_ref, ext, state, *, bsz):
    L = GDN_L
    T = GDN_T
    H = GDN_HEADS
    t = pl.program_id(0)

    @pl.when(t == 0)
    def _():
        ext[:, :, 0:SUBLANES, :] = jnp.zeros((3, bsz, SUBLANES, BRANCH), F32)
        state[...] = jnp.zeros_like(state)

    ri = lax.broadcasted_iota(jnp.int32, (L, L), 0)
    ci = lax.broadcasted_iota(jnp.int32, (L, L), 1)
    incl = ri >= ci
    strict = ri > ci
    eye = (ri == ci).astype(F32)
    bx = (ri >> 3) ^ (ci >> 3)
    blk = (bx > 0).astype(jnp.int32) + (bx > 1).astype(jnp.int32) + (bx > 3).astype(jnp.int32)
    rt = lax.broadcasted_iota(jnp.int32, (T, T), 0)
    ct = lax.broadcasted_iota(jnp.int32, (T, T), 1)
    tri = ((rt >= ct) & ((rt >> 6) == (ct >> 6))).astype(F32)
    nw = nw_ref[...]
    n_ch = T // L
    chains = [(b, cidx, h) for b in range(bsz) for cidx in range(n_ch) for h in range(H)]

    qkv, beta_all, gc_all, gc_t = [], [], [], []
    for b in range(bsz):
        outs = []
        for i, (r, w) in enumerate(((q_ref, cwq_ref), (k_ref, cwk_ref), (v_ref, cwv_ref))):
            e = ext.at[i, b]
            e[SUBLANES:SUBLANES + T, :] = r[b]
            outs.append(_silu(_causal_conv(e, w, T)))
            e[0:SUBLANES, :] = e[T:T + SUBLANES, :]
        qkv.append(outs)
        sm = sm_ref[b]
        beta_all.append(_sigmoid(sm))
        g_all = -jnp.exp(alog_ref[...]) * _softplus(sm + dtb_ref[...])
        gc = _sel_mm(tri, g_all)
        gc_all.append(gc)
        gc_t.append(gc.T)

    qs, ks, kbs, rhss, decays, egcs, glasts, gcbs = [], [], [], [], [], [], [], []
    for b, cidx, h in chains:
        sl = slice(h * GDN_DIM, (h + 1) * GDN_DIM)
        rw = slice(cidx * L, (cidx + 1) * L)
        qh, kh, vh = (a[rw, sl] for a in qkv[b])
        qh = qh * lax.rsqrt(jnp.sum(qh * qh, axis=-1, keepdims=True) + EPS) * (GDN_DIM ** -0.5)
        kh = kh * lax.rsqrt(jnp.sum(kh * kh, axis=-1, keepdims=True) + EPS)
        beta = beta_all[b][rw, SM_BETA + h:SM_BETA + h + 1]
        gcol = gc_all[b][rw, SM_DECAY + h:SM_DECAY + h + 1]
        grow = gc_t[b][SM_DECAY + h:SM_DECAY + h + 1, rw]
        decays.append(jnp.exp(jnp.where(incl, gcol - grow, NEG_INF)))
        gcb = jnp.broadcast_to(gcol, (L, GDN_DIM))
        egc = jnp.exp(gcb)
        kb = kh * beta
        qs.append(qh); ks.append(kh); kbs.append(kb); gcbs.append(gcb); egcs.append(egc)
        glasts.append(gcb[L - 1:L, :])
        rhss.append(jnp.concatenate([vh * beta, kb * egc], axis=1))

    n = len(chains)
    rng = range(n)
    kk = [_mm_nt(kbs[c], ks[c]) for c in rng]
    qk = [_mm_nt(qs[c], ks[c]) for c in rng]
    ms = [jnp.where(strict, kk[c] * decays[c], 0.0) for c in rng]
    mds = [jnp.where(blk == 0, ms[c], 0.0) for c in rng]
    p2 = [_mm(mds[c], mds[c]) for c in rng]
    base = [eye - mds[c] for c in rng]
    bp = [_mm(base[c], p2[c]) for c in rng]
    p4 = [_mm(p2[c], p2[c]) for c in rng]
    base = [base[c] + bp[c] for c in rng]
    bq = [_mm(base[c], p4[c]) for c in rng]
    inv = [base[c] + bq[c] for c in rng]
    for lvl in range(1, int(math.log2(L // SUBLANES)) + 1):
        oi = [_mm(jnp.where(blk == lvl, ms[c], 0.0), inv[c]) for c in rng]
        ioi = [_mm(inv[c], oi[c]) for c in rng]
        inv = [inv[c] - ioi[c] for c in rng]
    sol = [_mm(inv[c], rhss[c]) for c in rng]
    attn = [qk[c] * decays[c] for c in rng]
    qg = [qs[c] * egcs[c] for c in rng]
    kd = [ks[c] * jnp.exp(glasts[c] - gcbs[c]) for c in rng]
    s_cur = [state[i] for i in range(bsz * H)]
    out_parts = {}
    for cidx in range(n_ch):
        ids = [((b * n_ch + cidx) * H + h, b * H + h) for b in range(bsz) for h in range(H)]
        ws = [_mm(sol[c][:, GDN_DIM:], s_cur[s]) for c, s in ids]
        qgs = [_mm(qg[c], s_cur[s]) for c, s in ids]
        v_new = [sol[c][:, :GDN_DIM] - ws[i] for i, (c, s) in enumerate(ids)]
        av = [_mm(attn[c], v_new[i]) for i, (c, s) in enumerate(ids)]
        kv = [_mm_tn(kd[c], v_new[i]) for i, (c, s) in enumerate(ids)]
        for i, (c, s) in enumerate(ids):
            s_cur[s] = s_cur[s] * jnp.exp(glasts[c]) + kv[i]
            out_parts[c] = qgs[i] + av[i]
    for i in range(bsz * H):
        state[i] = s_cur[i]
    for b in range(bsz):
        z = z_ref[b]
        for cidx in range(n_ch):
            rw = slice(cidx * L, (cidx + 1) * L)
            outs = []
            for h in range(H):
                c = (b * n_ch + cidx) * H + h
                sl = slice(h * GDN_DIM, (h + 1) * GDN_DIM)
                outs.append(_rms_rows(out_parts[c], nw[:, sl]) * _silu(z[rw, sl]))
            o_ref[b, rw, :] = jnp.concatenate(outs, axis=1).astype(BF16)


def _gdn_call(proj3, p, bsz, seq):
    nt = seq // GDN_T
    row = lambda blk, w=512: pl.BlockSpec((bsz, GDN_T, w), lambda t: (0, t, blk))
    full = lambda shape: pl.BlockSpec(shape, lambda t: (0,) * len(shape))
    return pl.pallas_call(
        functools.partial(_gdn_kernel, bsz=bsz),
        grid=(nt,),
        in_specs=[row(U512["gdn_q"]), row(U512["gdn_k"]), row(U512["gdn_v"]), row(U512["gdn_z"]),
                  row(U256_SMALL, 256),
                  full((CONV_W, 512)), full((CONV_W, 512)), full((CONV_W, 512)),
                  full((1, SMALL_W)), full((1, SMALL_W)), full((1, 512))],
        out_specs=pl.BlockSpec((bsz, GDN_T, 512), lambda t: (0, t, 0)),
        out_shape=jax.ShapeDtypeStruct((bsz, seq, BRANCH), BF16),
        scratch_shapes=[pltpu.VMEM((3, bsz, SUBLANES + GDN_T, 512), F32),
                        pltpu.VMEM((bsz * GDN_HEADS, GDN_DIM, GDN_DIM), F32)],
        compiler_params=_params(("arbitrary",)),
    )(proj3, proj3, proj3, proj3, proj3, p["cwq"], p["cwk"], p["cwv"], p["dtb"], p["alog"], p["nw"])


GLA_T = 256
GLA_QK = GLA_HEADS * GLA_DK


def _gla_kernel(qk_ref, v_ref, r_ref, sm_ref, wg_ref, bg_ref, nw_ref, o_ref, state, pbuf, abuf):
    T = GLA_T
    C = GLA_CHUNK
    t = pl.program_id(1)

    @pl.when(t == 0)
    def _():
        state[...] = jnp.zeros_like(state)

    qk = qk_ref[...]
    q = qk[:, :GLA_QK] * (GLA_DK ** -0.5)
    k = qk[:, GLA_QK:]
    v = v_ref[...]
    pre = _mm(sm_ref[...], wg_ref[...]) + bg_ref[...]
    log_a = (jnp.minimum(pre, 0.0) - jnp.log1p(jnp.exp(-jnp.abs(pre)))) / GLA_TAU
    ri = lax.broadcasted_iota(jnp.int32, (T, T), 0)
    ci = lax.broadcasted_iota(jnp.int32, (T, T), 1)
    blocktri = ((ri >> 4) == (ci >> 4)) & (ri >= ci)
    G = _sel_mm(blocktri, log_a)

    rr = lax.broadcasted_iota(jnp.int32, (GLA_QK, BRANCH), 0)
    rc = lax.broadcasted_iota(jnp.int32, (GLA_QK, BRANCH), 1)
    red = ((rr >> 6) == (rc >> 7)).astype(BF16)
    rmod = lax.broadcasted_iota(jnp.int32, (T, GLA_QK), 0) & (C - 1)
    nc = T // C

    def chunk_row(x, jl):
        w = x.shape[1]
        x3 = x.reshape(nc, C, w)
        return jnp.broadcast_to(x3[:, jl:jl + 1, :], (nc, C, w)).reshape(T, w)

    for jl in range(C):
        ks = chunk_row(k, jl)
        gs = chunk_row(G, jl)
        qd = jnp.where(rmod >= jl, q, 0.0)
        pbuf[jl * T:(jl + 1) * T, :] = (qd * ks * jnp.exp(jnp.minimum(G - gs, 0.0))).astype(BF16)
    abuf[...] = jnp.dot(pbuf[...], red, preferred_element_type=F32)
    o_chunks = []
    for c in range(nc):
        oc = None
        for jl in range(C):
            a = abuf[jl * T + c * C:jl * T + (c + 1) * C, :]
            term = a * v_ref[c * C + jl:c * C + jl + 1, :]
            oc = term if oc is None else oc + term
        o_chunks.append(oc)
    o = jnp.concatenate(o_chunks, axis=0)

    head_of_lane = lax.broadcasted_iota(jnp.int32, (C, GLA_QK), 1) >> 6
    qgs, decs, upds = [], [], []
    for c in range(nc):
        rows = slice(c * C, (c + 1) * C)
        gc = G[rows]
        glast = gc[C - 1:C, :]
        qgs.append(q[rows] * jnp.exp(gc))
        decs.append(jnp.exp(glast))
        upds.append(_mm_tn(v[rows], k[rows] * jnp.exp(glast - gc)))
    st = state[...]
    st_in = []
    for c in range(nc):
        st_in.append(st)
        st = st * decs[c] + upds[c]
    state[...] = st
    inter = [jnp.concatenate(
        [_mm_nt(jnp.where(head_of_lane == h, qgs[c], 0.0), st_in[c][h * GLA_DV:(h + 1) * GLA_DV, :])
         for h in range(GLA_HEADS)], axis=1) for c in range(nc)]
    o = o + jnp.concatenate(inter, axis=0)
    r = r_ref[...]
    nw = nw_ref[...]
    outs = []
    for h in range(GLA_HEADS):
        sl = slice(h * GLA_DV, (h + 1) * GLA_DV)
        outs.append(_rms_rows(o[:, sl], nw[:, sl]) * _silu(r[:, sl]))
    o_ref[...] = jnp.concatenate(outs, axis=1).astype(BF16)


def _gla_call(proj, p, bsz, seq):
    nt = seq // GLA_T
    row = lambda blk: pl.BlockSpec((GLA_T, 512), lambda b, t: (b * nt + t, blk))
    full = lambda shape: pl.BlockSpec(shape, lambda b, t: (0,) * len(shape))
    return pl.pallas_call(
        _gla_kernel,
        grid=(bsz, nt),
        in_specs=[row(U512["gla_qk"]), row(U512["gla_v"]), row(U512["gla_r"]),
                  pl.BlockSpec((GLA_T, 256), lambda b, t: (b * nt + t, U256_SMALL)),
                  full((SMALL_W, GLA_QK)), full((1, GLA_QK)), full((1, 512))],
        out_specs=pl.BlockSpec((GLA_T, 512), lambda b, t: (b * nt + t, 0)),
        out_shape=jax.ShapeDtypeStruct((bsz * seq, BRANCH), BF16),
        scratch_shapes=[pltpu.VMEM((BRANCH, GLA_QK), F32),
                        pltpu.VMEM((GLA_CHUNK * GLA_T, GLA_QK), BF16),
                        pltpu.VMEM((GLA_CHUNK * GLA_T, BRANCH), F32)],
        compiler_params=_params(("arbitrary", "arbitrary")),
    )(proj, proj, proj, proj, p["wg"], p["bg"], p["nw"])


LRU_T = 256
LRU_PAD = LRU_T // 2


def _lru_kernel(x_ref, gate_ref, cw_ref, cb_ref, wa_ref, ba_ref, wx_ref, bx_ref, lam_ref, o_ref,
                ext, abuf, hbuf, carry):
    T = LRU_T
    P = LRU_PAD
    t = pl.program_id(1)

    @pl.when(t == 0)
    def _():
        ext[0:SUBLANES, :] = jnp.zeros((SUBLANES, BRANCH), F32)
        abuf[0:P, :] = jnp.ones((P, BRANCH), F32)
        hbuf[0:P, :] = jnp.zeros((P, BRANCH), F32)
        carry[...] = jnp.zeros_like(carry)

    ext[SUBLANES:SUBLANES + T, :] = x_ref[...]
    xc = _causal_conv(ext, cw_ref, T) + cb_ref[...]
    ext[0:SUBLANES, :] = ext[T:T + SUBLANES, :]
    gate_r = _sigmoid(_mm(xc, wa_ref[...]) + ba_ref[...])
    gate_i = _sigmoid(_mm(xc, wx_ref[...]) + bx_ref[...])
    log_a = -LRU_C * gate_r * _softplus(-lam_ref[...])
    abuf[P:P + T, :] = jnp.exp(log_a)
    th = jnp.tanh(log_a)
    hbuf[P:P + T, :] = jnp.sqrt(-2.0 * th / (1.0 - th)) * (gate_i * xc)
    s = 1
    while s < T:
        a_cur = abuf[P:P + T, :]
        h_cur = hbuf[P:P + T, :]
        a_sh = abuf[P - s:P - s + T, :]
        h_sh = hbuf[P - s:P - s + T, :]
        hbuf[P:P + T, :] = h_cur + a_cur * h_sh
        abuf[P:P + T, :] = a_cur * a_sh
        s *= 2
    h = hbuf[P:P + T, :] + abuf[P:P + T, :] * carry[0:1, :]
    carry[0:1, :] = h[T - 1:T, :]
    o_ref[...] = (h * _gelu(gate_ref[...])).astype(BF16)


def _lru_call(proj, p, bsz, seq):
    nt = seq // LRU_T
    row = lambda blk: pl.BlockSpec((LRU_T, 512), lambda b, t: (b * nt + t, blk))
    full = lambda shape: pl.BlockSpec(shape, lambda b, t: (0,) * len(shape))
    return pl.pallas_call(
        _lru_kernel,
        grid=(bsz, nt),
        in_specs=[row(U512["lru_x"]), row(U512["lru_gate"]),
                  full((CONV_W, 512)), full((1, 512)), full((512, 512)), full((1, 512)),
                  full((512, 512)), full((1, 512)), full((1, 512))],
        out_specs=pl.BlockSpec((LRU_T, 512), lambda b, t: (b * nt + t, 0)),
        out_shape=jax.ShapeDtypeStruct((bsz * seq, BRANCH), BF16),
        scratch_shapes=[pltpu.VMEM((SUBLANES + LRU_T, 512), F32),
                        pltpu.VMEM((LRU_PAD + LRU_T, 512), F32),
                        pltpu.VMEM((LRU_PAD + LRU_T, 512), F32),
                        pltpu.VMEM((SUBLANES, 512), F32)],
        compiler_params=_params(("arbitrary", "arbitrary")),
    )(proj, proj, p["cw"], p["cb"], p["wa"], p["ba"], p["wx"], p["bx"], p["lam"])


MRG_TM = 512


def _merge_kernel(x_ref, g_ref, lg_ref, y0_ref, y1_ref, y2_ref, y3_ref, wb_ref, wo_ref, o_ref):
    merged = None
    for i, y_ref in enumerate((y0_ref, y1_ref, y2_ref, y3_ref)):
        br = jnp.dot(y_ref[...], wb_ref[i], preferred_element_type=F32)
        term = _sigmoid(lg_ref[:, i * D_MODEL:(i + 1) * D_MODEL]) * br
        merged = term if merged is None else merged + term
    out = jnp.dot(merged.astype(BF16), wo_ref[...], preferred_element_type=F32)
    o_ref[...] = x_ref[...] + g_ref[0] * out


def _merge_call(x2, g1, proj, ys, wb, wo, seq):
    n_tok = x2.shape[0]
    per_b = seq // MRG_TM
    yspec = pl.BlockSpec((MRG_TM, BRANCH), lambda i: (i, 0))
    return pl.pallas_call(
        _merge_kernel,
        grid=(n_tok // MRG_TM,),
        in_specs=[pl.BlockSpec((MRG_TM, D_MODEL), lambda i: (i, 0)),
                  pl.BlockSpec((1, 1, D_MODEL), lambda i: (i // per_b, 0, 0)),
                  pl.BlockSpec((MRG_TM, N_BRANCH * D_MODEL), lambda i: (i, 0)),
                  yspec, yspec, yspec, yspec,
                  pl.BlockSpec((N_BRANCH, BRANCH, D_MODEL), lambda i: (0, 0, 0)),
                  pl.BlockSpec((D_MODEL, D_MODEL), lambda i: (0, 0))],
        out_specs=pl.BlockSpec((MRG_TM, D_MODEL), lambda i: (i, 0)),
        out_shape=jax.ShapeDtypeStruct((n_tok, D_MODEL), F32),
        compiler_params=_params(("parallel",)),
    )(x2, g1, proj, *ys, wb, wo)


PS_TB = 512
N_SCORE_ROWS = 2 * PEER_HEADS * PEER_KEYS


def _peer_score_kernel(x_ref, nw_ref, sc_ref, sh_ref, wqt_ref, keys_ref, h2t_ref, st_ref):
    h = _rms_rows(x_ref[...], nw_ref[...])
    h = h * (1.0 + sc_ref[0]) + sh_ref[0]
    ht = h.T.astype(BF16)
    h2t_ref[...] = ht
    qt = jnp.dot(wqt_ref[...], ht, preferred_element_type=F32).astype(BF16)
    for g in range(2 * PEER_HEADS):
        rows = slice(g * PEER_KEYS, (g + 1) * PEER_KEYS)
        st_ref[rows, :] = jnp.dot(keys_ref[g], qt[rows, :], preferred_element_type=F32)


def _peer_score_call(x2, nw, sc, sh, wqt, keys, seq):
    n_tok = x2.shape[0]
    per_b = seq // PS_TB
    return pl.pallas_call(
        _peer_score_kernel,
        grid=(n_tok // PS_TB,),
        in_specs=[pl.BlockSpec((PS_TB, D_MODEL), lambda i: (i, 0)),
                  pl.BlockSpec((1, D_MODEL), lambda i: (0, 0)),
                  pl.BlockSpec((1, 1, D_MODEL), lambda i: (i // per_b, 0, 0)),
                  pl.BlockSpec((1, 1, D_MODEL), lambda i: (i // per_b, 0, 0)),
                  pl.BlockSpec((N_SCORE_ROWS, D_MODEL), lambda i: (0, 0)),
                  pl.BlockSpec((2 * PEER_HEADS, PEER_KEYS, PEER_HALF), lambda i: (0, 0, 0))],
        out_specs=[pl.BlockSpec((D_MODEL, PS_TB), lambda i: (0, i)),
                   pl.BlockSpec((N_SCORE_ROWS, PS_TB), lambda i: (0, i))],
        out_shape=[jax.ShapeDtypeStruct((D_MODEL, n_tok), BF16),
                   jax.ShapeDtypeStruct((N_SCORE_ROWS, n_tok), F32)],
        compiler_params=_params(("parallel",)),
    )(x2, nw, sc, sh, wqt, keys)


PT_TL = 128
NOT_RANKED = 255.0
_CAND = [(i, j) for i in range(PEER_TOPK) for j in range(PEER_TOPK) if (i + 1) * (j + 1) <= PEER_TOPK]
N_CAND_ROWS = -(-len(_CAND) // SUBLANES) * SUBLANES


def _pop_max(x, iota):
    m = jnp.max(x, axis=0, keepdims=True)
    first = jnp.min(jnp.where(x == m, iota, float(x.shape[0])), axis=0, keepdims=True)
    return m, first, jnp.where(iota == first, NEG_INF, x)


def _gate_tables_head(st_ref, ce_ref, r2_ref, e2_ref, cand, h, exact_ties):
    TL = PT_TL
    iota_k = lax.broadcasted_iota(jnp.int32, (PEER_KEYS, TL), 0).astype(F32)
    iota_c = lax.broadcasted_iota(jnp.int32, (N_CAND_ROWS, TL), 0).astype(F32)
    rows1 = slice(h * PEER_KEYS, (h + 1) * PEER_KEYS)
    rows2 = slice((PEER_HEADS + h) * PEER_KEYS, (PEER_HEADS + h + 1) * PEER_KEYS)
    s1 = st_ref[rows1, :]
    s2 = st_ref[rows2, :]
    x = s1
    t1, pick1 = [], []
    for _ in range(PEER_TOPK):
        if exact_ties:
            m, f, x = _pop_max(x, iota_k)
            pick1.append(f)
        else:
            m = jnp.max(x, axis=0, keepdims=True)
            x = jnp.where(x == m, NEG_INF, x)
        t1.append(m)
    x = s2
    t2 = []
    rank2 = jnp.full((PEER_KEYS, TL), NOT_RANKED, F32)
    for r in range(PEER_TOPK):
        if exact_ties:
            m, f, x = _pop_max(x, iota_k)
            hit_rows = iota_k == f
        else:
            m = jnp.max(x, axis=0, keepdims=True)
            hit_rows = x == m
            x = jnp.where(hit_rows, NEG_INF, x)
        t2.append(m)
        rank2 = jnp.where(hit_rows, float(r), rank2)
    for n, (i, j) in enumerate(_CAND):
        cand[n:n + 1, :] = t1[i] + t2[j]
    c = cand[...]
    x = c
    tau = None
    for _ in range(PEER_TOPK):
        tau, _, x = _pop_max(x, iota_c)
    m1, m2 = t1[0], t2[0]
    zsum = jnp.sum(jnp.where(c >= tau, jnp.exp(c - (m1 + m2)), 0.0), axis=0, keepdims=True)
    count1 = jnp.zeros((PEER_KEYS, TL), F32)
    for i in range(PEER_TOPK):
        cnt = None
        for j in range(PEER_TOPK):
            if (i + 1) * (j + 1) <= PEER_TOPK:
                hit = jnp.where(t1[i] + t2[j] >= tau, 1.0, 0.0)
                cnt = hit if cnt is None else cnt + hit
        sel = (iota_k == pick1[i]) if exact_ties else (s1 == t1[i])
        count1 = jnp.where(sel, cnt, count1)
    ce_ref[rows1, :] = count1
    ce_ref[rows2, :] = jnp.exp(s1 - m1) * (0.5 / zsum)
    r2_ref[rows1, :] = rank2.astype(BF16)
    e2_ref[rows1, :] = jnp.exp(s2 - m2).astype(BF16)
    if exact_ties:
        return None
    n1 = jnp.sum(jnp.where(s1 >= t1[-1], 1.0, 0.0), axis=0, keepdims=True)
    n2 = jnp.sum(jnp.where(s2 >= t2[-1], 1.0, 0.0), axis=0, keepdims=True)
    return jnp.abs(n1 - PEER_TOPK) + jnp.abs(n2 - PEER_TOPK)


def _peer_gate_kernel(st_ref, ce_ref, r2_ref, e2_ref, cand):
    cand[...] = jnp.full((N_CAND_ROWS, PT_TL), NEG_INF, F32)
    tied = None
    for h in range(PEER_HEADS):
        t = _gate_tables_head(st_ref, ce_ref, r2_ref, e2_ref, cand, h, exact_ties=False)
        tied = t if tied is None else tied + t

    @pl.when(jnp.max(tied) > 0.0)
    def _():
        for h in range(PEER_HEADS):
            _gate_tables_head(st_ref, ce_ref, r2_ref, e2_ref, cand, h, exact_ties=True)


def _peer_gate_call(scores_t):
    n_tok = scores_t.shape[1]
    half = PEER_HEADS * PEER_KEYS
    return pl.pallas_call(
        _peer_gate_kernel,
        grid=(n_tok // PT_TL,),
        in_specs=[pl.BlockSpec((N_SCORE_ROWS, PT_TL), lambda i: (0, i))],
        out_specs=[pl.BlockSpec((N_SCORE_ROWS, PT_TL), lambda i: (0, i)),
                   pl.BlockSpec((half, PT_TL), lambda i: (0, i)),
                   pl.BlockSpec((half, PT_TL), lambda i: (0, i))],
        out_shape=[jax.ShapeDtypeStruct((N_SCORE_ROWS, n_tok), F32),
                   jax.ShapeDtypeStruct((half, n_tok), BF16),
                   jax.ShapeDtypeStruct((half, n_tok), BF16)],
        scratch_shapes=[pltpu.VMEM((N_CAND_ROWS, PT_TL), F32)],
        compiler_params=_params(("parallel",)),
    )(scores_t)


PE_TB = 512
PE_EB = 2048
HALF_ROWS = PEER_HEADS * PEER_KEYS


def _bcast_rows_bf16(row):
    r16 = jnp.broadcast_to(row, (2 * SUBLANES, LANES)).astype(BF16)
    return jnp.concatenate([r16] * (PEER_KEYS // (2 * SUBLANES)), axis=0)


def _peer_expert_kernel(h2t_ref, u_ref, vt_ref, ce_ref, r2_ref, e2_ref, x_ref, g_ref, fw_ref,
                        o_ref, acc, zt, *, final_norm):
    j = pl.program_id(1)
    H = PEER_HEADS

    @pl.when(j == 0)
    def _():
        acc[...] = jnp.zeros_like(acc)

    c0 = math.sqrt(2.0 / math.pi)
    zero = jnp.zeros((PEER_KEYS, LANES), BF16)
    st = jnp.dot(u_ref[0], h2t_ref[...], preferred_element_type=F32)
    for q in range(PE_EB // PEER_KEYS):
        rs = slice(q * PEER_KEYS, (q + 1) * PEER_KEYS)
        for lc in range(PE_TB // LANES):
            ls = slice(lc * LANES, (lc + 1) * LANES)
            w = None
            for h in range(H):
                rows = slice(h * PEER_KEYS, (h + 1) * PEER_KEYS)
                cnt = _bcast_rows_bf16(ce_ref[0, h, q:q + 1, ls])
                g1 = _bcast_rows_bf16(ce_ref[1, h, q:q + 1, ls])
                sel_g1 = jnp.minimum(jnp.maximum(cnt - r2_ref[rows, ls], zero), g1)
                term = sel_g1 * e2_ref[rows, ls]
                w = term if w is None else w + term
            x = st[rs, ls]
            th = jnp.tanh(x * (c0 + (c0 * 0.044715) * (x * x)))
            zt[rs, ls] = (x + x * th).astype(BF16) * w
    acc[...] += jnp.dot(vt_ref[0], zt[...], preferred_element_type=F32)

    @pl.when(j == pl.num_programs(1) - 1)
    def _():
        xn = x_ref[...] + g_ref[0] * acc[...].T
        if final_norm:
            xn = _rms_rows(xn, fw_ref[...])
        o_ref[...] = xn


def _peer_expert_call(h2, u_bf, vt_bf, layer, ce, r2, e2, x2, g2, fw, seq, final_norm):
    n_tok = x2.shape[0]
    per_b = seq // PE_TB
    kern = functools.partial(_peer_expert_kernel, final_norm=final_norm)
    ce_k = ce.reshape(2, PEER_HEADS, PEER_KEYS, n_tok)
    return pl.pallas_call(
        kern,
        grid=(n_tok // PE_TB, PEER_EXPERTS // PE_EB),
        in_specs=[pl.BlockSpec((D_MODEL, PE_TB), lambda i, j: (0, i)),
                  pl.BlockSpec((1, PE_EB, D_MODEL), lambda i, j: (layer, j, 0)),
                  pl.BlockSpec((1, D_MODEL, PE_EB), lambda i, j: (layer, 0, j)),
                  pl.BlockSpec((2, PEER_HEADS, PE_EB // PEER_KEYS, PE_TB), lambda i, j: (0, 0, j, i)),
                  pl.BlockSpec((HALF_ROWS, PE_TB), lambda i, j: (0, i)),
                  pl.BlockSpec((HALF_ROWS, PE_TB), lambda i, j: (0, i)),
                  pl.BlockSpec((PE_TB, D_MODEL), lambda i, j: (i, 0)),
                  pl.BlockSpec((1, 1, D_MODEL), lambda i, j: (i // per_b, 0, 0)),
                  pl.BlockSpec((1, D_MODEL), lambda i, j: (0, 0))],
        out_specs=pl.BlockSpec((PE_TB, D_MODEL), lambda i, j: (i, 0)),
        out_shape=jax.ShapeDtypeStruct((n_tok, D_MODEL), F32),
        scratch_shapes=[pltpu.VMEM((D_MODEL, PE_TB), F32),
                        pltpu.VMEM((PE_EB, PE_TB), BF16)],
        compiler_params=_params(("parallel", "arbitrary")),
    )(h2, u_bf, vt_bf, ce_k, r2, e2, x2, g2, fw)


def _pad_lanes(vec, start, width=SMALL_W):
    out = jnp.zeros((1, width), F32)
    return lax.dynamic_update_slice(out, vec.reshape(1, -1).astype(F32), (0, start))


def _block_diag(w):
    n, d, e = w.shape
    eye = jnp.eye(n, dtype=w.dtype)
    return (eye[:, None, :, None] * w[:, :, None, :]).reshape(n * d, n * e)


def kernel(x, c, w_ada, b_ada, norm_mix_w, norm_ffn_w, w_in, ssm_conv_w, ssm_conv_b, ssm_dt_bias,
           ssm_a_log, ssm_d, ssm_norm_w, gdn_conv_w, gdn_a_log, gdn_dt_bias, gdn_norm_w, gla_w_gate,
           gla_b_gate, gla_norm_w, lru_conv_w, lru_conv_b, lru_w_a, lru_b_a, lru_w_x, lru_b_x,
           lru_lambda, w_branch, w_out, peer_w_q, peer_sub_keys, peer_u, peer_v, final_norm_w):
    bsz, seq, d = x.shape
    n_layers = w_in.shape[0]
    n_tok = bsz * seq
    x2 = x.reshape(n_tok, d)

    c_pad = jnp.zeros((SUBLANES, d), F32).at[:bsz].set(c)
    mod = _ada_call(c_pad, w_ada, b_ada)

    row1 = lambda v: v.reshape(1, -1).astype(F32)
    w_perm = _permute_w_in(w_in)
    u_bf = peer_u.astype(BF16)
    vt_bf = jnp.swapaxes(peer_v, 1, 2).astype(BF16)

    for l in range(n_layers):
        m6 = mod[l, :bsz].reshape(bsz, 6, 1, d)
        sh1, sc1, g1, sh2, sc2, g2 = (m6[:, i] for i in range(6))
        proj = _inproj_call(x2, row1(norm_mix_w[l]), sc1, sh1, w_perm, l, seq)

        ssd_p = dict(cwx=ssm_conv_w[l][:, :512], cbx=row1(ssm_conv_b[l][:512]),
                     cwbc=ssm_conv_w[l][:, 512:], cbbc=row1(ssm_conv_b[l][512:]),
                     dtb=_pad_lanes(ssm_dt_bias[l], SM_DT), alog=_pad_lanes(ssm_a_log[l], SM_DT),
                     dfull=row1(jnp.repeat(ssm_d[l], SSM_HEAD_DIM)), nw=row1(ssm_norm_w[l]))
        y_ssd = _ssd_call(proj, ssd_p, bsz, seq)

        gdn_p = dict(cwq=gdn_conv_w[l][:, :512], cwk=gdn_conv_w[l][:, 512:1024],
                     cwv=gdn_conv_w[l][:, 1024:], dtb=_pad_lanes(gdn_dt_bias[l], SM_DECAY),
                     alog=_pad_lanes(gdn_a_log[l], SM_DECAY),
                     nw=row1(jnp.tile(gdn_norm_w[l], GDN_HEADS)))
        y_gdn = _gdn_call(proj.reshape(bsz, seq, N_COLS), gdn_p, bsz, seq).reshape(n_tok, BRANCH)

        wg = jnp.zeros((SMALL_W, GLA_QK), F32).at[SM_LOW:SM_LOW + GLA_RANK].set(gla_w_gate[l])
        gla_p = dict(wg=wg.astype(BF16), bg=row1(gla_b_gate[l]),
                     nw=row1(jnp.tile(gla_norm_w[l], GLA_HEADS)))
        y_gla = _gla_call(proj, gla_p, bsz, seq)

        lru_p = dict(cw=lru_conv_w[l], cb=row1(lru_conv_b[l]),
                     wa=_block_diag(lru_w_a[l]).astype(BF16), ba=row1(lru_b_a[l]),
                     wx=_block_diag(lru_w_x[l]).astype(BF16), bx=row1(lru_b_x[l]),
                     lam=row1(lru_lambda[l]))
        y_lru = _lru_call(proj, lru_p, bsz, seq)

        x2 = _merge_call(x2, g1, proj, (y_ssd, y_gdn, y_gla, y_lru),
                         w_branch[l].astype(BF16), w_out[l].astype(BF16), seq)

        wqt = peer_w_q[l].reshape(d, PEER_HEADS, 2, PEER_HALF).transpose(2, 1, 3, 0)
        wqt = wqt.reshape(N_SCORE_ROWS, d).astype(BF16)
        keys = peer_sub_keys[l].transpose(1, 0, 2, 3).reshape(2 * PEER_HEADS, PEER_KEYS, PEER_HALF)
        h2, scores_t = _peer_score_call(x2, row1(norm_ffn_w[l]), sc2, sh2, wqt, keys.astype(BF16), seq)
        ce, r2, e2 = _peer_gate_call(scores_t)
        x2 = _peer_expert_call(h2, u_bf, vt_bf, l, ce, r2, e2,
                               x2, g2, row1(final_norm_w), seq, final_norm=(l == n_layers - 1))
    return x2.reshape(bsz, seq, d)
```
